```python
import math
import jax, jax.numpy as jnp
from jax import lax
import numpy as np

D_MODEL = 1024
BATCH = 8
SEQ = 4096
DEPTH = 1

MIX_WIDTH = D_MODEL
ATTN_WIDTH = D_MODEL // 2
HGRN_WIDTH = MIX_WIDTH - ATTN_WIDTH
DIFF_HEADS = 4
DIFF_VDIM = ATTN_WIDTH // DIFF_HEADS
DIFF_QK = DIFF_VDIM // 2
Q_BLOCK = 128
HGRN_HEADS = 4
HGRN_DIM = HGRN_WIDTH // HGRN_HEADS
HGRN_CHUNK = 64
PEER_HEADS = 8
N_KEYS = 128
N_EXPERTS = N_KEYS * N_KEYS
PEER_TOPK = 16
PEER_QDIM = 256
PEER_QHALF = PEER_QDIM // 2
PEER_TOKEN_BLOCK = 128
N_MOD = 6
EPS = 1e-6
SPLIT_SIZES = [DIFF_HEADS * 2 * DIFF_QK, DIFF_HEADS * 2 * DIFF_QK, ATTN_WIDTH,
               HGRN_WIDTH, HGRN_WIDTH, HGRN_WIDTH, HGRN_WIDTH]
IN_COLS = sum(SPLIT_SIZES)
SPLIT_POINTS = [int(v) for v in np.cumsum(SPLIT_SIZES)[:-1]]

kernel_name = "hybrid_diffattn_hgrn2_peer_block"


def rmsnorm(x, g):
    xf = x.astype(jnp.float32)
    y = xf * lax.rsqrt(jnp.mean(xf * xf, axis=-1, keepdims=True) + EPS)
    return (y * g.astype(jnp.float32)).astype(x.dtype)


def diff_attention(q, k, v, lam):
    B, H, _, S, d = q.shape
    nblk = S // Q_BLOCK
    scale = 1.0 / math.sqrt(d)
    qb = q.reshape(B, H, 2, nblk, Q_BLOCK, d).transpose(3, 0, 1, 2, 4, 5)
    kpos = jnp.arange(S)
    vf = v.astype(jnp.float32)

    def one_block(args):
        qi, i = args
        s = jnp.einsum('bhjqd,bhjkd->bhjqk', qi, k).astype(jnp.float32) * scale
        qpos = i * Q_BLOCK + jnp.arange(Q_BLOCK)
        mask = kpos[None, :] <= qpos[:, None]
        s = jnp.where(mask, s, -jnp.inf)
        p = jax.nn.softmax(s, axis=-1)
        a = p[:, :, 0] - lam * p[:, :, 1]
        return jnp.einsum('bhqk,bhkv->bhqv', a, vf)

    out = lax.map(one_block, (qb, jnp.arange(nblk)))
    return out.transpose(1, 2, 0, 3, 4).reshape(B, H, S, v.shape[-1])


def hgrn2_chunkwise(q, k, v, logf):
    B, H, S, dk = q.shape
    dv = v.shape[-1]
    C = HGRN_CHUNK
    n = S // C

    def to_chunks(t):
        return t.reshape(B, H, n, C, t.shape[-1]).transpose(2, 0, 1, 3, 4)

    qc, kc, vc, gc = to_chunks(q), to_chunks(k), to_chunks(v.astype(jnp.float32)), to_chunks(logf)
    bc = jnp.cumsum(gc, axis=-2)
    causal = jnp.tril(jnp.ones((C, C), dtype=bool))[..., None]

    def step(state, inp):
        q_, k_, v_, b_ = inp
        diff = b_[..., :, None, :] - b_[..., None, :, :]
        decay = jnp.exp(jnp.where(causal, diff, -jnp.inf))
        attn = jnp.einsum('bhtk,bhtsk,bhsk->bhts', q_, decay, k_)
        o_intra = jnp.einsum('bhts,bhsv->bhtv', attn, v_)
        o_inter = jnp.einsum('bhtk,bhkv->bhtv', q_ * jnp.exp(b_), state)
        b_last = b_[..., -1:, :]
        new_state = (jnp.exp(b_last[..., 0, :])[..., None] * state
                     + jnp.einsum('bhsk,bhsv->bhkv', k_ * jnp.exp(b_last - b_), v_))
        return new_state, o_intra + o_inter

    state0 = jnp.zeros((B, H, dk, dv), jnp.float32)
    _, ys = lax.scan(step, state0, (qc, kc, vc, bc))
    return ys.transpose(1, 2, 0, 3, 4).reshape(B, H, S, dv)


def hybrid_mixer(h, w_in, lam_qk, g_diff_sub, lb, g_hgrn_out, w_out, lam_init):
    B, S, _ = h.shape
    proj = h @ w_in
    aq, ak, av, hq, hf, hi, hg = jnp.split(proj, SPLIT_POINTS, axis=-1)

    aq = aq.reshape(B, S, DIFF_HEADS, 2, DIFF_QK).transpose(0, 2, 3, 1, 4)
    ak = ak.reshape(B, S, DIFF_HEADS, 2, DIFF_QK).transpose(0, 2, 3, 1, 4)
    av = av.reshape(B, S, DIFF_HEADS, DIFF_VDIM).transpose(0, 2, 1, 3)
    lq = lam_qk.astype(jnp.float32)
    lam = jnp.exp(jnp.sum(lq[0] * lq[1])) - jnp.exp(jnp.sum(lq[2] * lq[3])) + lam_init
    ao = diff_attention(aq, ak, av, lam)
    ao = rmsnorm(ao, g_diff_sub) * (1.0 - lam_init)
    ao = ao.transpose(0, 2, 1, 3).reshape(B, S, ATTN_WIDTH)

    def heads(t):
        return t.reshape(B, S, HGRN_HEADS, HGRN_DIM).transpose(0, 2, 1, 3)
    lbf = lb.astype(jnp.float32)
    f = lbf + (1.0 - lbf) * jax.nn.sigmoid(hf.astype(jnp.float32))
    logf = jnp.log(f)
    kk = 1.0 - f
    qh = jax.nn.silu(hq.astype(jnp.float32))
    ho = hgrn2_chunkwise(heads(qh), heads(kk), heads(hi), heads(logf))
    ho = rmsnorm(ho, g_hgrn_out).transpose(0, 2, 1, 3).reshape(B, S, HGRN_WIDTH)
    ho = ho * jax.nn.silu(hg.astype(jnp.float32))

    cat = jnp.concatenate([ao.astype(jnp.float32), ho], axis=-1).astype(h.dtype)
    return cat @ w_out


def peer(h, w_pq, sub_keys, expert_u, expert_v):
    B, S, D = h.shape
    q = (h @ w_pq).reshape(B, S, PEER_HEADS, 2, PEER_QHALF)
    s = jnp.einsum('bshjd,hjnd->bshjn', q, sub_keys).astype(jnp.float32)
    v1, i1 = lax.top_k(s[..., 0, :], PEER_TOPK)
    v2, i2 = lax.top_k(s[..., 1, :], PEER_TOPK)
    cand = (v1[..., :, None] + v2[..., None, :]).reshape(B, S, PEER_HEADS, PEER_TOPK * PEER_TOPK)
    cidx = (i1[..., :, None] * N_KEYS + i2[..., None, :]).reshape(B, S, PEER_HEADS, PEER_TOPK * PEER_TOPK)
    top_s, pos = lax.top_k(cand, PEER_TOPK)
    idx = jnp.take_along_axis(cidx, pos, axis=-1)
    gate = jax.nn.softmax(top_s, axis=-1)

    T = B * S
    nb = T // PEER_TOKEN_BLOCK
    hb_all = h.reshape(nb, PEER_TOKEN_BLOCK, D)
    ib_all = idx.reshape(nb, PEER_TOKEN_BLOCK, PEER_HEADS * PEER_TOPK)
    gb_all = gate.reshape(nb, PEER_TOKEN_BLOCK, PEER_HEADS * PEER_TOPK)

    def one_block(args):
        hb, ib, gb = args
        u = jnp.take(expert_u, ib, axis=0)
        a = jnp.einsum('tkd,td->tk', u, hb).astype(jnp.float32)
        w = gb * jax.nn.gelu(a, approximate=False)
        vsel = jnp.take(expert_v, ib, axis=0)
        return jnp.einsum('tk,tkd->td', w.astype(vsel.dtype), vsel)

    out = lax.map(one_block, (hb_all, ib_all, gb_all))
    return out.reshape(B, S, D)


def setup_inputs(seed: int = 0) -> dict:
    key = jax.random.key(seed)
    ks = jax.random.split(key, 20)
    f32 = jnp.float32
    D = D_MODEL
    nrm = lambda k, shp, s: jax.random.normal(k, shp, f32) * s
    gain = lambda k, shp: 1.0 + 0.02 * jax.random.normal(k, shp, f32)
    return {
        "x": nrm(ks[0], (BATCH, SEQ, D), 1.0),
        "c": nrm(ks[1], (BATCH, D), 1.0),
        "w_ada": nrm(ks[2], (DEPTH, D, N_MOD * D), 0.5 * D ** -0.5),
        "b_ada": nrm(ks[3], (DEPTH, N_MOD * D), 0.02),
        "g_pre_mix": gain(ks[4], (DEPTH, D)),
        "g_post_mix": gain(ks[5], (DEPTH, D)),
        "g_pre_ffn": gain(ks[6], (DEPTH, D)),
        "g_post_ffn": gain(ks[7], (DEPTH, D)),
        "w_in": nrm(ks[8], (DEPTH, D, IN_COLS), D ** -0.5),
        "lam_qk": nrm(ks[9], (DEPTH, 4, DIFF_QK), 0.1),
        "g_diff_sub": gain(ks[10], (DEPTH, DIFF_VDIM)),
        "lb_theta": nrm(ks[11], (DEPTH + 1, HGRN_WIDTH), 0.1),
        "g_hgrn_out": gain(ks[12], (DEPTH, HGRN_DIM)),
        "w_out": nrm(ks[13], (DEPTH, MIX_WIDTH, D), MIX_WIDTH ** -0.5),
        "w_pq": nrm(ks[14], (DEPTH, D, PEER_HEADS * PEER_QDIM), D ** -0.5),
        "sub_keys": nrm(ks[15], (DEPTH, PEER_HEADS, 2, N_KEYS, PEER_QHALF), PEER_QHALF ** -0.5),
        "expert_u": nrm(ks[16], (DEPTH, N_EXPERTS, D), D ** -0.5),
        "expert_v": nrm(ks[17], (DEPTH, N_EXPERTS, D), D ** -0.5),
    }


def reference(x, c, w_ada, b_ada, g_pre_mix, g_post_mix, g_pre_ffn, g_post_ffn, w_in, lam_qk,
              g_diff_sub, lb_theta, g_hgrn_out, w_out, w_pq, sub_keys, expert_u, expert_v):
    lb_all = jnp.cumsum(jax.nn.softmax(lb_theta.astype(jnp.float32), axis=0), axis=0)
    c_act = jax.nn.silu(c)
    for l in range(DEPTH):
        lam_init = 0.8 - 0.6 * math.exp(-0.3 * l)
        mod = (c_act @ w_ada[l] + b_ada[l])[:, None, :]
        sh1, sc1, gt1, sh2, sc2, gt2 = jnp.split(mod, N_MOD, axis=-1)
        h = rmsnorm(x, g_pre_mix[l]) * (1.0 + sc1) + sh1
        y = hybrid_mixer(h, w_in[l], lam_qk[l], g_diff_sub[l], lb_all[l], g_hgrn_out[l], w_out[l], lam_init)
        x = x + gt1 * rmsnorm(y, g_post_mix[l])
        h = rmsnorm(x, g_pre_ffn[l]) * (1.0 + sc2) + sh2
        y = peer(h, w_pq[l], sub_keys[l], expert_u[l], expert_v[l])
        x = x + gt2 * rmsnorm(y, g_post_ffn[l])
    return x
```

```python
import functools
import math

import jax
import jax.numpy as jnp
from jax import lax
from jax.experimental import pallas as pl
from jax.experimental.pallas import tpu as pltpu

EPS = 1e-6
N_MOD = 6
DIFF_HEADS = 4
DIFF_QK = 64
HEAD_W = 128
HGRN_HEADS = 4
HGRN_CHUNK = 64
HGRN_SUB = 16
PEER_HEADS = 8
N_KEYS = 128
PEER_TOPK = 16
VMEM_LIMIT = 56 * 1024 * 1024

_HI = lax.Precision.HIGHEST
_NEG_INF = float("-inf")


def _dot(a, b, dims, precision=None):
    return lax.dot_general(a, b, (dims, ((), ())), precision=precision,
                           preferred_element_type=jnp.float32)


def _mm(a, b, precision=None):
    return _dot(a, b, ((1,), (0,)), precision)


def _mm_nt(a, b, precision=None):
    return _dot(a, b, ((1,), (1,)), precision)


def _mm_tn(a, b, precision=None):
    return _dot(a, b, ((0,), (0,)), precision)


def _rms(x, g):
    return x * lax.rsqrt(jnp.mean(x * x, axis=-1, keepdims=True) + EPS) * g


def _sigmoid(x):
    return 1.0 / (1.0 + jnp.exp(-x))


def _mod_kernel(c_ref, w_ref, b_ref, o_ref):
    c = c_ref[...]
    ca = c * _sigmoid(c)
    o_ref[0] = _mm(ca, w_ref[...], _HI) + b_ref[...]


def _modulation(c, w_ada, b_ada):
    B, D = c.shape
    return pl.pallas_call(
        _mod_kernel,
        grid=(N_MOD,),
        in_specs=[pl.BlockSpec((B, D), lambda j: (0, 0)),
                  pl.BlockSpec((D, D), lambda j: (0, j)),
                  pl.BlockSpec((1, D), lambda j: (0, j))],
        out_specs=pl.BlockSpec((1, B, D), lambda j: (j, 0, 0)),
        out_shape=jax.ShapeDtypeStruct((N_MOD, B, D), jnp.float32),
        compiler_params=pltpu.CompilerParams(vmem_limit_bytes=VMEM_LIMIT),
        name="adaln_mod",
    )(c, w_ada, b_ada.reshape(1, N_MOD * D))


def _inproj_kernel(x_ref, g_ref, sc_ref, sh_ref, w_ref, attn_ref, hg_ref, *, n_attn, col_chunk):
    x = x_ref[0]
    h = _rms(x, g_ref[...]) * (1.0 + sc_ref[0]) + sh_ref[0]
    hb = h.astype(jnp.bfloat16)
    n_cols = w_ref.shape[1]
    for c0 in range(0, n_cols, col_chunk):
        r = _mm(hb, w_ref[:, c0:c0 + col_chunk])
        if c0 < n_attn:
            attn_ref[0, :, c0:c0 + col_chunk] = r.astype(attn_ref.dtype)
        else:
            hg_ref[0, :, c0 - n_attn:c0 - n_attn + col_chunk] = r


def _in_projection(x, g, sc, sh, w_in_bf16, n_attn, tm):
    B, S, D = x.shape
    n_cols = w_in_bf16.shape[1]
    vec = pl.BlockSpec((1, 1, D), lambda b, i: (b, 0, 0))
    return pl.pallas_call(
        functools.partial(_inproj_kernel, n_attn=n_attn, col_chunk=512),
        grid=(B, S // tm),
        in_specs=[pl.BlockSpec((1, tm, D), lambda b, i: (b, i, 0)),
                  pl.BlockSpec((1, D), lambda b, i: (0, 0)),
                  vec, vec,
                  pl.BlockSpec((D, n_cols), lambda b, i: (0, 0))],
        out_specs=[pl.BlockSpec((1, tm, n_attn), lambda b, i: (b, i, 0)),
                   pl.BlockSpec((1, tm, n_cols - n_attn), lambda b, i: (b, i, 0))],
        out_shape=[jax.ShapeDtypeStruct((B, S, n_attn), jnp.bfloat16),
                   jax.ShapeDtypeStruct((B, S, n_cols - n_attn), jnp.float32)],
        compiler_params=pltpu.CompilerParams(
            dimension_semantics=("parallel", "parallel"), vmem_limit_bytes=VMEM_LIMIT),
        name="prenorm_inproj",
    )(x, g.reshape(1, D), sc, sh, w_in_bf16)


def _attn_kernel(q_ref, k_ref, v_ref, lam_ref, g_ref, o_ref,
                 m1_ref, l1_ref, a1_ref, m2_ref, l2_ref, a2_ref, *, tq, lam_init):
    qi = pl.program_id(2)
    q = q_ref[0]
    lane = lax.broadcasted_iota(jnp.int32, q.shape, 1)
    scale = jnp.asarray(1.0 / math.sqrt(DIFF_QK), q.dtype)
    qs = q * scale
    zero = jnp.zeros_like(qs)
    q1 = jnp.where(lane < DIFF_QK, qs, zero)
    q2 = jnp.where(lane >= DIFF_QK, qs, zero)

    for m_ref, l_ref, a_ref in ((m1_ref, l1_ref, a1_ref), (m2_ref, l2_ref, a2_ref)):
        m_ref[...] = jnp.full(m_ref.shape, _NEG_INF, jnp.float32)
        l_ref[...] = jnp.zeros(l_ref.shape, jnp.float32)
        a_ref[...] = jnp.zeros(a_ref.shape, jnp.float32)

    def step(j, masked):
        kb = k_ref[0, pl.ds(j * tq, tq), :]
        vb = v_ref[0, pl.ds(j * tq, tq), :]
        if masked:
            row = lax.broadcasted_iota(jnp.int32, (tq, tq), 0)
            col = lax.broadcasted_iota(jnp.int32, (tq, tq), 1)
            keep = col <= row
        for qm, m_ref, l_ref, a_ref in ((q1, m1_ref, l1_ref, a1_ref), (q2, m2_ref, l2_ref, a2_ref)):
            s = _mm_nt(qm, kb)
            if masked:
                s = jnp.where(keep, s, _NEG_INF)
            m_old = m_ref[...]
            m_new = jnp.maximum(m_old, jnp.max(s, axis=1, keepdims=True))
            alpha = jnp.exp(m_old - m_new)
            p = jnp.exp(s - m_new)
            l_ref[...] = alpha * l_ref[...] + jnp.sum(p, axis=1, keepdims=True)
            a_ref[...] = alpha * a_ref[...] + _mm(p.astype(vb.dtype), vb)
            m_ref[...] = m_new

    def full_step(j, carry):
        step(j, False)
        return carry

    lax.fori_loop(0, qi, full_step, 0)
    step(qi, True)

    lq = lam_ref[...]
    lam = (jnp.exp(jnp.sum(lq[0:1] * lq[1:2], axis=1, keepdims=True))
           - jnp.exp(jnp.sum(lq[2:3] * lq[3:4], axis=1, keepdims=True)) + lam_init)
    o = a1_ref[...] / l1_ref[...] - lam * (a2_ref[...] / l2_ref[...])
    o = _rms(o, g_ref[...]) * (1.0 - lam_init)
    o_ref[0] = o.astype(o_ref.dtype)


def _diff_attention(attn_in, lam_qk, g_diff_sub, lam_init, tq):
    B, S, _ = attn_in.shape
    H = DIFF_HEADS
    kv_spec = lambda off: pl.BlockSpec((1, S, HEAD_W), lambda b, h, i: (b, 0, off + h))
    return pl.pallas_call(
        functools.partial(_attn_kernel, tq=tq, lam_init=lam_init),
        grid=(B, H, S // tq),
        in_specs=[pl.BlockSpec((1, tq, HEAD_W), lambda b, h, i: (b, i, h)),
                  kv_spec(H), kv_spec(2 * H),
                  pl.BlockSpec(lam_qk.shape, lambda b, h, i: (0, 0)),
                  pl.BlockSpec((1, HEAD_W), lambda b, h, i: (0, 0))],
        out_specs=pl.BlockSpec((1, tq, HEAD_W), lambda b, h, i: (b, i, h)),
        out_shape=jax.ShapeDtypeStruct((B, S, H * HEAD_W), jnp.bfloat16),
        scratch_shapes=[pltpu.VMEM((tq, 1), jnp.float32), pltpu.VMEM((tq, 1), jnp.float32),
                        pltpu.VMEM((tq, HEAD_W), jnp.float32),
                        pltpu.VMEM((tq, 1), jnp.float32), pltpu.VMEM((tq, 1), jnp.float32),
                        pltpu.VMEM((tq, HEAD_W), jnp.float32)],
        compiler_params=pltpu.CompilerParams(
            dimension_semantics=("parallel", "parallel", "arbitrary"), vmem_limit_bytes=VMEM_LIMIT),
        name="diff_attention",
    )(attn_in, attn_in, attn_in, lam_qk, g_diff_sub.reshape(1, HEAD_W))


def _hgrn_kernel(hq_ref, hf_ref, hi_ref, hgate_ref, lbt_ref, g_ref, o_ref, state_ref, *, layer, n_chunks):
    C, SUB = HGRN_CHUNK, HGRN_SUB
    n_sub = C // SUB

    @pl.when(pl.program_id(2) == 0)
    def _():
        state_ref[...] = jnp.zeros(state_ref.shape, jnp.float32)

    th = lbt_ref[...]
    e = jnp.exp(th - jnp.max(th, axis=0, keepdims=True))
    lb = jnp.sum(e[0:layer + 1], axis=0, keepdims=True) / jnp.sum(e, axis=0, keepdims=True)

    r_io = lax.broadcasted_iota(jnp.int32, (C, C), 0)
    c_io = lax.broadcasted_iota(jnp.int32, (C, C), 1)
    tril = (c_io <= r_io).astype(jnp.float32)
    tloc = lax.broadcasted_iota(jnp.int32, (C, HEAD_W), 0) % SUB
    sub_col = lax.broadcasted_iota(jnp.int32, (SUB, C), 1)
    g_out = g_ref[...]

    def group_rows(x, s):
        return jnp.concatenate(
            [jnp.broadcast_to(x[i * SUB + s:i * SUB + s + 1, :], (SUB, HEAD_W)) for i in range(n_sub)], axis=0)

    def chunk(ci, carry):
        rows = pl.ds(pl.multiple_of(ci * C, C), C)
        hq = hq_ref[0, rows, :]
        f = lb + (1.0 - lb) * _sigmoid(hf_ref[0, rows, :])
        glog = jnp.log(f)
        kk = 1.0 - f
        q = hq * _sigmoid(hq)
        v = hi_ref[0, rows, :]
        b = _mm(tril, glog, _HI)
        st = state_ref[...]

        o = _mm_nt(q * jnp.exp(b), st, _HI)

        o_sub = [jnp.zeros((SUB, HEAD_W), jnp.float32)]
        for i in range(1, n_sub):
            beta = b[i * SUB:i * SUB + 1, :]
            qt = q[i * SUB:(i + 1) * SUB, :] * jnp.exp(b[i * SUB:(i + 1) * SUB, :] - beta)
            kt = kk * jnp.exp(jnp.minimum(beta - b, 0.0))
            p = _mm_nt(qt, kt, _HI)
            p = jnp.where(sub_col < i * SUB, p, 0.0)
            o_sub.append(_mm(p, v, _HI))
        o = o + jnp.concatenate(o_sub, axis=0)

        for s in range(SUB):
            b_s, k_s, v_s = group_rows(b, s), group_rows(kk, s), group_rows(v, s)
            w = q * k_s * jnp.exp(jnp.where(tloc >= s, b - b_s, _NEG_INF))
            o = o + jnp.sum(w, axis=1, keepdims=True) * v_s

        b_last = b[C - 1:C, :]
        kdec = kk * jnp.exp(b_last - b)
        state_ref[...] = st * jnp.exp(b_last) + _mm_tn(v, kdec, _HI)

        hgate = hgate_ref[0, rows, :]
        y = _rms(o, g_out) * (hgate * _sigmoid(hgate))
        o_ref[0, rows, :] = y.astype(o_ref.dtype)
        return carry

    lax.fori_loop(0, n_chunks, chunk, 0)


def _hgrn(hg_in, lb_theta, g_hgrn_out, layer, sb):
    B, S, _ = hg_in.shape
    H = HGRN_HEADS
    spec = lambda off: pl.BlockSpec((1, sb, HEAD_W), lambda b, h, i: (b, i, off + h))
    n_slots = lb_theta.shape[0]
    return pl.pallas_call(
        functools.partial(_hgrn_kernel, layer=layer, n_chunks=sb // HGRN_CHUNK),
        grid=(B, H, S // sb),
        in_specs=[spec(0), spec(H), spec(2 * H), spec(3 * H),
                  pl.BlockSpec((n_slots, HEAD_W), lambda b, h, i: (0, h)),
                  pl.BlockSpec((1, HEAD_W), lambda b, h, i: (0, 0))],
        out_specs=pl.BlockSpec((1, sb, HEAD_W), lambda b, h, i: (b, i, h)),
        out_shape=jax.ShapeDtypeStruct((B, S, H * HEAD_W), jnp.bfloat16),
        scratch_shapes=[pltpu.VMEM((HEAD_W, HEAD_W), jnp.float32)],
        compiler_params=pltpu.CompilerParams(
            dimension_semantics=("parallel", "parallel", "arbitrary"), vmem_limit_bytes=VMEM_LIMIT),
        name="hgrn2",
    )(hg_in, hg_in, hg_in, hg_in, lb_theta, g_hgrn_out.reshape(1, HEAD_W))


def _outproj_kernel(ao_ref, ho_ref, x_ref, wo_ref, gpost_ref, gt_ref, gpre_ref, sc_ref, sh_ref, wq_ref,
                    x1_ref, h2_ref, q_ref):
    n_a = ao_ref.shape[2]
    y = _mm(ao_ref[0], wo_ref[0:n_a, :]) + _mm(ho_ref[0], wo_ref[n_a:, :])
    x1 = x_ref[0] + gt_ref[0] * _rms(y, gpost_ref[...])
    x1_ref[0] = x1
    h2 = _rms(x1, gpre_ref[...]) * (1.0 + sc_ref[0]) + sh_ref[0]
    h2_ref[0] = h2
    q_ref[0] = _mm(h2.astype(jnp.bfloat16), wq_ref[...]).astype(q_ref.dtype)


def _out_projection(ao, ho, x, w_out_bf16, g_post, gt1, g_pre, sc2, sh2, w_pq_bf16, tm):
    B, S, D = x.shape
    n_a, n_h, n_q = ao.shape[2], ho.shape[2], w_pq_bf16.shape[1]
    vec = pl.BlockSpec((1, 1, D), lambda b, i: (b, 0, 0))
    par = pl.BlockSpec((1, D), lambda b, i: (0, 0))
    row = lambda n: pl.BlockSpec((1, tm, n), lambda b, i: (b, i, 0))
    return pl.pallas_call(
        _outproj_kernel,
        grid=(B, S // tm),
        in_specs=[row(n_a), row(n_h), row(D),
                  pl.BlockSpec((n_a + n_h, D), lambda b, i: (0, 0)),
                  par, vec, par, vec, vec,
                  pl.BlockSpec((D, n_q), lambda b, i: (0, 0))],
        out_specs=[row(D), row(D), row(n_q)],
        out_shape=[jax.ShapeDtypeStruct((B, S, D), jnp.float32),
                   jax.ShapeDtypeStruct((B, S, D), jnp.float32),
                   jax.ShapeDtypeStruct((B, S, n_q), jnp.bfloat16)],
        compiler_params=pltpu.CompilerParams(
            dimension_semantics=("parallel", "parallel"), vmem_limit_bytes=VMEM_LIMIT),
        name="outproj_norms_peerq",
    )(ao, ho, x, w_out_bf16, g_post.reshape(1, D), gt1, g_pre.reshape(1, D), sc2, sh2, w_pq_bf16)


def _pair_list():
    return [(a, b) for a in range(PEER_TOPK) for b in range(PEER_TOPK) if (a + 1) * (b + 1) <= PEER_TOPK]


def _topk_kernel(q_ref, keys_ref, idx_ref, gate_ref, v_scr, i_scr, cand_scr, cidx_scr, ts_scr, sel_scr):
    K = PEER_TOPK
    tb = q_ref.shape[0]
    pairs = _pair_list()
    n_cand = cand_scr.shape[0]
    kio = lax.broadcasted_iota(jnp.int32, (N_KEYS, tb), 0)
    pio = lax.broadcasted_iota(jnp.int32, (n_cand, tb), 0)

    for h in range(PEER_HEADS):
        for j in range(2):
            c0 = (h * 2 + j) * N_KEYS
            s = _mm_nt(keys_ref[h, j], q_ref[:, c0:c0 + N_KEYS])
            for r in range(K):
                m = jnp.max(s, axis=0, keepdims=True)
                am = jnp.min(jnp.where(s == m, kio, N_KEYS), axis=0, keepdims=True)
                v_scr[j, r:r + 1, :] = m
                i_scr[j, r:r + 1, :] = am
                s = jnp.where(kio == am, _NEG_INF, s)
        v1, v2 = v_scr[0], v_scr[1]
        i1, i2 = i_scr[0], i_scr[1]
        cand_scr[...] = jnp.full(cand_scr.shape, _NEG_INF, jnp.float32)
        cidx_scr[...] = jnp.zeros(cidx_scr.shape, jnp.int32)
        off = 0
        for a in range(K):
            nb = sum(1 for (aa, _) in pairs if aa == a)
            cand_scr[off:off + nb, :] = v1[a:a + 1, :] + v2[0:nb, :]
            cidx_scr[off:off + nb, :] = i1[a:a + 1, :] * N_KEYS + i2[0:nb, :]
            off += nb
        cand = cand_scr[...]
        cidx = cidx_scr[...]
        for r in range(K):
            m = jnp.max(cand, axis=0, keepdims=True)
            pos = jnp.min(jnp.where(cand == m, pio, n_cand), axis=0, keepdims=True)
            hit = pio == pos
            ts_scr[r:r + 1, :] = m
            sel_scr[h * K + r:h * K + r + 1, :] = jnp.sum(jnp.where(hit, cidx, 0), axis=0, keepdims=True)
            cand = jnp.where(hit, _NEG_INF, cand)
        ts = ts_scr[...]
        e = jnp.exp(ts - ts[0:1, :])
        gate_ref[h * K:(h + 1) * K, :] = e / jnp.sum(e, axis=0, keepdims=True)
    idx_ref[...] = sel_scr[...].T


def _peer_topk(q, sub_keys_bf16, tb):
    T = q.shape[0]
    n_sel = PEER_HEADS * PEER_TOPK
    n_cand = -(-len(_pair_list()) // 8) * 8
    return pl.pallas_call(
        _topk_kernel,
        grid=(T // tb,),
        in_specs=[pl.BlockSpec((tb, q.shape[1]), lambda i: (i, 0)),
                  pl.BlockSpec(sub_keys_bf16.shape, lambda i: (0, 0, 0, 0))],
        out_specs=[pl.BlockSpec((tb, n_sel), lambda i: (i, 0)),
                   pl.BlockSpec((n_sel, tb), lambda i: (0, i))],
        out_shape=[jax.ShapeDtypeStruct((T, n_sel), jnp.int32),
                   jax.ShapeDtypeStruct((n_sel, T), jnp.float32)],
        scratch_shapes=[pltpu.VMEM((2, PEER_TOPK, tb), jnp.float32),
                        pltpu.VMEM((2, PEER_TOPK, tb), jnp.int32),
                        pltpu.VMEM((n_cand, tb), jnp.float32),
                        pltpu.VMEM((n_cand, tb), jnp.int32),
                        pltpu.VMEM((PEER_TOPK, tb), jnp.float32),
                        pltpu.VMEM((n_sel, tb), jnp.int32)],
        compiler_params=pltpu.CompilerParams(
            dimension_semantics=("parallel",), vmem_limit_bytes=VMEM_LIMIT),
        name="peer_topk",
    )(q, sub_keys_bf16)


def _peer_kernel(idx_ref, h_ref, gate_ref, x1_ref, gt_ref, g_ref, uv_ref, o_ref, buf, sem, y_scr, *, n_slots):
    tb, D = h_ref.shape
    n_sel = idx_ref.shape[1]

    def row_copy(t, k, slot):
        e = idx_ref[t, k]
        return pltpu.make_async_copy(uv_ref.at[pl.ds(e, 1), :], buf.at[slot, pl.ds(k, 1), :], sem.at[slot])

    def start_token(t):
        slot = t % n_slots
        for k in range(n_sel):
            row_copy(t, k, slot).start()

    def wait_token(t):
        slot = t % n_slots
        pltpu.make_async_copy(uv_ref.at[pl.ds(0, n_sel), :], buf.at[slot], sem.at[slot]).wait()

    for t in range(n_slots - 1):
        start_token(t)

    lane = lax.broadcasted_iota(jnp.int32, gate_ref.shape, 1)

    def token(t, carry):
        @pl.when(t + n_slots - 1 < tb)
        def _():
            start_token(t + n_slots - 1)

        wait_token(t)
        slot = t % n_slots
        h = h_ref[pl.ds(t, 1), :]
        a = jnp.sum(buf[slot, :, 0:D] * h, axis=1, keepdims=True)
        gcol = jnp.sum(jnp.where(lane == t, gate_ref[...], 0.0), axis=1, keepdims=True)
        w = gcol * (0.5 * a * (1.0 + lax.erf(a * (1.0 / math.sqrt(2.0)))))
        y_scr[pl.ds(t, 1), :] = jnp.sum(w * buf[slot, :, D:2 * D], axis=0, keepdims=True)
        return carry

    lax.fori_loop(0, tb, token, 0)
    o_ref[...] = x1_ref[...] + gt_ref[0] * _rms(y_scr[...], g_ref[...])


def _peer_mix(idx, h2, gate_t, x1, gt2, g_post, uv, seq_len, tb, n_slots):
    T, D = h2.shape
    n_sel = idx.shape[1]
    blocks_per_seq = seq_len // tb
    rows = pl.BlockSpec((tb, D), lambda i: (i, 0))
    return pl.pallas_call(
        functools.partial(_peer_kernel, n_slots=n_slots),
        grid=(T // tb,),
        in_specs=[pl.BlockSpec((tb, n_sel), lambda i: (i, 0), memory_space=pltpu.SMEM),
                  rows,
                  pl.BlockSpec((n_sel, tb), lambda i: (0, i)),
                  rows,
                  pl.BlockSpec((1, 1, D), lambda i: (i // blocks_per_seq, 0, 0)),
                  pl.BlockSpec((1, D), lambda i: (0, 0)),
                  pl.BlockSpec(memory_space=pl.ANY)],
        out_specs=rows,
        out_shape=jax.ShapeDtypeStruct((T, D), jnp.float32),
        scratch_shapes=[pltpu.VMEM((n_slots, n_sel, 2 * D), jnp.float32),
                        pltpu.SemaphoreType.DMA((n_slots,)),
                        pltpu.VMEM((tb, D), jnp.float32)],
        compiler_params=pltpu.CompilerParams(
            dimension_semantics=("arbitrary",), vmem_limit_bytes=VMEM_LIMIT),
        name="peer_gather_mix",
    )(idx, h2, gate_t, x1, gt2, g_post.reshape(1, D), uv)


def _pick(n, pref):
    t = min(n, pref)
    assert n % t == 0, (n, pref)
    return t


def kernel(x, c, w_ada, b_ada, g_pre_mix, g_post_mix, g_pre_ffn, g_post_ffn, w_in, lam_qk, g_diff_sub,
           lb_theta, g_hgrn_out, w_out, w_pq, sub_keys, expert_u, expert_v):
    B, S, D = x.shape
    depth = w_in.shape[0]
    n_attn = 3 * DIFF_HEADS * HEAD_W
    tm = _pick(S, 512)
    tq = _pick(S, 256)
    sb = _pick(S, 512)
    tb = _pick(S, 128)
    bf = jnp.bfloat16
    for l in range(depth):
        lam_init = 0.8 - 0.6 * math.exp(-0.3 * l)
        mod = _modulation(c, w_ada[l], b_ada[l]).reshape(N_MOD, B, 1, D)
        sh1, sc1, gt1, sh2, sc2, gt2 = (mod[i] for i in range(N_MOD))
        attn_in, hg_in = _in_projection(x, g_pre_mix[l], sc1, sh1, w_in[l].astype(bf), n_attn, tm)
        ao = _diff_attention(attn_in, lam_qk[l], g_diff_sub[l], lam_init, tq)
        ho = _hgrn(hg_in, lb_theta, g_hgrn_out[l], l, sb)
        x1, h2, q = _out_projection(ao, ho, x, w_out[l].astype(bf), g_post_mix[l], gt1, g_pre_ffn[l],
                                    sc2, sh2, w_pq[l].astype(bf), tm)
        idx, gate_t = _peer_topk(q.reshape(B * S, -1), sub_keys[l].astype(bf), tb)
        uv = jnp.concatenate([expert_u[l], expert_v[l]], axis=1)
        out = _peer_mix(idx, h2.reshape(B * S, D), gate_t, x1.reshape(B * S, D), gt2, g_post_ffn[l], uv,
                        S, tb, n_slots=4)
        x = out.reshape(B, S, D)
    return x
```

```python
import functools
import math

import jax
import jax.numpy as jnp
from jax import lax
from jax.experimental import pallas as pl
from jax.experimental.pallas import tpu as pltpu

EPS = 1e-6
N_MOD = 6
DIFF_HEADS = 4
DIFF_QK = 64
HEAD_W = 128
HGRN_HEADS = 4
HGRN_CHUNK = 64
HGRN_SUB = 16
PEER_HEADS = 8
N_KEYS = 128
PEER_TOPK = 16
VMEM_LIMIT = 56 * 1024 * 1024

_HI = lax.Precision.HIGHEST
_NEG_INF = float("-inf")


def _dot(a, b, dims, precision=None):
    return lax.dot_general(a, b, (dims, ((), ())), precision=precision,
                           preferred_element_type=jnp.float32)


def _mm(a, b, precision=None):
    return _dot(a, b, ((1,), (0,)), precision)


def _mm_nt(a, b, precision=None):
    return _dot(a, b, ((1,), (1,)), precision)


def _mm_tn(a, b, precision=None):
    return _dot(a, b, ((0,), (0,)), precision)


def _rms(x, g):
    return x * lax.rsqrt(jnp.mean(x * x, axis=-1, keepdims=True) + EPS) * g


def _sigmoid(x):
    return 1.0 / (1.0 + jnp.exp(-x))


def _mod_kernel(c_ref, w_ref, b_ref, o_ref):
    c = c_ref[...]
    ca = c * _sigmoid(c)
    o_ref[0] = _mm(ca, w_ref[...], _HI) + b_ref[...]


def _modulation(c, w_ada, b_ada):
    B, D = c.shape
    return pl.pallas_call(
        _mod_kernel,
        grid=(N_MOD,),
        in_specs=[pl.BlockSpec((B, D), lambda j: (0, 0)),
                  pl.BlockSpec((D, D), lambda j: (0, j)),
                  pl.BlockSpec((1, D), lambda j: (0, j))],
        out_specs=pl.BlockSpec((1, B, D), lambda j: (j, 0, 0)),
        out_shape=jax.ShapeDtypeStruct((N_MOD, B, D), jnp.float32),
        compiler_params=pltpu.CompilerParams(vmem_limit_bytes=VMEM_LIMIT),
        name="adaln_mod",
    )(c, w_ada, b_ada.reshape(1, N_MOD * D))


def _inproj_kernel(x_ref, g_ref, sc_ref, sh_ref, w_ref, attn_ref, hg_ref, *, n_attn, col_chunk):
    x = x_ref[0]
    h = _rms(x, g_ref[...]) * (1.0 + sc_ref[0]) + sh_ref[0]
    hb = h.astype(jnp.bfloat16)
    n_cols = w_ref.shape[1]
    for c0 in range(0, n_cols, col_chunk):
        r = _mm(hb, w_ref[:, c0:c0 + col_chunk])
        if c0 < n_attn:
            attn_ref[0, :, c0:c0 + col_chunk] = r.astype(attn_ref.dtype)
        else:
            hg_ref[0, :, c0 - n_attn:c0 - n_attn + col_chunk] = r


def _in_projection(x, g, sc, sh, w_in_bf16, n_attn, tm):
    B, S, D = x.shape
    n_cols = w_in_bf16.shape[1]
    vec = pl.BlockSpec((1, 1, D), lambda b, i: (b, 0, 0))
    return pl.pallas_call(
        functools.partial(_inproj_kernel, n_attn=n_attn, col_chunk=512),
        grid=(B, S // tm),
        in_specs=[pl.BlockSpec((1, tm, D), lambda b, i: (b, i, 0)),
                  pl.BlockSpec((1, D), lambda b, i: (0, 0)),
                  vec, vec,
                  pl.BlockSpec((D, n_cols), lambda b, i: (0, 0))],
        out_specs=[pl.BlockSpec((1, tm, n_attn), lambda b, i: (b, i, 0)),
                   pl.BlockSpec((1, tm, n_cols - n_attn), lambda b, i: (b, i, 0))],
        out_shape=[jax.ShapeDtypeStruct((B, S, n_attn), jnp.bfloat16),
                   jax.ShapeDtypeStruct((B, S, n_cols - n_attn), jnp.float32)],
        compiler_params=pltpu.CompilerParams(
            dimension_semantics=("parallel", "parallel"), vmem_limit_bytes=VMEM_LIMIT),
        name="prenorm_inproj",
    )(x, g.reshape(1, D), sc, sh, w_in_bf16)


def _attn_kernel(q_ref, k_ref, v_ref, lam_ref, g_ref, o_ref,
                 m1_ref, l1_ref, a1_ref, m2_ref, l2_ref, a2_ref, *, tq, lam_init):
    qi = pl.program_id(2)
    q = q_ref[0]
    lane = lax.broadcasted_iota(jnp.int32, q.shape, 1)
    scale = jnp.asarray(1.0 / math.sqrt(DIFF_QK), q.dtype)
    qs = q * scale
    zero = jnp.zeros_like(qs)
    q1 = jnp.where(lane < DIFF_QK, qs, zero)
    q2 = jnp.where(lane >= DIFF_QK, qs, zero)

    for m_ref, l_ref, a_ref in ((m1_ref, l1_ref, a1_ref), (m2_ref, l2_ref, a2_ref)):
        m_ref[...] = jnp.full(m_ref.shape, _NEG_INF, jnp.float32)
        l_ref[...] = jnp.zeros(l_ref.shape, jnp.float32)
        a_ref[...] = jnp.zeros(a_ref.shape, jnp.float32)

    def step(j, masked):
        kb = k_ref[0, pl.ds(j * tq, tq), :]
        vb = v_ref[0, pl.ds(j * tq, tq), :]
        if masked:
            row = lax.broadcasted_iota(jnp.int32, (tq, tq), 0)
            col = lax.broadcasted_iota(jnp.int32, (tq, tq), 1)
            keep = col <= row
        for qm, m_ref, l_ref, a_ref in ((q1, m1_ref, l1_ref, a1_ref), (q2, m2_ref, l2_ref, a2_ref)):
            s = _mm_nt(qm, kb)
            if masked:
                s = jnp.where(keep, s, _NEG_INF)
            m_old = m_ref[...]
            m_new = jnp.maximum(m_old, jnp.max(s, axis=1, keepdims=True))
            alpha = jnp.exp(m_old - m_new)
            p = jnp.exp(s - m_new)
            l_ref[...] = alpha * l_ref[...] + jnp.sum(p, axis=1, keepdims=True)
            a_ref[...] = alpha * a_ref[...] + _mm(p.astype(vb.dtype), vb)
            m_ref[...] = m_new

    def full_step(j, carry):
        step(j, False)
        return carry

    lax.fori_loop(0, qi, full_step, 0)
    step(qi, True)

    lq = lam_ref[...]
    lam = (jnp.exp(jnp.sum(lq[0:1] * lq[1:2], axis=1, keepdims=True))
           - jnp.exp(jnp.sum(lq[2:3] * lq[3:4], axis=1, keepdims=True)) + lam_init)
    o = a1_ref[...] / l1_ref[...] - lam * (a2_ref[...] / l2_ref[...])
    o = _rms(o, g_ref[...]) * (1.0 - lam_init)
    o_ref[0] = o.astype(o_ref.dtype)


def _diff_attention(attn_in, lam_qk, g_diff_sub, lam_init, tq):
    B, S, _ = attn_in.shape
    H = DIFF_HEADS
    kv_spec = lambda off: pl.BlockSpec((1, S, HEAD_W), lambda b, h, i: (b, 0, off + h))
    return pl.pallas_call(
        functools.partial(_attn_kernel, tq=tq, lam_init=lam_init),
        grid=(B, H, S // tq),
        in_specs=[pl.BlockSpec((1, tq, HEAD_W), lambda b, h, i: (b, i, h)),
                  kv_spec(H), kv_spec(2 * H),
                  pl.BlockSpec(lam_qk.shape, lambda b, h, i: (0, 0)),
                  pl.BlockSpec((1, HEAD_W), lambda b, h, i: (0, 0))],
        out_specs=pl.BlockSpec((1, tq, HEAD_W), lambda b, h, i: (b, i, h)),
        out_shape=jax.ShapeDtypeStruct((B, S, H * HEAD_W), jnp.bfloat16),
        scratch_shapes=[pltpu.VMEM((tq, 1), jnp.float32), pltpu.VMEM((tq, 1), jnp.float32),
                        pltpu.VMEM((tq, HEAD_W), jnp.float32),
                        pltpu.VMEM((tq, 1), jnp.float32), pltpu.VMEM((tq, 1), jnp.float32),
                        pltpu.VMEM((tq, HEAD_W), jnp.float32)],
        compiler_params=pltpu.CompilerParams(
            dimension_semantics=("parallel", "parallel", "arbitrary"), vmem_limit_bytes=VMEM_LIMIT),
        name="diff_attention",
    )(attn_in, attn_in, attn_in, lam_qk, g_diff_sub.reshape(1, HEAD_W))


def _hgrn_kernel(hq_ref, hf_ref, hi_ref, hgate_ref, lbt_ref, g_ref, o_ref, state_ref, *, layer, n_chunks):
    C, SUB = HGRN_CHUNK, HGRN_SUB
    n_sub = C // SUB

    @pl.when(pl.program_id(2) == 0)
    def _():
        state_ref[...] = jnp.zeros(state_ref.shape, jnp.float32)

    th = lbt_ref[...]
    e = jnp.exp(th - jnp.max(th, axis=0, keepdims=True))
    lb = jnp.sum(e[0:layer + 1], axis=0, keepdims=True) / jnp.sum(e, axis=0, keepdims=True)

    r_io = lax.broadcasted_iota(jnp.int32, (C, C), 0)
    c_io = lax.broadcasted_iota(jnp.int32, (C, C), 1)
    tril = (c_io <= r_io).astype(jnp.float32)
    tloc = lax.broadcasted_iota(jnp.int32, (C, HEAD_W), 0) % SUB
    sub_col = lax.broadcasted_iota(jnp.int32, (SUB, C), 1)
    g_out = g_ref[...]

    def group_rows(x, s):
        return jnp.concatenate(
            [jnp.broadcast_to(x[i * SUB + s:i * SUB + s + 1, :], (SUB, HEAD_W)) for i in range(n_sub)], axis=0)

    def chunk(ci, carry):
        rows = pl.ds(pl.multiple_of(ci * C, C), C)
        hq = hq_ref[0, rows, :]
        f = lb + (1.0 - lb) * _sigmoid(hf_ref[0, rows, :])
        glog = jnp.log(f)
        kk = 1.0 - f
        q = hq * _sigmoid(hq)
        v = hi_ref[0, rows, :]
        b = _mm(tril, glog, _HI)
        st = state_ref[...]

        o = _mm_nt(q * jnp.exp(b), st, _HI)

        o_sub = [jnp.zeros((SUB, HEAD_W), jnp.float32)]
        for i in range(1, n_sub):
            beta = b[i * SUB:i * SUB + 1, :]
            qt = q[i * SUB:(i + 1) * SUB, :] * jnp.exp(b[i * SUB:(i + 1) * SUB, :] - beta)
            kt = kk * jnp.exp(jnp.minimum(beta - b, 0.0))
            p = _mm_nt(qt, kt, _HI)
            p = jnp.where(sub_col < i * SUB, p, 0.0)
            o_sub.append(_mm(p, v, _HI))
        o = o + jnp.concatenate(o_sub, axis=0)

        for s in range(SUB):
            b_s, k_s, v_s = group_rows(b, s), group_rows(kk, s), group_rows(v, s)
            w = q * k_s * jnp.exp(jnp.where(tloc >= s, b - b_s, _NEG_INF))
            o = o + jnp.sum(w, axis=1, keepdims=True) * v_s

        b_last = b[C - 1:C, :]
        kdec = kk * jnp.exp(b_last - b)
        state_ref[...] = st * jnp.exp(b_last) + _mm_tn(v, kdec, _HI)

        hgate = hgate_ref[0, rows, :]
        y = _rms(o, g_out) * (hgate * _sigmoid(hgate))
        o_ref[0, rows, :] = y.astype(o_ref.dtype)
        return carry

    lax.fori_loop(0, n_chunks, chunk, 0)


def _hgrn(hg_in, lb_theta, g_hgrn_out, layer, sb):
    B, S, _ = hg_in.shape
    H = HGRN_HEADS
    spec = lambda off: pl.BlockSpec((1, sb, HEAD_W), lambda b, h, i: (b, i, off + h))
    n_slots = lb_theta.shape[0]
    return pl.pallas_call(
        functools.partial(_hgrn_kernel, layer=layer, n_chunks=sb // HGRN_CHUNK),
        grid=(B, H, S // sb),
        in_specs=[spec(0), spec(H), spec(2 * H), spec(3 * H),
                  pl.BlockSpec((n_slots, HEAD_W), lambda b, h, i: (0, h)),
                  pl.BlockSpec((1, HEAD_W), lambda b, h, i: (0, 0))],
        out_specs=pl.BlockSpec((1, sb, HEAD_W), lambda b, h, i: (b, i, h)),
        out_shape=jax.ShapeDtypeStruct((B, S, H * HEAD_W), jnp.bfloat16),
        scratch_shapes=[pltpu.VMEM((HEAD_W, HEAD_W), jnp.float32)],
        compiler_params=pltpu.CompilerParams(
            dimension_semantics=("parallel", "parallel", "arbitrary"), vmem_limit_bytes=VMEM_LIMIT),
        name="hgrn2",
    )(hg_in, hg_in, hg_in, hg_in, lb_theta, g_hgrn_out.reshape(1, HEAD_W))


def _outproj_kernel(ao_ref, ho_ref, x_ref, wo_ref, gpost_ref, gt_ref, gpre_ref, sc_ref, sh_ref, wq_ref,
                    x1_ref, h2_ref, q_ref):
    n_a = ao_ref.shape[2]
    y = _mm(ao_ref[0], wo_ref[0:n_a, :]) + _mm(ho_ref[0], wo_ref[n_a:, :])
    x1 = x_ref[0] + gt_ref[0] * _rms(y, gpost_ref[...])
    x1_ref[0] = x1
    h2 = _rms(x1, gpre_ref[...]) * (1.0 + sc_ref[0]) + sh_ref[0]
    h2_ref[0] = h2
    q_ref[0] = _mm(h2.astype(jnp.bfloat16), wq_ref[...]).astype(q_ref.dtype)


def _out_projection(ao, ho, x, w_out_bf16, g_post, gt1, g_pre, sc2, sh2, w_pq_bf16, tm):
    B, S, D = x.shape
    n_a, n_h, n_q = ao.shape[2], ho.shape[2], w_pq_bf16.shape[1]
    vec = pl.BlockSpec((1, 1, D), lambda b, i: (b, 0, 0))
    par = pl.BlockSpec((1, D), lambda b, i: (0, 0))
    row = lambda n: pl.BlockSpec((1, tm, n), lambda b, i: (b, i, 0))
    return pl.pallas_call(
        _outproj_kernel,
        grid=(B, S // tm),
        in_specs=[row(n_a), row(n_h), row(D),
                  pl.BlockSpec((n_a + n_h, D), lambda b, i: (0, 0)),
                  par, vec, par, vec, vec,
                  pl.BlockSpec((D, n_q), lambda b, i: (0, 0))],
        out_specs=[row(D), row(D), row(n_q)],
        out_shape=[jax.ShapeDtypeStruct((B, S, D), jnp.float32),
                   jax.ShapeDtypeStruct((B, S, D), jnp.float32),
                   jax.ShapeDtypeStruct((B, S, n_q), jnp.bfloat16)],
        compiler_params=pltpu.CompilerParams(
            dimension_semantics=("parallel", "parallel"), vmem_limit_bytes=VMEM_LIMIT),
        name="outproj_norms_peerq",
    )(ao, ho, x, w_out_bf16, g_post.reshape(1, D), gt1, g_pre.reshape(1, D), sc2, sh2, w_pq_bf16)


def _pair_list():
    return [(a, b) for a in range(PEER_TOPK) for b in range(PEER_TOPK) if (a + 1) * (b + 1) <= PEER_TOPK]


def _topk_kernel(q_ref, keys_ref, idx_ref, gate_ref, v_scr, i_scr, cand_scr, cidx_scr, ts_scr, sel_scr):
    K = PEER_TOPK
    tb = q_ref.shape[0]
    pairs = _pair_list()
    n_cand = cand_scr.shape[0]
    kio = lax.broadcasted_iota(jnp.int32, (N_KEYS, tb), 0)
    pio = lax.broadcasted_iota(jnp.int32, (n_cand, tb), 0)

    for h in range(PEER_HEADS):
        for j in range(2):
            c0 = (h * 2 + j) * N_KEYS
            s = _mm_nt(keys_ref[h, j], q_ref[:, c0:c0 + N_KEYS])
            for r in range(K):
                m = jnp.max(s, axis=0, keepdims=True)
                am = jnp.min(jnp.where(s == m, kio, N_KEYS), axis=0, keepdims=True)
                v_scr[j, r:r + 1, :] = m
                i_scr[j, r:r + 1, :] = am
                s = jnp.where(kio == am, _NEG_INF, s)
        v1, v2 = v_scr[0], v_scr[1]
        i1, i2 = i_scr[0], i_scr[1]
        cand_scr[...] = jnp.full(cand_scr.shape, _NEG_INF, jnp.float32)
        cidx_scr[...] = jnp.zeros(cidx_scr.shape, jnp.int32)
        off = 0
        for a in range(K):
            nb = sum(1 for (aa, _) in pairs if aa == a)
            cand_scr[off:off + nb, :] = v1[a:a + 1, :] + v2[0:nb, :]
            cidx_scr[off:off + nb, :] = i1[a:a + 1, :] * N_KEYS + i2[0:nb, :]
            off += nb
        cand = cand_scr[...]
        cidx = cidx_scr[...]
        for r in range(K):
            m = jnp.max(cand, axis=0, keepdims=True)
            pos = jnp.min(jnp.where(cand == m, pio, n_cand), axis=0, keepdims=True)
            hit = pio == pos
            ts_scr[r:r + 1, :] = m
            sel_scr[h * K + r:h * K + r + 1, :] = jnp.sum(jnp.where(hit, cidx, 0), axis=0, keepdims=True)
            cand = jnp.where(hit, _NEG_INF, cand)
        ts = ts_scr[...]
        e = jnp.exp(ts - ts[0:1, :])
        gate_ref[h * K:(h + 1) * K, :] = e / jnp.sum(e, axis=0, keepdims=True)
    idx_ref[...] = sel_scr[...].T


def _peer_topk(q, sub_keys_bf16, tb):
    T = q.shape[0]
    n_sel = PEER_HEADS * PEER_TOPK
    n_cand = -(-len(_pair_list()) // 8) * 8
    return pl.pallas_call(
        _topk_kernel,
        grid=(T // tb,),
        in_specs=[pl.BlockSpec((tb, q.shape[1]), lambda i: (i, 0)),
                  pl.BlockSpec(sub_keys_bf16.shape, lambda i: (0, 0, 0, 0))],
        out_specs=[pl.BlockSpec((tb, n_sel), lambda i: (i, 0)),
                   pl.BlockSpec((n_sel, tb), lambda i: (0, i))],
        out_shape=[jax.ShapeDtypeStruct((T, n_sel), jnp.int32),
                   jax.ShapeDtypeStruct((n_sel, T), jnp.float32)],
        scratch_shapes=[pltpu.VMEM((2, PEER_TOPK, tb), jnp.float32),
                        pltpu.VMEM((2, PEER_TOPK, tb), jnp.int32),
                        pltpu.VMEM((n_cand, tb), jnp.float32),
                        pltpu.VMEM((n_cand, tb), jnp.int32),
                        pltpu.VMEM((PEER_TOPK, tb), jnp.float32),
                        pltpu.VMEM((n_sel, tb), jnp.int32)],
        compiler_params=pltpu.CompilerParams(
            dimension_semantics=("parallel",), vmem_limit_bytes=VMEM_LIMIT),
        name="peer_topk",
    )(q, sub_keys_bf16)


def _peer_kernel(idx_ref, h_ref, gate_ref, x1_ref, gt_ref, g_ref, uv_ref, o_ref, buf, sem, y_scr, *, n_slots):
    tb, D = h_ref.shape
    n_sel = idx_ref.shape[1]

    def row_copy(t, k, slot):
        e = idx_ref[t, k]
        return pltpu.make_async_copy(uv_ref.at[pl.ds(e, 1), :], buf.at[slot, pl.ds(k, 1), :], sem.at[slot])

    def start_token(t):
        slot = t % n_slots
        for k in range(n_sel):
            row_copy(t, k, slot).start()

    def wait_token(t):
        slot = t % n_slots
        pltpu.make_async_copy(uv_ref.at[pl.ds(0, n_sel), :], buf.at[slot], sem.at[slot]).wait()

    for t in range(n_slots - 1):
        start_token(t)

    lane = lax.broadcasted_iota(jnp.int32, gate_ref.shape, 1)

    def token(t, carry):
        @pl.when(t + n_slots - 1 < tb)
        def _():
            start_token(t + n_slots - 1)

        wait_token(t)
        slot = t % n_slots
        h = h_ref[pl.ds(t, 1), :]
        a = jnp.sum(buf[slot, :, 0:D] * h, axis=1, keepdims=True)
        gcol = jnp.sum(jnp.where(lane == t, gate_ref[...], 0.0), axis=1, keepdims=True)
        w = gcol * (0.5 * a * (1.0 + lax.erf(a * (1.0 / math.sqrt(2.0)))))
        y_scr[pl.ds(t, 1), :] = jnp.sum(w * buf[slot, :, D:2 * D], axis=0, keepdims=True)
        return carry

    lax.fori_loop(0, tb, token, 0)
    o_ref[...] = x1_ref[...] + gt_ref[0] * _rms(y_scr[...], g_ref[...])


def _peer_mix(idx, h2, gate_t, x1, gt2, g_post, uv, seq_len, tb, n_slots):
    T, D = h2.shape
    n_sel = idx.shape[1]
    blocks_per_seq = seq_len // tb
    rows = pl.BlockSpec((tb, D), lambda i: (i, 0))
    return pl.pallas_call(
        functools.partial(_peer_kernel, n_slots=n_slots),
        grid=(T // tb,),
        in_specs=[pl.BlockSpec((tb, n_sel), lambda i: (i, 0), memory_space=pltpu.SMEM),
                  rows,
                  pl.BlockSpec((n_sel, tb), lambda i: (0, i)),
                  rows,
                  pl.BlockSpec((1, 1, D), lambda i: (i // blocks_per_seq, 0, 0)),
                  pl.BlockSpec((1, D), lambda i: (0, 0)),
                  pl.BlockSpec(memory_space=pl.ANY)],
        out_specs=rows,
        out_shape=jax.ShapeDtypeStruct((T, D), jnp.float32),
        scratch_shapes=[pltpu.VMEM((n_slots, n_sel, 2 * D), jnp.float32),
                        pltpu.SemaphoreType.DMA((n_slots,)),
                        pltpu.VMEM((tb, D), jnp.float32)],
        compiler_params=pltpu.CompilerParams(
            dimension_semantics=("arbitrary",), vmem_limit_bytes=VMEM_LIMIT),
        name="peer_gather_mix",
    )(idx, h2, gate_t, x1, gt2, g_post.reshape(1, D), uv)


def _pick(n, pref):
    t = min(n, pref)
    assert n % t == 0, (n, pref)
    return t


def kernel(x, c, w_ada, b_ada, g_pre_mix, g_post_mix, g_pre_ffn, g_post_ffn, w_in, lam_qk, g_diff_sub,
           lb_theta, g_hgrn_out, w_out, w_pq, sub_keys, expert_u, expert_v):
    B, S, D = x.shape
    depth = w_in.shape[0]
    n_attn = 3 * DIFF_HEADS * HEAD_W
    tm = _pick(S, 512)
    tq = _pick(S, 256)
    sb = _pick(S, 512)
    tb = _pick(S, 128)
    bf = jnp.bfloat16
    for l in range(depth):
        lam_init = 0.8 - 0.6 * math.exp(-0.3 * l)
        mod = _modulation(c, w_ada[l], b_ada[l]).reshape(N_MOD, B, 1, D)
        sh1, sc1, gt1, sh2, sc2, gt2 = (mod[i] for i in range(N_MOD))
        attn_in, hg_in = _in_projection(x, g_pre_mix[l], sc1, sh1, w_in[l].astype(bf), n_attn, tm)
        ao = _diff_attention(attn_in, lam_qk[l], g_diff_sub[l], lam_init, tq)
        ho = _hgrn(hg_in, lb_theta, g_hgrn_out[l], l, sb)
        x1, h2, q = _out_projection(ao, ho, x, w_out[l].astype(bf), g_post_mix[l], gt1, g_pre_ffn[l],
                                    sc2, sh2, w_pq[l].astype(bf), tm)
        idx, gate_t = _peer_topk(q.reshape(B * S, -1), sub_keys[l].astype(bf), tb)
        uv = jnp.concatenate([expert_u[l], expert_v[l]], axis=1)
        out = _peer_mix(idx, h2.reshape(B * S, D), gate_t, x1.reshape(B * S, D), gt2, g_post_ffn[l], uv,
                        S, tb, n_slots=8)
        x = out.reshape(B, S, D)
    return x
```

```python
import functools
import math

import jax
import jax.numpy as jnp
from jax import lax
from jax.experimental import pallas as pl
from jax.experimental.pallas import tpu as pltpu

EPS = 1e-6
N_MOD = 6
DIFF_HEADS = 4
DIFF_QK = 64
HEAD_W = 128
HGRN_HEADS = 4
HGRN_CHUNK = 64
HGRN_SUB = 16
PEER_HEADS = 8
N_KEYS = 128
PEER_TOPK = 16
VMEM_LIMIT = 56 * 1024 * 1024

_HI = lax.Precision.HIGHEST
_NEG_INF = float("-inf")


def _dot(a, b, dims, precision=None):
    return lax.dot_general(a, b, (dims, ((), ())), precision=precision,
                           preferred_element_type=jnp.float32)


def _mm(a, b, precision=None):
    return _dot(a, b, ((1,), (0,)), precision)


def _mm_nt(a, b, precision=None):
    return _dot(a, b, ((1,), (1,)), precision)


def _mm_tn(a, b, precision=None):
    return _dot(a, b, ((0,), (0,)), precision)


def _rms(x, g):
    return x * lax.rsqrt(jnp.mean(x * x, axis=-1, keepdims=True) + EPS) * g


def _sigmoid(x):
    return 1.0 / (1.0 + jnp.exp(-x))


def _mod_kernel(c_ref, w_ref, b_ref, o_ref):
    c = c_ref[...]
    ca = c * _sigmoid(c)
    o_ref[0] = _mm(ca, w_ref[...], _HI) + b_ref[...]


def _modulation(c, w_ada, b_ada):
    B, D = c.shape
    return pl.pallas_call(
        _mod_kernel,
        grid=(N_MOD,),
        in_specs=[pl.BlockSpec((B, D), lambda j: (0, 0)),
                  pl.BlockSpec((D, D), lambda j: (0, j)),
                  pl.BlockSpec((1, D), lambda j: (0, j))],
        out_specs=pl.BlockSpec((1, B, D), lambda j: (j, 0, 0)),
        out_shape=jax.ShapeDtypeStruct((N_MOD, B, D), jnp.float32),
        compiler_params=pltpu.CompilerParams(vmem_limit_bytes=VMEM_LIMIT),
        name="adaln_mod",
    )(c, w_ada, b_ada.reshape(1, N_MOD * D))


def _inproj_kernel(x_ref, g_ref, sc_ref, sh_ref, w_ref, attn_ref, hg_ref, *, n_attn, col_chunk):
    x = x_ref[0]
    h = _rms(x, g_ref[...]) * (1.0 + sc_ref[0]) + sh_ref[0]
    hb = h.astype(jnp.bfloat16)
    n_cols = w_ref.shape[1]
    for c0 in range(0, n_cols, col_chunk):
        r = _mm(hb, w_ref[:, c0:c0 + col_chunk])
        if c0 < n_attn:
            attn_ref[0, :, c0:c0 + col_chunk] = r.astype(attn_ref.dtype)
        else:
            hg_ref[0, :, c0 - n_attn:c0 - n_attn + col_chunk] = r


def _in_projection(x, g, sc, sh, w_in_bf16, n_attn, tm):
    B, S, D = x.shape
    n_cols = w_in_bf16.shape[1]
    vec = pl.BlockSpec((1, 1, D), lambda b, i: (b, 0, 0))
    return pl.pallas_call(
        functools.partial(_inproj_kernel, n_attn=n_attn, col_chunk=512),
        grid=(B, S // tm),
        in_specs=[pl.BlockSpec((1, tm, D), lambda b, i: (b, i, 0)),
                  pl.BlockSpec((1, D), lambda b, i: (0, 0)),
                  vec, vec,
                  pl.BlockSpec((D, n_cols), lambda b, i: (0, 0))],
        out_specs=[pl.BlockSpec((1, tm, n_attn), lambda b, i: (b, i, 0)),
                   pl.BlockSpec((1, tm, n_cols - n_attn), lambda b, i: (b, i, 0))],
        out_shape=[jax.ShapeDtypeStruct((B, S, n_attn), jnp.bfloat16),
                   jax.ShapeDtypeStruct((B, S, n_cols - n_attn), jnp.float32)],
        compiler_params=pltpu.CompilerParams(
            dimension_semantics=("parallel", "parallel"), vmem_limit_bytes=VMEM_LIMIT),
        name="prenorm_inproj",
    )(x, g.reshape(1, D), sc, sh, w_in_bf16)


def _fold_lanes(x, op):
    out = x[:, 0:HEAD_W]
    for i in range(1, x.shape[1] // HEAD_W):
        out = op(out, x[:, i * HEAD_W:(i + 1) * HEAD_W])
    return out


def _attn_kernel(q_ref, k_ref, v_ref, lam_ref, g_ref, o_ref, s_scr, mx_scr, l_scr, acc_scr, *, tq, lam_init):
    qi = pl.program_id(2)
    q = q_ref[0]
    lane = lax.broadcasted_iota(jnp.int32, q.shape, 1)
    qs = q * jnp.asarray(1.0 / math.sqrt(DIFF_QK), q.dtype)
    zero = jnp.zeros_like(qs)
    q2 = jnp.concatenate([jnp.where(lane < DIFF_QK, qs, zero), jnp.where(lane >= DIFF_QK, qs, zero)], axis=0)
    n_rep = tq // HEAD_W

    mx_scr[...] = jnp.full(mx_scr.shape, _NEG_INF, jnp.float32)

    def scores(j, carry):
        s = _mm_nt(q2, k_ref[0, pl.ds(pl.multiple_of(j * tq, tq), tq), :])
        s_scr[j] = s
        mx_scr[...] = jnp.maximum(mx_scr[...], _fold_lanes(s, jnp.maximum))
        return carry

    lax.fori_loop(0, qi, scores, 0)
    s = _mm_nt(q2, k_ref[0, pl.ds(pl.multiple_of(qi * tq, tq), tq), :])
    row = lax.broadcasted_iota(jnp.int32, s.shape, 0) & (tq - 1)
    col = lax.broadcasted_iota(jnp.int32, s.shape, 1)
    s = jnp.where(col <= row, s, _NEG_INF)
    s_scr[qi] = s
    m = jnp.max(jnp.maximum(mx_scr[...], _fold_lanes(s, jnp.maximum)), axis=1, keepdims=True)
    mx_scr[...] = jnp.broadcast_to(m, mx_scr.shape)
    l_scr[...] = jnp.zeros(l_scr.shape, jnp.float32)
    acc_scr[...] = jnp.zeros(acc_scr.shape, jnp.float32)

    def contract(j, carry):
        mb = mx_scr[...]
        p = jnp.exp(s_scr[j] - jnp.concatenate([mb] * n_rep, axis=1))
        l_scr[...] += _fold_lanes(p, jnp.add)
        vb = v_ref[0, pl.ds(pl.multiple_of(j * tq, tq), tq), :]
        acc_scr[...] += _mm(p.astype(vb.dtype), vb)
        return carry

    lax.fori_loop(0, qi + 1, contract, 0)

    lq = lam_ref[...]
    lam = (jnp.exp(jnp.sum(lq[0:1] * lq[1:2], axis=1, keepdims=True))
           - jnp.exp(jnp.sum(lq[2:3] * lq[3:4], axis=1, keepdims=True)) + lam_init)
    o = acc_scr[...] / jnp.sum(l_scr[...], axis=1, keepdims=True)
    o = o[0:tq] - lam * o[tq:2 * tq]
    o = _rms(o, g_ref[...]) * (1.0 - lam_init)
    o_ref[0] = o.astype(o_ref.dtype)


def _diff_attention(attn_in, lam_qk, g_diff_sub, lam_init, tq):
    B, S, _ = attn_in.shape
    H = DIFF_HEADS
    assert tq % HEAD_W == 0 and tq & (tq - 1) == 0, tq
    kv_spec = lambda off: pl.BlockSpec((1, S, HEAD_W), lambda b, h, i: (b, 0, off + h))
    return pl.pallas_call(
        functools.partial(_attn_kernel, tq=tq, lam_init=lam_init),
        grid=(B, H, S // tq),
        in_specs=[pl.BlockSpec((1, tq, HEAD_W), lambda b, h, i: (b, i, h)),
                  kv_spec(H), kv_spec(2 * H),
                  pl.BlockSpec(lam_qk.shape, lambda b, h, i: (0, 0)),
                  pl.BlockSpec((1, HEAD_W), lambda b, h, i: (0, 0))],
        out_specs=pl.BlockSpec((1, tq, HEAD_W), lambda b, h, i: (b, i, h)),
        out_shape=jax.ShapeDtypeStruct((B, S, H * HEAD_W), jnp.bfloat16),
        scratch_shapes=[pltpu.VMEM((S // tq, 2 * tq, tq), jnp.float32),
                        pltpu.VMEM((2 * tq, HEAD_W), jnp.float32),
                        pltpu.VMEM((2 * tq, HEAD_W), jnp.float32),
                        pltpu.VMEM((2 * tq, HEAD_W), jnp.float32)],
        compiler_params=pltpu.CompilerParams(
            dimension_semantics=("parallel", "parallel", "arbitrary"), vmem_limit_bytes=VMEM_LIMIT),
        name="diff_attention",
    )(attn_in, attn_in, attn_in, lam_qk, g_diff_sub.reshape(1, HEAD_W))


def _hgrn_kernel(hq_ref, hf_ref, hi_ref, hgate_ref, lbt_ref, g_ref, o_ref, state_ref, *, layer, n_chunks):
    C, SUB = HGRN_CHUNK, HGRN_SUB
    n_sub = C // SUB
    n_heads = state_ref.shape[0]

    @pl.when(pl.program_id(1) == 0)
    def _():
        state_ref[...] = jnp.zeros(state_ref.shape, jnp.float32)

    th = lbt_ref[...]
    e = jnp.exp(th - jnp.max(th, axis=0, keepdims=True))
    lb_all = jnp.sum(e[0:layer + 1], axis=0, keepdims=True) / jnp.sum(e, axis=0, keepdims=True)

    r_io = lax.broadcasted_iota(jnp.int32, (C, C), 0)
    c_io = lax.broadcasted_iota(jnp.int32, (C, C), 1)
    tril = (c_io <= r_io).astype(jnp.float32)
    tloc = lax.broadcasted_iota(jnp.int32, (C, HEAD_W), 0) % SUB
    sub_col = lax.broadcasted_iota(jnp.int32, (SUB, C), 1)
    g_out = g_ref[...]

    def group_rows(x, s):
        return jnp.concatenate(
            [jnp.broadcast_to(x[i * SUB + s:i * SUB + s + 1, :], (SUB, HEAD_W)) for i in range(n_sub)], axis=0)

    def head_chunk(rows, hd):
        cols = slice(hd * HEAD_W, (hd + 1) * HEAD_W)
        lb = lb_all[:, cols]
        hq = hq_ref[0, rows, cols]
        f = lb + (1.0 - lb) * _sigmoid(hf_ref[0, rows, cols])
        glog = jnp.log(f)
        kk = 1.0 - f
        q = hq * _sigmoid(hq)
        v = hi_ref[0, rows, cols]
        b = _mm(tril, glog, _HI)
        st = state_ref[hd]

        o = _mm_nt(q * jnp.exp(b), st, _HI)

        o_sub = [jnp.zeros((SUB, HEAD_W), jnp.float32)]
        for i in range(1, n_sub):
            beta = b[i * SUB:i * SUB + 1, :]
            qt = q[i * SUB:(i + 1) * SUB, :] * jnp.exp(b[i * SUB:(i + 1) * SUB, :] - beta)
            kt = kk * jnp.exp(jnp.minimum(beta - b, 0.0))
            p = _mm_nt(qt, kt, _HI)
            p = jnp.where(sub_col < i * SUB, p, 0.0)
            o_sub.append(_mm(p, v, _HI))
        o = o + jnp.concatenate(o_sub, axis=0)

        for s in range(SUB):
            b_s, k_s, v_s = group_rows(b, s), group_rows(kk, s), group_rows(v, s)
            w = q * k_s * jnp.exp(jnp.where(tloc >= s, b - b_s, _NEG_INF))
            o = o + jnp.sum(w, axis=1, keepdims=True) * v_s

        b_last = b[C - 1:C, :]
        kdec = kk * jnp.exp(b_last - b)
        state_ref[hd] = st * jnp.exp(b_last) + _mm_tn(v, kdec, _HI)

        hgate = hgate_ref[0, rows, cols]
        y = _rms(o, g_out) * (hgate * _sigmoid(hgate))
        o_ref[0, rows, cols] = y.astype(o_ref.dtype)

    def chunk(ci, carry):
        rows = pl.ds(pl.multiple_of(ci * C, C), C)
        for hd in range(n_heads):
            head_chunk(rows, hd)
        return carry

    lax.fori_loop(0, n_chunks, chunk, 0)


def _hgrn(hg_in, lb_theta, g_hgrn_out, layer, sb):
    B, S, _ = hg_in.shape
    H = HGRN_HEADS
    W = H * HEAD_W
    spec = lambda off: pl.BlockSpec((1, sb, W), lambda b, i: (b, i, off))
    n_slots = lb_theta.shape[0]
    return pl.pallas_call(
        functools.partial(_hgrn_kernel, layer=layer, n_chunks=sb // HGRN_CHUNK),
        grid=(B, S // sb),
        in_specs=[spec(0), spec(1), spec(2), spec(3),
                  pl.BlockSpec((n_slots, W), lambda b, i: (0, 0)),
                  pl.BlockSpec((1, HEAD_W), lambda b, i: (0, 0))],
        out_specs=pl.BlockSpec((1, sb, W), lambda b, i: (b, i, 0)),
        out_shape=jax.ShapeDtypeStruct((B, S, W), jnp.bfloat16),
        scratch_shapes=[pltpu.VMEM((H, HEAD_W, HEAD_W), jnp.float32)],
        compiler_params=pltpu.CompilerParams(
            dimension_semantics=("parallel", "arbitrary"), vmem_limit_bytes=VMEM_LIMIT),
        name="hgrn2",
    )(hg_in, hg_in, hg_in, hg_in, lb_theta, g_hgrn_out.reshape(1, HEAD_W))


def _outproj_kernel(ao_ref, ho_ref, x_ref, wo_ref, gpost_ref, gt_ref, gpre_ref, sc_ref, sh_ref, wq_ref,
                    x1_ref, h2_ref, q_ref):
    n_a = ao_ref.shape[2]
    y = _mm(ao_ref[0], wo_ref[0:n_a, :]) + _mm(ho_ref[0], wo_ref[n_a:, :])
    x1 = x_ref[0] + gt_ref[0] * _rms(y, gpost_ref[...])
    x1_ref[0] = x1
    h2 = _rms(x1, gpre_ref[...]) * (1.0 + sc_ref[0]) + sh_ref[0]
    h2_ref[0] = h2
    q_ref[0] = _mm(h2.astype(jnp.bfloat16), wq_ref[...]).astype(q_ref.dtype)


def _out_projection(ao, ho, x, w_out_bf16, g_post, gt1, g_pre, sc2, sh2, w_pq_bf16, tm):
    B, S, D = x.shape
    n_a, n_h, n_q = ao.shape[2], ho.shape[2], w_pq_bf16.shape[1]
    vec = pl.BlockSpec((1, 1, D), lambda b, i: (b, 0, 0))
    par = pl.BlockSpec((1, D), lambda b, i: (0, 0))
    row = lambda n: pl.BlockSpec((1, tm, n), lambda b, i: (b, i, 0))
    return pl.pallas_call(
        _outproj_kernel,
        grid=(B, S // tm),
        in_specs=[row(n_a), row(n_h), row(D),
                  pl.BlockSpec((n_a + n_h, D), lambda b, i: (0, 0)),
                  par, vec, par, vec, vec,
                  pl.BlockSpec((D, n_q), lambda b, i: (0, 0))],
        out_specs=[row(D), row(D), row(n_q)],
        out_shape=[jax.ShapeDtypeStruct((B, S, D), jnp.float32),
                   jax.ShapeDtypeStruct((B, S, D), jnp.float32),
                   jax.ShapeDtypeStruct((B, S, n_q), jnp.bfloat16)],
        compiler_params=pltpu.CompilerParams(
            dimension_semantics=("parallel", "parallel"), vmem_limit_bytes=VMEM_LIMIT),
        name="outproj_norms_peerq",
    )(ao, ho, x, w_out_bf16, g_post.reshape(1, D), gt1, g_pre.reshape(1, D), sc2, sh2, w_pq_bf16)


def _pair_list():
    return [(a, b) for a in range(PEER_TOPK) for b in range(PEER_TOPK) if (a + 1) * (b + 1) <= PEER_TOPK]


def _topk_kernel(q_ref, keys_ref, idx_ref, gate_ref, v_scr, i_scr, cand_scr, cidx_scr, ts_scr, sel_scr):
    K = PEER_TOPK
    tb = q_ref.shape[0]
    pairs = _pair_list()
    n_cand = cand_scr.shape[0]
    kio = lax.broadcasted_iota(jnp.int32, (N_KEYS, tb), 0)
    pio = lax.broadcasted_iota(jnp.int32, (n_cand, tb), 0)

    for h in range(PEER_HEADS):
        for j in range(2):
            c0 = (h * 2 + j) * N_KEYS
            s = _mm_nt(keys_ref[h, j], q_ref[:, c0:c0 + N_KEYS])
            for r in range(K):
                m = jnp.max(s, axis=0, keepdims=True)
                am = jnp.min(jnp.where(s == m, kio, N_KEYS), axis=0, keepdims=True)
                v_scr[j, r:r + 1, :] = m
                i_scr[j, r:r + 1, :] = am
                s = jnp.where(kio == am, _NEG_INF, s)
        v1, v2 = v_scr[0], v_scr[1]
        i1, i2 = i_scr[0], i_scr[1]
        cand_scr[...] = jnp.full(cand_scr.shape, _NEG_INF, jnp.float32)
        cidx_scr[...] = jnp.zeros(cidx_scr.shape, jnp.int32)
        off = 0
        for a in range(K):
            nb = sum(1 for (aa, _) in pairs if aa == a)
            cand_scr[off:off + nb, :] = v1[a:a + 1, :] + v2[0:nb, :]
            cidx_scr[off:off + nb, :] = i1[a:a + 1, :] * N_KEYS + i2[0:nb, :]
            off += nb
        cand = cand_scr[...]
        cidx = cidx_scr[...]
        for r in range(K):
            m = jnp.max(cand, axis=0, keepdims=True)
            pos = jnp.min(jnp.where(cand == m, pio, n_cand), axis=0, keepdims=True)
            hit = pio == pos
            ts_scr[r:r + 1, :] = m
            sel_scr[h * K + r:h * K + r + 1, :] = jnp.sum(jnp.where(hit, cidx, 0), axis=0, keepdims=True)
            cand = jnp.where(hit, _NEG_INF, cand)
        ts = ts_scr[...]
        e = jnp.exp(ts - ts[0:1, :])
        gate_ref[h * K:(h + 1) * K, :] = e / jnp.sum(e, axis=0, keepdims=True)
    idx_ref[...] = sel_scr[...].T


def _peer_topk(q, sub_keys_bf16, tb):
    T = q.shape[0]
    n_sel = PEER_HEADS * PEER_TOPK
    n_cand = -(-len(_pair_list()) // 8) * 8
    return pl.pallas_call(
        _topk_kernel,
        grid=(T // tb,),
        in_specs=[pl.BlockSpec((tb, q.shape[1]), lambda i: (i, 0)),
                  pl.BlockSpec(sub_keys_bf16.shape, lambda i: (0, 0, 0, 0))],
        out_specs=[pl.BlockSpec((tb, n_sel), lambda i: (i, 0)),
                   pl.BlockSpec((n_sel, tb), lambda i: (0, i))],
        out_shape=[jax.ShapeDtypeStruct((T, n_sel), jnp.int32),
                   jax.ShapeDtypeStruct((n_sel, T), jnp.float32)],
        scratch_shapes=[pltpu.VMEM((2, PEER_TOPK, tb), jnp.float32),
                        pltpu.VMEM((2, PEER_TOPK, tb), jnp.int32),
                        pltpu.VMEM((n_cand, tb), jnp.float32),
                        pltpu.VMEM((n_cand, tb), jnp.int32),
                        pltpu.VMEM((PEER_TOPK, tb), jnp.float32),
                        pltpu.VMEM((n_sel, tb), jnp.int32)],
        compiler_params=pltpu.CompilerParams(
            dimension_semantics=("parallel",), vmem_limit_bytes=VMEM_LIMIT),
        name="peer_topk",
    )(q, sub_keys_bf16)


def _peer_kernel(idx_ref, h_ref, gate_ref, x1_ref, gt_ref, g_ref, uv_ref, o_ref, buf, sem, y_scr, *, n_slots):
    tb, D = h_ref.shape
    n_sel = idx_ref.shape[1]

    def row_copy(t, k, slot):
        e = idx_ref[t, k]
        return pltpu.make_async_copy(uv_ref.at[pl.ds(e, 1), :], buf.at[slot, pl.ds(k, 1), :], sem.at[slot])

    def start_token(t):
        slot = t % n_slots
        for k in range(n_sel):
            row_copy(t, k, slot).start(priority=k % 2)

    def wait_token(t):
        slot = t % n_slots
        pltpu.make_async_copy(uv_ref.at[pl.ds(0, n_sel), :], buf.at[slot], sem.at[slot]).wait()

    for t in range(n_slots - 1):
        start_token(t)

    lane = lax.broadcasted_iota(jnp.int32, gate_ref.shape, 1)

    def token(t, carry):
        @pl.when(t + n_slots - 1 < tb)
        def _():
            start_token(t + n_slots - 1)

        wait_token(t)
        slot = t % n_slots
        h = h_ref[pl.ds(t, 1), :]
        a = jnp.sum(buf[slot, :, 0:D] * h, axis=1, keepdims=True)
        gcol = jnp.sum(jnp.where(lane == t, gate_ref[...], 0.0), axis=1, keepdims=True)
        w = gcol * (0.5 * a * (1.0 + lax.erf(a * (1.0 / math.sqrt(2.0)))))
        y_scr[pl.ds(t, 1), :] = jnp.sum(w * buf[slot, :, D:2 * D], axis=0, keepdims=True)
        return carry

    lax.fori_loop(0, tb, token, 0)
    o_ref[...] = x1_ref[...] + gt_ref[0] * _rms(y_scr[...], g_ref[...])


def _peer_mix(idx, h2, gate_t, x1, gt2, g_post, uv, seq_len, tb, n_slots):
    T, D = h2.shape
    n_sel = idx.shape[1]
    blocks_per_seq = seq_len // tb
    rows = pl.BlockSpec((tb, D), lambda i: (i, 0))
    return pl.pallas_call(
        functools.partial(_peer_kernel, n_slots=n_slots),
        grid=(T // tb,),
        in_specs=[pl.BlockSpec((tb, n_sel), lambda i: (i, 0), memory_space=pltpu.SMEM),
                  rows,
                  pl.BlockSpec((n_sel, tb), lambda i: (0, i)),
                  rows,
                  pl.BlockSpec((1, 1, D), lambda i: (i // blocks_per_seq, 0, 0)),
                  pl.BlockSpec((1, D), lambda i: (0, 0)),
                  pl.BlockSpec(memory_space=pl.ANY)],
        out_specs=rows,
        out_shape=jax.ShapeDtypeStruct((T, D), jnp.float32),
        scratch_shapes=[pltpu.VMEM((n_slots, n_sel, 2 * D), jnp.float32),
                        pltpu.SemaphoreType.DMA((n_slots,)),
                        pltpu.VMEM((tb, D), jnp.float32)],
        compiler_params=pltpu.CompilerParams(
            dimension_semantics=("arbitrary",), vmem_limit_bytes=VMEM_LIMIT),
        name="peer_gather_mix",
    )(idx, h2, gate_t, x1, gt2, g_post.reshape(1, D), uv)


def _pick(n, pref):
    t = min(n, pref)
    assert n % t == 0, (n, pref)
    return t


def kernel(x, c, w_ada, b_ada, g_pre_mix, g_post_mix, g_pre_ffn, g_post_ffn, w_in, lam_qk, g_diff_sub,
           lb_theta, g_hgrn_out, w_out, w_pq, sub_keys, expert_u, expert_v):
    B, S, D = x.shape
    depth = w_in.shape[0]
    n_attn = 3 * DIFF_HEADS * HEAD_W
    tm = _pick(S, 512)
    tq = _pick(S, 256)
    sb = _pick(S, 512)
    tb = _pick(S, 128)
    bf = jnp.bfloat16
    for l in range(depth):
        lam_init = 0.8 - 0.6 * math.exp(-0.3 * l)
        mod = _modulation(c, w_ada[l], b_ada[l]).reshape(N_MOD, B, 1, D)
        sh1, sc1, gt1, sh2, sc2, gt2 = (mod[i] for i in range(N_MOD))
        attn_in, hg_in = _in_projection(x, g_pre_mix[l], sc1, sh1, w_in[l].astype(bf), n_attn, tm)
        ao = _diff_attention(attn_in, lam_qk[l], g_diff_sub[l], lam_init, tq)
        ho = _hgrn(hg_in, lb_theta, g_hgrn_out[l], l, sb)
        x1, h2, q = _out_projection(ao, ho, x, w_out[l].astype(bf), g_post_mix[l], gt1, g_pre_ffn[l],
                                    sc2, sh2, w_pq[l].astype(bf), tm)
        idx, gate_t = _peer_topk(q.reshape(B * S, -1), sub_keys[l].astype(bf), tb)
        uv = jnp.concatenate([expert_u[l], expert_v[l]], axis=1)
        out = _peer_mix(idx, h2.reshape(B * S, D), gate_t, x1.reshape(B * S, D), gt2, g_post_ffn[l], uv,
                        S, tb, n_slots=4)
        x = out.reshape(B, S, D)
    return x
```

```python
import functools
import math

import jax
import jax.numpy as jnp
from jax import lax
from jax.experimental import pallas as pl
from jax.experimental.pallas import tpu as pltpu

EPS = 1e-6
N_MOD = 6
DIFF_HEADS = 4
DIFF_QK = 64
HEAD_W = 128
HGRN_HEADS = 4
HGRN_CHUNK = 64
HGRN_SUB = 16
PEER_HEADS = 8
N_KEYS = 128
PEER_TOPK = 16
VMEM_LIMIT = 56 * 1024 * 1024

_HI = lax.Precision.HIGHEST
_NEG_INF = float("-inf")


def _dot(a, b, dims, precision=None):
    return lax.dot_general(a, b, (dims, ((), ())), precision=precision,
                           preferred_element_type=jnp.float32)


def _mm(a, b, precision=None):
    return _dot(a, b, ((1,), (0,)), precision)


def _mm_nt(a, b, precision=None):
    return _dot(a, b, ((1,), (1,)), precision)


def _mm_tn(a, b, precision=None):
    return _dot(a, b, ((0,), (0,)), precision)


def _rms(x, g):
    return x * lax.rsqrt(jnp.mean(x * x, axis=-1, keepdims=True) + EPS) * g


def _sigmoid(x):
    return 1.0 / (1.0 + jnp.exp(-x))


def _mod_kernel(c_ref, w_ref, b_ref, o_ref):
    c = c_ref[...]
    ca = c * _sigmoid(c)
    o_ref[0] = _mm(ca, w_ref[...], _HI) + b_ref[...]


def _modulation(c, w_ada, b_ada):
    B, D = c.shape
    return pl.pallas_call(
        _mod_kernel,
        grid=(N_MOD,),
        in_specs=[pl.BlockSpec((B, D), lambda j: (0, 0)),
                  pl.BlockSpec((D, D), lambda j: (0, j)),
                  pl.BlockSpec((1, D), lambda j: (0, j))],
        out_specs=pl.BlockSpec((1, B, D), lambda j: (j, 0, 0)),
        out_shape=jax.ShapeDtypeStruct((N_MOD, B, D), jnp.float32),
        compiler_params=pltpu.CompilerParams(vmem_limit_bytes=VMEM_LIMIT),
        name="adaln_mod",
    )(c, w_ada, b_ada.reshape(1, N_MOD * D))


def _inproj_kernel(x_ref, g_ref, sc_ref, sh_ref, w_ref, attn_ref, hg_ref, *, n_attn, col_chunk):
    x = x_ref[0]
    h = _rms(x, g_ref[...]) * (1.0 + sc_ref[0]) + sh_ref[0]
    hb = h.astype(jnp.bfloat16)
    n_cols = w_ref.shape[1]
    for c0 in range(0, n_cols, col_chunk):
        r = _mm(hb, w_ref[:, c0:c0 + col_chunk])
        if c0 < n_attn:
            attn_ref[0, :, c0:c0 + col_chunk] = r.astype(attn_ref.dtype)
        else:
            hg_ref[0, :, c0 - n_attn:c0 - n_attn + col_chunk] = r


def _in_projection(x, g, sc, sh, w_in_bf16, n_attn, tm):
    B, S, D = x.shape
    n_cols = w_in_bf16.shape[1]
    vec = pl.BlockSpec((1, 1, D), lambda b, i: (b, 0, 0))
    return pl.pallas_call(
        functools.partial(_inproj_kernel, n_attn=n_attn, col_chunk=512),
        grid=(B, S // tm),
        in_specs=[pl.BlockSpec((1, tm, D), lambda b, i: (b, i, 0)),
                  pl.BlockSpec((1, D), lambda b, i: (0, 0)),
                  vec, vec,
                  pl.BlockSpec((D, n_cols), lambda b, i: (0, 0))],
        out_specs=[pl.BlockSpec((1, tm, n_attn), lambda b, i: (b, i, 0)),
                   pl.BlockSpec((1, tm, n_cols - n_attn), lambda b, i: (b, i, 0))],
        out_shape=[jax.ShapeDtypeStruct((B, S, n_attn), jnp.bfloat16),
                   jax.ShapeDtypeStruct((B, S, n_cols - n_attn), jnp.float32)],
        compiler_params=pltpu.CompilerParams(
            dimension_semantics=("parallel", "parallel"), vmem_limit_bytes=VMEM_LIMIT),
        name="prenorm_inproj",
    )(x, g.reshape(1, D), sc, sh, w_in_bf16)


def _fold_lanes(x, op):
    out = x[:, 0:HEAD_W]
    for i in range(1, x.shape[1] // HEAD_W):
        out = op(out, x[:, i * HEAD_W:(i + 1) * HEAD_W])
    return out


def _attn_kernel(q_ref, k_ref, v_ref, lam_ref, g_ref, o_ref, s_scr, mx_scr, l_scr, acc_scr, *, tq, lam_init):
    qi = pl.program_id(2)
    q = q_ref[0]
    lane = lax.broadcasted_iota(jnp.int32, q.shape, 1)
    qs = q * jnp.asarray(1.0 / math.sqrt(DIFF_QK), q.dtype)
    zero = jnp.zeros_like(qs)
    q2 = jnp.concatenate([jnp.where(lane < DIFF_QK, qs, zero), jnp.where(lane >= DIFF_QK, qs, zero)], axis=0)
    n_rep = tq // HEAD_W

    mx_scr[...] = jnp.full(mx_scr.shape, _NEG_INF, jnp.float32)

    def scores(j, carry):
        s = _mm_nt(q2, k_ref[0, pl.ds(pl.multiple_of(j * tq, tq), tq), :])
        s_scr[j] = s
        mx_scr[...] = jnp.maximum(mx_scr[...], _fold_lanes(s, jnp.maximum))
        return carry

    lax.fori_loop(0, qi, scores, 0)
    s = _mm_nt(q2, k_ref[0, pl.ds(pl.multiple_of(qi * tq, tq), tq), :])
    row = lax.broadcasted_iota(jnp.int32, s.shape, 0) & (tq - 1)
    col = lax.broadcasted_iota(jnp.int32, s.shape, 1)
    s = jnp.where(col <= row, s, _NEG_INF)
    s_scr[qi] = s
    m = jnp.max(jnp.maximum(mx_scr[...], _fold_lanes(s, jnp.maximum)), axis=1, keepdims=True)
    mx_scr[...] = jnp.broadcast_to(m, mx_scr.shape)
    l_scr[...] = jnp.zeros(l_scr.shape, jnp.float32)
    acc_scr[...] = jnp.zeros(acc_scr.shape, jnp.float32)

    def contract(j, carry):
        mb = mx_scr[...]
        p = jnp.exp(s_scr[j] - jnp.concatenate([mb] * n_rep, axis=1))
        l_scr[...] += _fold_lanes(p, jnp.add)
        vb = v_ref[0, pl.ds(pl.multiple_of(j * tq, tq), tq), :]
        acc_scr[...] += _mm(p.astype(vb.dtype), vb)
        return carry

    lax.fori_loop(0, qi + 1, contract, 0)

    lq = lam_ref[...]
    lam = (jnp.exp(jnp.sum(lq[0:1] * lq[1:2], axis=1, keepdims=True))
           - jnp.exp(jnp.sum(lq[2:3] * lq[3:4], axis=1, keepdims=True)) + lam_init)
    o = acc_scr[...] / jnp.sum(l_scr[...], axis=1, keepdims=True)
    o = o[0:tq] - lam * o[tq:2 * tq]
    o = _rms(o, g_ref[...]) * (1.0 - lam_init)
    o_ref[0] = o.astype(o_ref.dtype)


def _diff_attention(attn_in, lam_qk, g_diff_sub, lam_init, tq):
    B, S, _ = attn_in.shape
    H = DIFF_HEADS
    assert tq % HEAD_W == 0 and tq & (tq - 1) == 0, tq
    kv_spec = lambda off: pl.BlockSpec((1, S, HEAD_W), lambda b, h, i: (b, 0, off + h))
    return pl.pallas_call(
        functools.partial(_attn_kernel, tq=tq, lam_init=lam_init),
        grid=(B, H, S // tq),
        in_specs=[pl.BlockSpec((1, tq, HEAD_W), lambda b, h, i: (b, i, h)),
                  kv_spec(H), kv_spec(2 * H),
                  pl.BlockSpec(lam_qk.shape, lambda b, h, i: (0, 0)),
                  pl.BlockSpec((1, HEAD_W), lambda b, h, i: (0, 0))],
        out_specs=pl.BlockSpec((1, tq, HEAD_W), lambda b, h, i: (b, i, h)),
        out_shape=jax.ShapeDtypeStruct((B, S, H * HEAD_W), jnp.bfloat16),
        scratch_shapes=[pltpu.VMEM((S // tq, 2 * tq, tq), jnp.float32),
                        pltpu.VMEM((2 * tq, HEAD_W), jnp.float32),
                        pltpu.VMEM((2 * tq, HEAD_W), jnp.float32),
                        pltpu.VMEM((2 * tq, HEAD_W), jnp.float32)],
        compiler_params=pltpu.CompilerParams(
            dimension_semantics=("parallel", "parallel", "arbitrary"), vmem_limit_bytes=VMEM_LIMIT),
        name="diff_attention",
    )(attn_in, attn_in, attn_in, lam_qk, g_diff_sub.reshape(1, HEAD_W))


def _hgrn_kernel(hq_ref, hf_ref, hi_ref, hgate_ref, lbt_ref, g_ref, o_ref, state_ref, *, layer, n_chunks):
    C, SUB = HGRN_CHUNK, HGRN_SUB
    n_sub = C // SUB
    n_heads = state_ref.shape[0]

    @pl.when(pl.program_id(1) == 0)
    def _():
        state_ref[...] = jnp.zeros(state_ref.shape, jnp.float32)

    th = lbt_ref[...]
    e = jnp.exp(th - jnp.max(th, axis=0, keepdims=True))
    lb_all = jnp.sum(e[0:layer + 1], axis=0, keepdims=True) / jnp.sum(e, axis=0, keepdims=True)

    r_io = lax.broadcasted_iota(jnp.int32, (C, C), 0)
    c_io = lax.broadcasted_iota(jnp.int32, (C, C), 1)
    tril = (c_io <= r_io).astype(jnp.float32)
    tloc = lax.broadcasted_iota(jnp.int32, (C, HEAD_W), 0) % SUB
    sub_col = lax.broadcasted_iota(jnp.int32, (SUB, C), 1)
    g_out = g_ref[...]

    def group_rows(x, s):
        return jnp.concatenate(
            [jnp.broadcast_to(x[i * SUB + s:i * SUB + s + 1, :], (SUB, HEAD_W)) for i in range(n_sub)], axis=0)

    def head_chunk(rows, hd):
        cols = slice(hd * HEAD_W, (hd + 1) * HEAD_W)
        lb = lb_all[:, cols]
        hq = hq_ref[0, rows, cols]
        f = lb + (1.0 - lb) * _sigmoid(hf_ref[0, rows, cols])
        glog = jnp.log(f)
        kk = 1.0 - f
        q = hq * _sigmoid(hq)
        v = hi_ref[0, rows, cols]
        b = _mm(tril, glog, _HI)
        st = state_ref[hd]

        o = _mm_nt(q * jnp.exp(b), st, _HI)

        o_sub = [jnp.zeros((SUB, HEAD_W), jnp.float32)]
        for i in range(1, n_sub):
            beta = b[i * SUB:i * SUB + 1, :]
            qt = q[i * SUB:(i + 1) * SUB, :] * jnp.exp(b[i * SUB:(i + 1) * SUB, :] - beta)
            kt = kk * jnp.exp(jnp.minimum(beta - b, 0.0))
            p = _mm_nt(qt, kt, _HI)
            p = jnp.where(sub_col < i * SUB, p, 0.0)
            o_sub.append(_mm(p, v, _HI))
        o = o + jnp.concatenate(o_sub, axis=0)

        for s in range(SUB):
            b_s, k_s, v_s = group_rows(b, s), group_rows(kk, s), group_rows(v, s)
            w = q * k_s * jnp.exp(jnp.where(tloc >= s, b - b_s, _NEG_INF))
            o = o + jnp.sum(w, axis=1, keepdims=True) * v_s

        b_last = b[C - 1:C, :]
        kdec = kk * jnp.exp(b_last - b)
        state_ref[hd] = st * jnp.exp(b_last) + _mm_tn(v, kdec, _HI)

        hgate = hgate_ref[0, rows, cols]
        y = _rms(o, g_out) * (hgate * _sigmoid(hgate))
        o_ref[0, rows, cols] = y.astype(o_ref.dtype)

    def chunk(ci, carry):
        rows = pl.ds(pl.multiple_of(ci * C, C), C)
        for hd in range(n_heads):
            head_chunk(rows, hd)
        return carry

    lax.fori_loop(0, n_chunks, chunk, 0)


def _hgrn(hg_in, lb_theta, g_hgrn_out, layer, sb):
    B, S, _ = hg_in.shape
    H = HGRN_HEADS
    W = H * HEAD_W
    spec = lambda off: pl.BlockSpec((1, sb, W), lambda b, i: (b, i, off))
    n_slots = lb_theta.shape[0]
    return pl.pallas_call(
        functools.partial(_hgrn_kernel, layer=layer, n_chunks=sb // HGRN_CHUNK),
        grid=(B, S // sb),
        in_specs=[spec(0), spec(1), spec(2), spec(3),
                  pl.BlockSpec((n_slots, W), lambda b, i: (0, 0)),
                  pl.BlockSpec((1, HEAD_W), lambda b, i: (0, 0))],
        out_specs=pl.BlockSpec((1, sb, W), lambda b, i: (b, i, 0)),
        out_shape=jax.ShapeDtypeStruct((B, S, W), jnp.bfloat16),
        scratch_shapes=[pltpu.VMEM((H, HEAD_W, HEAD_W), jnp.float32)],
        compiler_params=pltpu.CompilerParams(
            dimension_semantics=("parallel", "arbitrary"), vmem_limit_bytes=VMEM_LIMIT),
        name="hgrn2",
    )(hg_in, hg_in, hg_in, hg_in, lb_theta, g_hgrn_out.reshape(1, HEAD_W))


def _outproj_kernel(ao_ref, ho_ref, x_ref, wo_ref, gpost_ref, gt_ref, gpre_ref, sc_ref, sh_ref, wq_ref,
                    x1_ref, h2_ref, q_ref):
    n_a = ao_ref.shape[2]
    y = _mm(ao_ref[0], wo_ref[0:n_a, :]) + _mm(ho_ref[0], wo_ref[n_a:, :])
    x1 = x_ref[0] + gt_ref[0] * _rms(y, gpost_ref[...])
    x1_ref[0] = x1
    h2 = _rms(x1, gpre_ref[...]) * (1.0 + sc_ref[0]) + sh_ref[0]
    h2_ref[0] = h2
    q_ref[0] = _mm(h2.astype(jnp.bfloat16), wq_ref[...]).astype(q_ref.dtype)


def _out_projection(ao, ho, x, w_out_bf16, g_post, gt1, g_pre, sc2, sh2, w_pq_bf16, tm):
    B, S, D = x.shape
    n_a, n_h, n_q = ao.shape[2], ho.shape[2], w_pq_bf16.shape[1]
    vec = pl.BlockSpec((1, 1, D), lambda b, i: (b, 0, 0))
    par = pl.BlockSpec((1, D), lambda b, i: (0, 0))
    row = lambda n: pl.BlockSpec((1, tm, n), lambda b, i: (b, i, 0))
    return pl.pallas_call(
        _outproj_kernel,
        grid=(B, S // tm),
        in_specs=[row(n_a), row(n_h), row(D),
                  pl.BlockSpec((n_a + n_h, D), lambda b, i: (0, 0)),
                  par, vec, par, vec, vec,
                  pl.BlockSpec((D, n_q), lambda b, i: (0, 0))],
        out_specs=[row(D), row(D), row(n_q)],
        out_shape=[jax.ShapeDtypeStruct((B, S, D), jnp.float32),
                   jax.ShapeDtypeStruct((B, S, D), jnp.float32),
                   jax.ShapeDtypeStruct((B, S, n_q), jnp.bfloat16)],
        compiler_params=pltpu.CompilerParams(
            dimension_semantics=("parallel", "parallel"), vmem_limit_bytes=VMEM_LIMIT),
        name="outproj_norms_peerq",
    )(ao, ho, x, w_out_bf16, g_post.reshape(1, D), gt1, g_pre.reshape(1, D), sc2, sh2, w_pq_bf16)


def _pair_list():
    return [(a, b) for a in range(PEER_TOPK) for b in range(PEER_TOPK) if (a + 1) * (b + 1) <= PEER_TOPK]


def _topk_kernel(q_ref, keys_ref, idx_ref, gate_ref, v_scr, i_scr, cand_scr, cidx_scr, ts_scr, sel_scr):
    K = PEER_TOPK
    tb = q_ref.shape[0]
    pairs = _pair_list()
    n_cand = cand_scr.shape[0]
    kio = lax.broadcasted_iota(jnp.int32, (N_KEYS, tb), 0)
    pio = lax.broadcasted_iota(jnp.int32, (n_cand, tb), 0)

    for h in range(PEER_HEADS):
        for j in range(2):
            c0 = (h * 2 + j) * N_KEYS
            s = _mm_nt(keys_ref[h, j], q_ref[:, c0:c0 + N_KEYS])
            for r in range(K):
                m = jnp.max(s, axis=0, keepdims=True)
                am = jnp.min(jnp.where(s == m, kio, N_KEYS), axis=0, keepdims=True)
                v_scr[j, r:r + 1, :] = m
                i_scr[j, r:r + 1, :] = am
                s = jnp.where(kio == am, _NEG_INF, s)
        v1, v2 = v_scr[0], v_scr[1]
        i1, i2 = i_scr[0], i_scr[1]
        cand_scr[...] = jnp.full(cand_scr.shape, _NEG_INF, jnp.float32)
        cidx_scr[...] = jnp.zeros(cidx_scr.shape, jnp.int32)
        off = 0
        for a in range(K):
            nb = sum(1 for (aa, _) in pairs if aa == a)
            cand_scr[off:off + nb, :] = v1[a:a + 1, :] + v2[0:nb, :]
            cidx_scr[off:off + nb, :] = i1[a:a + 1, :] * N_KEYS + i2[0:nb, :]
            off += nb
        cand = cand_scr[...]
        cidx = cidx_scr[...]
        for r in range(K):
            m = jnp.max(cand, axis=0, keepdims=True)
            pos = jnp.min(jnp.where(cand == m, pio, n_cand), axis=0, keepdims=True)
            hit = pio == pos
            ts_scr[r:r + 1, :] = m
            sel_scr[h * K + r:h * K + r + 1, :] = jnp.sum(jnp.where(hit, cidx, 0), axis=0, keepdims=True)
            cand = jnp.where(hit, _NEG_INF, cand)
        ts = ts_scr[...]
        e = jnp.exp(ts - ts[0:1, :])
        gate_ref[h * K:(h + 1) * K, :] = e / jnp.sum(e, axis=0, keepdims=True)
    idx_ref[...] = sel_scr[...].T


def _peer_topk(q, sub_keys_bf16, tb):
    T = q.shape[0]
    n_sel = PEER_HEADS * PEER_TOPK
    n_cand = -(-len(_pair_list()) // 8) * 8
    return pl.pallas_call(
        _topk_kernel,
        grid=(T // tb,),
        in_specs=[pl.BlockSpec((tb, q.shape[1]), lambda i: (i, 0)),
                  pl.BlockSpec(sub_keys_bf16.shape, lambda i: (0, 0, 0, 0))],
        out_specs=[pl.BlockSpec((tb, n_sel), lambda i: (i, 0)),
                   pl.BlockSpec((n_sel, tb), lambda i: (0, i))],
        out_shape=[jax.ShapeDtypeStruct((T, n_sel), jnp.int32),
                   jax.ShapeDtypeStruct((n_sel, T), jnp.float32)],
        scratch_shapes=[pltpu.VMEM((2, PEER_TOPK, tb), jnp.float32),
                        pltpu.VMEM((2, PEER_TOPK, tb), jnp.int32),
                        pltpu.VMEM((n_cand, tb), jnp.float32),
                        pltpu.VMEM((n_cand, tb), jnp.int32),
                        pltpu.VMEM((PEER_TOPK, tb), jnp.float32),
                        pltpu.VMEM((n_sel, tb), jnp.int32)],
        compiler_params=pltpu.CompilerParams(
            dimension_semantics=("parallel",), vmem_limit_bytes=VMEM_LIMIT),
        name="peer_topk",
    )(q, sub_keys_bf16)


def _peer_kernel(idx_ref, h_ref, gate_ref, x1_ref, gt_ref, g_ref, uv_ref, o_ref, buf, sem, y_scr, *, n_slots):
    tb, D = h_ref.shape
    n_sel = idx_ref.shape[1]
    n_c = D // HEAD_W

    def row_copy(t, k, slot):
        e = idx_ref[t, k]
        return pltpu.make_async_copy(uv_ref.at[e], buf.at[slot, :, k], sem.at[slot])

    def start_token(t):
        slot = t % n_slots
        for k in range(n_sel):
            row_copy(t, k, slot).start()

    def wait_token(t):
        slot = t % n_slots
        pltpu.make_async_copy(buf.at[slot], buf.at[slot], sem.at[slot]).wait()

    for t in range(n_slots - 1):
        start_token(t)

    lane = lax.broadcasted_iota(jnp.int32, gate_ref.shape, 1)

    def token(t, carry):
        @pl.when(t + n_slots - 1 < tb)
        def _():
            start_token(t + n_slots - 1)

        wait_token(t)
        slot = t % n_slots
        h = h_ref[pl.ds(t, 1), :]
        p = buf[slot, 0] * h[:, 0:HEAD_W]
        for c in range(1, n_c):
            p = p + buf[slot, c] * h[:, c * HEAD_W:(c + 1) * HEAD_W]
        a = jnp.sum(p, axis=1, keepdims=True)
        gcol = jnp.sum(jnp.where(lane == t, gate_ref[...], 0.0), axis=1, keepdims=True)
        w = gcol * (0.5 * a * (1.0 + lax.erf(a * (1.0 / math.sqrt(2.0)))))
        y_scr[pl.ds(t, 1), :] = jnp.concatenate(
            [jnp.sum(w * buf[slot, n_c + c], axis=0, keepdims=True) for c in range(n_c)], axis=1)
        return carry

    lax.fori_loop(0, tb, token, 0)
    o_ref[...] = x1_ref[...] + gt_ref[0] * _rms(y_scr[...], g_ref[...])


def _peer_mix(idx, h2, gate_t, x1, gt2, g_post, uv, seq_len, tb, n_slots):
    T, D = h2.shape
    n_sel = idx.shape[1]
    blocks_per_seq = seq_len // tb
    rows = pl.BlockSpec((tb, D), lambda i: (i, 0))
    return pl.pallas_call(
        functools.partial(_peer_kernel, n_slots=n_slots),
        grid=(T // tb,),
        in_specs=[pl.BlockSpec((tb, n_sel), lambda i: (i, 0), memory_space=pltpu.SMEM),
                  rows,
                  pl.BlockSpec((n_sel, tb), lambda i: (0, i)),
                  rows,
                  pl.BlockSpec((1, 1, D), lambda i: (i // blocks_per_seq, 0, 0)),
                  pl.BlockSpec((1, D), lambda i: (0, 0)),
                  pl.BlockSpec(memory_space=pl.ANY)],
        out_specs=rows,
        out_shape=jax.ShapeDtypeStruct((T, D), jnp.float32),
        scratch_shapes=[pltpu.VMEM((n_slots, 2 * D // HEAD_W, n_sel, HEAD_W), jnp.float32),
                        pltpu.SemaphoreType.DMA((n_slots,)),
                        pltpu.VMEM((tb, D), jnp.float32)],
        compiler_params=pltpu.CompilerParams(
            dimension_semantics=("arbitrary",), vmem_limit_bytes=VMEM_LIMIT),
        name="peer_gather_mix",
    )(idx, h2, gate_t, x1, gt2, g_post.reshape(1, D), uv)


def _pick(n, pref):
    t = min(n, pref)
    assert n % t == 0, (n, pref)
    return t


def kernel(x, c, w_ada, b_ada, g_pre_mix, g_post_mix, g_pre_ffn, g_post_ffn, w_in, lam_qk, g_diff_sub,
           lb_theta, g_hgrn_out, w_out, w_pq, sub_keys, expert_u, expert_v):
    B, S, D = x.shape
    depth = w_in.shape[0]
    n_attn = 3 * DIFF_HEADS * HEAD_W
    tm = _pick(S, 512)
    tq = _pick(S, 256)
    sb = _pick(S, 512)
    tb = _pick(S, 128)
    bf = jnp.bfloat16
    for l in range(depth):
        lam_init = 0.8 - 0.6 * math.exp(-0.3 * l)
        mod = _modulation(c, w_ada[l], b_ada[l]).reshape(N_MOD, B, 1, D)
        sh1, sc1, gt1, sh2, sc2, gt2 = (mod[i] for i in range(N_MOD))
        attn_in, hg_in = _in_projection(x, g_pre_mix[l], sc1, sh1, w_in[l].astype(bf), n_attn, tm)
        ao = _diff_attention(attn_in, lam_qk[l], g_diff_sub[l], lam_init, tq)
        ho = _hgrn(hg_in, lb_theta, g_hgrn_out[l], l, sb)
        x1, h2, q = _out_projection(ao, ho, x, w_out[l].astype(bf), g_post_mix[l], gt1, g_pre_ffn[l],
                                    sc2, sh2, w_pq[l].astype(bf), tm)
        idx, gate_t = _peer_topk(q.reshape(B * S, -1), sub_keys[l].astype(bf), tb)
        uv = jnp.concatenate([expert_u[l], expert_v[l]], axis=1).reshape(-1, 2 * D // HEAD_W, HEAD_W)
        out = _peer_mix(idx, h2.reshape(B * S, D), gate_t, x1.reshape(B * S, D), gt2, g_post_ffn[l], uv,
                        S, tb, n_slots=4)
        x = out.reshape(B, S, D)
    return x
```

```python
import functools
import math

import jax
import jax.numpy as jnp
from jax import lax
from jax.experimental import pallas as pl
from jax.experimental.pallas import tpu as pltpu

EPS = 1e-6
N_MOD = 6
DIFF_HEADS = 4
DIFF_QK = 64
HEAD_W = 128
HGRN_HEADS = 4
HGRN_CHUNK = 64
HGRN_SUB = 16
PEER_HEADS = 8
N_KEYS = 128
PEER_TOPK = 16
VMEM_LIMIT = 56 * 1024 * 1024

_HI = lax.Precision.HIGHEST
_NEG_INF = float("-inf")


def _dot(a, b, dims, precision=None):
    return lax.dot_general(a, b, (dims, ((), ())), precision=precision,
                           preferred_element_type=jnp.float32)


def _mm(a, b, precision=None):
    return _dot(a, b, ((1,), (0,)), precision)


def _mm_nt(a, b, precision=None):
    return _dot(a, b, ((1,), (1,)), precision)


def _mm_tn(a, b, precision=None):
    return _dot(a, b, ((0,), (0,)), precision)


def _rms(x, g):
    return x * lax.rsqrt(jnp.mean(x * x, axis=-1, keepdims=True) + EPS) * g


def _sigmoid(x):
    return 1.0 / (1.0 + jnp.exp(-x))


def _mod_kernel(c_ref, w_ref, b_ref, o_ref):
    c = c_ref[...]
    ca = c * _sigmoid(c)
    o_ref[0] = _mm(ca, w_ref[...], _HI) + b_ref[...]


def _modulation(c, w_ada, b_ada):
    B, D = c.shape
    return pl.pallas_call(
        _mod_kernel,
        grid=(N_MOD,),
        in_specs=[pl.BlockSpec((B, D), lambda j: (0, 0)),
                  pl.BlockSpec((D, D), lambda j: (0, j)),
                  pl.BlockSpec((1, D), lambda j: (0, j))],
        out_specs=pl.BlockSpec((1, B, D), lambda j: (j, 0, 0)),
        out_shape=jax.ShapeDtypeStruct((N_MOD, B, D), jnp.float32),
        compiler_params=pltpu.CompilerParams(vmem_limit_bytes=VMEM_LIMIT),
        name="adaln_mod",
    )(c, w_ada, b_ada.reshape(1, N_MOD * D))


def _inproj_kernel(x_ref, g_ref, sc_ref, sh_ref, w_ref, attn_ref, hg_ref, *, n_attn, col_chunk):
    x = x_ref[0]
    h = _rms(x, g_ref[...]) * (1.0 + sc_ref[0]) + sh_ref[0]
    hb = h.astype(jnp.bfloat16)
    n_cols = w_ref.shape[1]
    for c0 in range(0, n_cols, col_chunk):
        r = _mm(hb, w_ref[:, c0:c0 + col_chunk])
        if c0 < n_attn:
            attn_ref[0, :, c0:c0 + col_chunk] = r.astype(attn_ref.dtype)
        else:
            hg_ref[0, :, c0 - n_attn:c0 - n_attn + col_chunk] = r


def _in_projection(x, g, sc, sh, w_in_bf16, n_attn, tm):
    B, S, D = x.shape
    n_cols = w_in_bf16.shape[1]
    vec = pl.BlockSpec((1, 1, D), lambda b, i: (b, 0, 0))
    return pl.pallas_call(
        functools.partial(_inproj_kernel, n_attn=n_attn, col_chunk=512),
        grid=(B, S // tm),
        in_specs=[pl.BlockSpec((1, tm, D), lambda b, i: (b, i, 0)),
                  pl.BlockSpec((1, D), lambda b, i: (0, 0)),
                  vec, vec,
                  pl.BlockSpec((D, n_cols), lambda b, i: (0, 0))],
        out_specs=[pl.BlockSpec((1, tm, n_attn), lambda b, i: (b, i, 0)),
                   pl.BlockSpec((1, tm, n_cols - n_attn), lambda b, i: (b, i, 0))],
        out_shape=[jax.ShapeDtypeStruct((B, S, n_attn), jnp.bfloat16),
                   jax.ShapeDtypeStruct((B, S, n_cols - n_attn), jnp.float32)],
        compiler_params=pltpu.CompilerParams(
            dimension_semantics=("parallel", "parallel"), vmem_limit_bytes=VMEM_LIMIT),
        name="prenorm_inproj",
    )(x, g.reshape(1, D), sc, sh, w_in_bf16)


def _fold_lanes(x, op):
    out = x[:, 0:HEAD_W]
    for i in range(1, x.shape[1] // HEAD_W):
        out = op(out, x[:, i * HEAD_W:(i + 1) * HEAD_W])
    return out


def _attn_kernel(q_ref, k_ref, v_ref, lam_ref, g_ref, o_ref, s_scr, mx_scr, l_scr, acc_scr, *, tq, lam_init):
    qi = pl.program_id(2)
    q = q_ref[0]
    lane = lax.broadcasted_iota(jnp.int32, q.shape, 1)
    qs = q * jnp.asarray(1.0 / math.sqrt(DIFF_QK), q.dtype)
    zero = jnp.zeros_like(qs)
    q2 = jnp.concatenate([jnp.where(lane < DIFF_QK, qs, zero), jnp.where(lane >= DIFF_QK, qs, zero)], axis=0)
    n_rep = tq // HEAD_W

    mx_scr[...] = jnp.full(mx_scr.shape, _NEG_INF, jnp.float32)

    def scores(j, carry):
        s = _mm_nt(q2, k_ref[0, pl.ds(pl.multiple_of(j * tq, tq), tq), :])
        s_scr[j] = s
        mx_scr[...] = jnp.maximum(mx_scr[...], _fold_lanes(s, jnp.maximum))
        return carry

    lax.fori_loop(0, qi, scores, 0)
    s = _mm_nt(q2, k_ref[0, pl.ds(pl.multiple_of(qi * tq, tq), tq), :])
    row = lax.broadcasted_iota(jnp.int32, s.shape, 0) & (tq - 1)
    col = lax.broadcasted_iota(jnp.int32, s.shape, 1)
    s = jnp.where(col <= row, s, _NEG_INF)
    s_scr[qi] = s
    m = jnp.max(jnp.maximum(mx_scr[...], _fold_lanes(s, jnp.maximum)), axis=1, keepdims=True)
    mx_scr[...] = jnp.broadcast_to(m, mx_scr.shape)
    l_scr[...] = jnp.zeros(l_scr.shape, jnp.float32)
    acc_scr[...] = jnp.zeros(acc_scr.shape, jnp.float32)

    def contract(j, carry):
        mb = mx_scr[...]
        p = jnp.exp(s_scr[j] - jnp.concatenate([mb] * n_rep, axis=1))
        l_scr[...] += _fold_lanes(p, jnp.add)
        vb = v_ref[0, pl.ds(pl.multiple_of(j * tq, tq), tq), :]
        acc_scr[...] += _mm(p.astype(vb.dtype), vb)
        return carry

    lax.fori_loop(0, qi + 1, contract, 0)

    lq = lam_ref[...]
    lam = (jnp.exp(jnp.sum(lq[0:1] * lq[1:2], axis=1, keepdims=True))
           - jnp.exp(jnp.sum(lq[2:3] * lq[3:4], axis=1, keepdims=True)) + lam_init)
    o = acc_scr[...] / jnp.sum(l_scr[...], axis=1, keepdims=True)
    o = o[0:tq] - lam * o[tq:2 * tq]
    o = _rms(o, g_ref[...]) * (1.0 - lam_init)
    o_ref[0] = o.astype(o_ref.dtype)


def _diff_attention(attn_in, lam_qk, g_diff_sub, lam_init, tq):
    B, S, _ = attn_in.shape
    H = DIFF_HEADS
    assert tq % HEAD_W == 0 and tq & (tq - 1) == 0, tq
    kv_spec = lambda off: pl.BlockSpec((1, S, HEAD_W), lambda b, h, i: (b, 0, off + h))
    return pl.pallas_call(
        functools.partial(_attn_kernel, tq=tq, lam_init=lam_init),
        grid=(B, H, S // tq),
        in_specs=[pl.BlockSpec((1, tq, HEAD_W), lambda b, h, i: (b, i, h)),
                  kv_spec(H), kv_spec(2 * H),
                  pl.BlockSpec(lam_qk.shape, lambda b, h, i: (0, 0)),
                  pl.BlockSpec((1, HEAD_W), lambda b, h, i: (0, 0))],
        out_specs=pl.BlockSpec((1, tq, HEAD_W), lambda b, h, i: (b, i, h)),
        out_shape=jax.ShapeDtypeStruct((B, S, H * HEAD_W), jnp.bfloat16),
        scratch_shapes=[pltpu.VMEM((S // tq, 2 * tq, tq), jnp.float32),
                        pltpu.VMEM((2 * tq, HEAD_W), jnp.float32),
                        pltpu.VMEM((2 * tq, HEAD_W), jnp.float32),
                        pltpu.VMEM((2 * tq, HEAD_W), jnp.float32)],
        compiler_params=pltpu.CompilerParams(
            dimension_semantics=("parallel", "parallel", "arbitrary"), vmem_limit_bytes=VMEM_LIMIT),
        name="diff_attention",
    )(attn_in, attn_in, attn_in, lam_qk, g_diff_sub.reshape(1, HEAD_W))


def _hgrn_kernel(hq_ref, hf_ref, hi_ref, hgate_ref, lbt_ref, g_ref, o_ref, state_ref, *, layer, n_chunks):
    C, SUB = HGRN_CHUNK, HGRN_SUB
    n_sub = C // SUB
    n_heads = state_ref.shape[0]

    @pl.when(pl.program_id(1) == 0)
    def _():
        state_ref[...] = jnp.zeros(state_ref.shape, jnp.float32)

    th = lbt_ref[...]
    e = jnp.exp(th - jnp.max(th, axis=0, keepdims=True))
    lb_all = jnp.sum(e[0:layer + 1], axis=0, keepdims=True) / jnp.sum(e, axis=0, keepdims=True)

    r_io = lax.broadcasted_iota(jnp.int32, (C, C), 0)
    c_io = lax.broadcasted_iota(jnp.int32, (C, C), 1)
    tril = (c_io <= r_io).astype(jnp.float32)
    tloc = lax.broadcasted_iota(jnp.int32, (C, HEAD_W), 0) % SUB
    sub_col = lax.broadcasted_iota(jnp.int32, (SUB, C), 1)
    g_out = g_ref[...]

    def group_rows(x, s):
        return jnp.concatenate(
            [jnp.broadcast_to(x[i * SUB + s:i * SUB + s + 1, :], (SUB, HEAD_W)) for i in range(n_sub)], axis=0)

    def head_chunk(rows, hd):
        cols = slice(hd * HEAD_W, (hd + 1) * HEAD_W)
        lb = lb_all[:, cols]
        hq = hq_ref[0, rows, cols]
        f = lb + (1.0 - lb) * _sigmoid(hf_ref[0, rows, cols])
        glog = jnp.log(f)
        kk = 1.0 - f
        q = hq * _sigmoid(hq)
        v = hi_ref[0, rows, cols]
        b = _mm(tril, glog, _HI)
        st = state_ref[hd]

        bf = jnp.bfloat16
        vb = v.astype(bf)
        o = _mm_nt((q * jnp.exp(b)).astype(bf), st.astype(bf))

        o_sub = [jnp.zeros((SUB, HEAD_W), jnp.float32)]
        for i in range(1, n_sub):
            beta = b[i * SUB:i * SUB + 1, :]
            qt = q[i * SUB:(i + 1) * SUB, :] * jnp.exp(b[i * SUB:(i + 1) * SUB, :] - beta)
            kt = kk * jnp.exp(jnp.minimum(beta - b, 0.0))
            p = _mm_nt(qt.astype(bf), kt.astype(bf))
            p = jnp.where(sub_col < i * SUB, p, 0.0)
            o_sub.append(_mm(p.astype(bf), vb))
        o = o + jnp.concatenate(o_sub, axis=0)

        for s in range(SUB):
            b_s, k_s, v_s = group_rows(b, s), group_rows(kk, s), group_rows(v, s)
            w = q * k_s * jnp.exp(jnp.where(tloc >= s, b - b_s, _NEG_INF))
            o = o + jnp.sum(w, axis=1, keepdims=True) * v_s

        b_last = b[C - 1:C, :]
        kdec = kk * jnp.exp(b_last - b)
        state_ref[hd] = st * jnp.exp(b_last) + _mm_tn(vb, kdec.astype(bf))

        hgate = hgate_ref[0, rows, cols]
        y = _rms(o, g_out) * (hgate * _sigmoid(hgate))
        o_ref[0, rows, cols] = y.astype(o_ref.dtype)

    def chunk(ci, carry):
        rows = pl.ds(pl.multiple_of(ci * C, C), C)
        for hd in range(n_heads):
            head_chunk(rows, hd)
        return carry

    lax.fori_loop(0, n_chunks, chunk, 0)


def _hgrn(hg_in, lb_theta, g_hgrn_out, layer, sb):
    B, S, _ = hg_in.shape
    H = HGRN_HEADS
    W = H * HEAD_W
    spec = lambda off: pl.BlockSpec((1, sb, W), lambda b, i: (b, i, off))
    n_slots = lb_theta.shape[0]
    return pl.pallas_call(
        functools.partial(_hgrn_kernel, layer=layer, n_chunks=sb // HGRN_CHUNK),
        grid=(B, S // sb),
        in_specs=[spec(0), spec(1), spec(2), spec(3),
                  pl.BlockSpec((n_slots, W), lambda b, i: (0, 0)),
                  pl.BlockSpec((1, HEAD_W), lambda b, i: (0, 0))],
        out_specs=pl.BlockSpec((1, sb, W), lambda b, i: (b, i, 0)),
        out_shape=jax.ShapeDtypeStruct((B, S, W), jnp.bfloat16),
        scratch_shapes=[pltpu.VMEM((H, HEAD_W, HEAD_W), jnp.float32)],
        compiler_params=pltpu.CompilerParams(
            dimension_semantics=("parallel", "arbitrary"), vmem_limit_bytes=VMEM_LIMIT),
        name="hgrn2",
    )(hg_in, hg_in, hg_in, hg_in, lb_theta, g_hgrn_out.reshape(1, HEAD_W))


def _outproj_kernel(ao_ref, ho_ref, x_ref, wo_ref, gpost_ref, gt_ref, gpre_ref, sc_ref, sh_ref, wq_ref,
                    x1_ref, h2_ref, q_ref):
    n_a = ao_ref.shape[2]
    y = _mm(ao_ref[0], wo_ref[0:n_a, :]) + _mm(ho_ref[0], wo_ref[n_a:, :])
    x1 = x_ref[0] + gt_ref[0] * _rms(y, gpost_ref[...])
    x1_ref[0] = x1
    h2 = _rms(x1, gpre_ref[...]) * (1.0 + sc_ref[0]) + sh_ref[0]
    h2_ref[0] = h2
    q_ref[0] = _mm(h2.astype(jnp.bfloat16), wq_ref[...]).astype(q_ref.dtype)


def _out_projection(ao, ho, x, w_out_bf16, g_post, gt1, g_pre, sc2, sh2, w_pq_bf16, tm):
    B, S, D = x.shape
    n_a, n_h, n_q = ao.shape[2], ho.shape[2], w_pq_bf16.shape[1]
    vec = pl.BlockSpec((1, 1, D), lambda b, i: (b, 0, 0))
    par = pl.BlockSpec((1, D), lambda b, i: (0, 0))
    row = lambda n: pl.BlockSpec((1, tm, n), lambda b, i: (b, i, 0))
    return pl.pallas_call(
        _outproj_kernel,
        grid=(B, S // tm),
        in_specs=[row(n_a), row(n_h), row(D),
                  pl.BlockSpec((n_a + n_h, D), lambda b, i: (0, 0)),
                  par, vec, par, vec, vec,
                  pl.BlockSpec((D, n_q), lambda b, i: (0, 0))],
        out_specs=[row(D), row(D), row(n_q)],
        out_shape=[jax.ShapeDtypeStruct((B, S, D), jnp.float32),
                   jax.ShapeDtypeStruct((B, S, D), jnp.float32),
                   jax.ShapeDtypeStruct((B, S, n_q), jnp.bfloat16)],
        compiler_params=pltpu.CompilerParams(
            dimension_semantics=("parallel", "parallel"), vmem_limit_bytes=VMEM_LIMIT),
        name="outproj_norms_peerq",
    )(ao, ho, x, w_out_bf16, g_post.reshape(1, D), gt1, g_pre.reshape(1, D), sc2, sh2, w_pq_bf16)


def _pair_list():
    return [(a, b) for a in range(PEER_TOPK) for b in range(PEER_TOPK) if (a + 1) * (b + 1) <= PEER_TOPK]


def _topk_kernel(q_ref, keys_ref, idx_ref, gate_ref, v_scr, i_scr, cand_scr, cidx_scr, ts_scr, sel_scr):
    K = PEER_TOPK
    tb = q_ref.shape[0]
    pairs = _pair_list()
    n_cand = cand_scr.shape[0]
    kio = lax.broadcasted_iota(jnp.int32, (N_KEYS, tb), 0).astype(jnp.float32)
    pio = lax.broadcasted_iota(jnp.int32, (n_cand, tb), 0).astype(jnp.float32)

    for h in range(PEER_HEADS):
        for j in range(2):
            c0 = (h * 2 + j) * N_KEYS
            s = _mm_nt(keys_ref[h, j], q_ref[:, c0:c0 + N_KEYS])
            for r in range(K):
                m = jnp.max(s, axis=0, keepdims=True)
                am = jnp.min(jnp.where(s == m, kio, float(N_KEYS)), axis=0, keepdims=True)
                v_scr[j, r:r + 1, :] = m
                i_scr[j, r:r + 1, :] = am
                s = jnp.where(kio == am, _NEG_INF, s)
        v1, v2 = v_scr[0], v_scr[1]
        i1, i2 = i_scr[0], i_scr[1]
        cand_scr[...] = jnp.full(cand_scr.shape, _NEG_INF, jnp.float32)
        cidx_scr[...] = jnp.zeros(cidx_scr.shape, jnp.float32)
        off = 0
        for a in range(K):
            nb = sum(1 for (aa, _) in pairs if aa == a)
            cand_scr[off:off + nb, :] = v1[a:a + 1, :] + v2[0:nb, :]
            cidx_scr[off:off + nb, :] = i1[a:a + 1, :] * float(N_KEYS) + i2[0:nb, :]
            off += nb
        cand = cand_scr[...]
        cidx = cidx_scr[...]
        for r in range(K):
            m = jnp.max(cand, axis=0, keepdims=True)
            pos = jnp.min(jnp.where(cand == m, pio, float(n_cand)), axis=0, keepdims=True)
            hit = pio == pos
            ts_scr[r:r + 1, :] = m
            sel_scr[h * K + r:h * K + r + 1, :] = jnp.sum(jnp.where(hit, cidx, 0.0), axis=0, keepdims=True)
            cand = jnp.where(hit, _NEG_INF, cand)
        ts = ts_scr[...]
        e = jnp.exp(ts - ts[0:1, :])
        gate_ref[h * K:(h + 1) * K, :] = e / jnp.sum(e, axis=0, keepdims=True)
    idx_ref[...] = sel_scr[...].T.astype(jnp.int32)


def _peer_topk(q, sub_keys_bf16, tb):
    T = q.shape[0]
    n_sel = PEER_HEADS * PEER_TOPK
    n_cand = -(-len(_pair_list()) // 8) * 8
    return pl.pallas_call(
        _topk_kernel,
        grid=(T // tb,),
        in_specs=[pl.BlockSpec((tb, q.shape[1]), lambda i: (i, 0)),
                  pl.BlockSpec(sub_keys_bf16.shape, lambda i: (0, 0, 0, 0))],
        out_specs=[pl.BlockSpec((tb, n_sel), lambda i: (i, 0)),
                   pl.BlockSpec((n_sel, tb), lambda i: (0, i))],
        out_shape=[jax.ShapeDtypeStruct((T, n_sel), jnp.int32),
                   jax.ShapeDtypeStruct((n_sel, T), jnp.float32)],
        scratch_shapes=[pltpu.VMEM((2, PEER_TOPK, tb), jnp.float32),
                        pltpu.VMEM((2, PEER_TOPK, tb), jnp.float32),
                        pltpu.VMEM((n_cand, tb), jnp.float32),
                        pltpu.VMEM((n_cand, tb), jnp.float32),
                        pltpu.VMEM((PEER_TOPK, tb), jnp.float32),
                        pltpu.VMEM((n_sel, tb), jnp.float32)],
        compiler_params=pltpu.CompilerParams(
            dimension_semantics=("parallel",), vmem_limit_bytes=VMEM_LIMIT),
        name="peer_topk",
    )(q, sub_keys_bf16)


def _peer_kernel(idx_ref, h_ref, gate_ref, x1_ref, gt_ref, g_ref, uv_ref, o_ref, buf, sem, y_scr, *, n_slots):
    tb, D = h_ref.shape
    n_sel = idx_ref.shape[1]
    n_c = D // HEAD_W

    def row_copy(t, k, slot):
        e = idx_ref[t, k]
        return pltpu.make_async_copy(uv_ref.at[e], buf.at[slot, :, k], sem.at[slot])

    def start_token(t):
        slot = t % n_slots
        for k in range(n_sel):
            row_copy(t, k, slot).start(priority=k % 2)

    def wait_token(t):
        slot = t % n_slots
        pltpu.make_async_copy(buf.at[slot], buf.at[slot], sem.at[slot]).wait()

    for t in range(n_slots - 1):
        start_token(t)

    lane = lax.broadcasted_iota(jnp.int32, gate_ref.shape, 1)

    def token(t, carry):
        @pl.when(t + n_slots - 1 < tb)
        def _():
            start_token(t + n_slots - 1)

        wait_token(t)
        slot = t % n_slots
        h = h_ref[pl.ds(t, 1), :]
        p = buf[slot, 0] * h[:, 0:HEAD_W]
        for c in range(1, n_c):
            p = p + buf[slot, c] * h[:, c * HEAD_W:(c + 1) * HEAD_W]
        a = jnp.sum(p, axis=1, keepdims=True)
        gcol = jnp.sum(jnp.where(lane == t, gate_ref[...], 0.0), axis=1, keepdims=True)
        w = gcol * (0.5 * a * (1.0 + lax.erf(a * (1.0 / math.sqrt(2.0)))))
        y_scr[pl.ds(t, 1), :] = jnp.concatenate(
            [jnp.sum(w * buf[slot, n_c + c], axis=0, keepdims=True) for c in range(n_c)], axis=1)
        return carry

    lax.fori_loop(0, tb, token, 0)
    o_ref[...] = x1_ref[...] + gt_ref[0] * _rms(y_scr[...], g_ref[...])


def _peer_mix(idx, h2, gate_t, x1, gt2, g_post, uv, seq_len, tb, n_slots):
    T, D = h2.shape
    n_sel = idx.shape[1]
    blocks_per_seq = seq_len // tb
    rows = pl.BlockSpec((tb, D), lambda i: (i, 0))
    return pl.pallas_call(
        functools.partial(_peer_kernel, n_slots=n_slots),
        grid=(T // tb,),
        in_specs=[pl.BlockSpec((tb, n_sel), lambda i: (i, 0), memory_space=pltpu.SMEM),
                  rows,
                  pl.BlockSpec((n_sel, tb), lambda i: (0, i)),
                  rows,
                  pl.BlockSpec((1, 1, D), lambda i: (i // blocks_per_seq, 0, 0)),
                  pl.BlockSpec((1, D), lambda i: (0, 0)),
                  pl.BlockSpec(memory_space=pl.ANY)],
        out_specs=rows,
        out_shape=jax.ShapeDtypeStruct((T, D), jnp.float32),
        scratch_shapes=[pltpu.VMEM((n_slots, 2 * D // HEAD_W, n_sel, HEAD_W), jnp.float32),
                        pltpu.SemaphoreType.DMA((n_slots,)),
                        pltpu.VMEM((tb, D), jnp.float32)],
        compiler_params=pltpu.CompilerParams(
            dimension_semantics=("arbitrary",), vmem_limit_bytes=VMEM_LIMIT),
        name="peer_gather_mix",
    )(idx, h2, gate_t, x1, gt2, g_post.reshape(1, D), uv)


def _pick(n, pref):
    t = min(n, pref)
    assert n % t == 0, (n, pref)
    return t


def kernel(x, c, w_ada, b_ada, g_pre_mix, g_post_mix, g_pre_ffn, g_post_ffn, w_in, lam_qk, g_diff_sub,
           lb_theta, g_hgrn_out, w_out, w_pq, sub_keys, expert_u, expert_v):
    B, S, D = x.shape
    depth = w_in.shape[0]
    n_attn = 3 * DIFF_HEADS * HEAD_W
    tm = _pick(S, 512)
    tq = _pick(S, 512)
    sb = _pick(S, 512)
    tb = _pick(S, 128)
    bf = jnp.bfloat16
    for l in range(depth):
        lam_init = 0.8 - 0.6 * math.exp(-0.3 * l)
        mod = _modulation(c, w_ada[l], b_ada[l]).reshape(N_MOD, B, 1, D)
        sh1, sc1, gt1, sh2, sc2, gt2 = (mod[i] for i in range(N_MOD))
        attn_in, hg_in = _in_projection(x, g_pre_mix[l], sc1, sh1, w_in[l].astype(bf), n_attn, tm)
        ao = _diff_attention(attn_in, lam_qk[l], g_diff_sub[l], lam_init, tq)
        ho = _hgrn(hg_in, lb_theta, g_hgrn_out[l], l, sb)
        x1, h2, q = _out_projection(ao, ho, x, w_out[l].astype(bf), g_post_mix[l], gt1, g_pre_ffn[l],
                                    sc2, sh2, w_pq[l].astype(bf), tm)
        idx, gate_t = _peer_topk(q.reshape(B * S, -1), sub_keys[l].astype(bf), tb)
        uv = jnp.concatenate([expert_u[l], expert_v[l]], axis=1).reshape(-1, 2 * D // HEAD_W, HEAD_W)
        out = _peer_mix(idx, h2.reshape(B * S, D), gate_t, x1.reshape(B * S, D), gt2, g_post_ffn[l], uv,
                        S, tb, n_slots=4)
        x = out.reshape(B, S, D)
    return x
```

```python
import functools
import math

import jax
import jax.numpy as jnp
from jax import lax
from jax.experimental import pallas as pl
from jax.experimental.pallas import tpu as pltpu

EPS = 1e-6
N_MOD = 6
DIFF_HEADS = 4
DIFF_QK = 64
HEAD_W = 128
HGRN_HEADS = 4
HGRN_CHUNK = 64
HGRN_SUB = 16
PEER_HEADS = 8
N_KEYS = 128
PEER_TOPK = 16
VMEM_LIMIT = 56 * 1024 * 1024

_HI = lax.Precision.HIGHEST
_NEG_INF = float("-inf")


def _dot(a, b, dims, precision=None):
    return lax.dot_general(a, b, (dims, ((), ())), precision=precision,
                           preferred_element_type=jnp.float32)


def _mm(a, b, precision=None):
    return _dot(a, b, ((1,), (0,)), precision)


def _mm_nt(a, b, precision=None):
    return _dot(a, b, ((1,), (1,)), precision)


def _mm_tn(a, b, precision=None):
    return _dot(a, b, ((0,), (0,)), precision)


def _rms(x, g):
    return x * lax.rsqrt(jnp.mean(x * x, axis=-1, keepdims=True) + EPS) * g


def _sigmoid(x):
    return 1.0 / (1.0 + jnp.exp(-x))


def _mod_kernel(c_ref, w_ref, b_ref, o_ref):
    c = c_ref[...]
    ca = c * _sigmoid(c)
    o_ref[0] = _mm(ca, w_ref[...], _HI) + b_ref[...]


def _modulation(c, w_ada, b_ada):
    B, D = c.shape
    return pl.pallas_call(
        _mod_kernel,
        grid=(N_MOD,),
        in_specs=[pl.BlockSpec((B, D), lambda j: (0, 0)),
                  pl.BlockSpec((D, D), lambda j: (0, j)),
                  pl.BlockSpec((1, D), lambda j: (0, j))],
        out_specs=pl.BlockSpec((1, B, D), lambda j: (j, 0, 0)),
        out_shape=jax.ShapeDtypeStruct((N_MOD, B, D), jnp.float32),
        compiler_params=pltpu.CompilerParams(vmem_limit_bytes=VMEM_LIMIT),
        name="adaln_mod",
    )(c, w_ada, b_ada.reshape(1, N_MOD * D))


def _inproj_kernel(x_ref, g_ref, sc_ref, sh_ref, w_ref, attn_ref, hg_ref, *, n_attn, col_chunk):
    x = x_ref[0]
    h = _rms(x, g_ref[...]) * (1.0 + sc_ref[0]) + sh_ref[0]
    hb = h.astype(jnp.bfloat16)
    n_cols = w_ref.shape[1]
    for c0 in range(0, n_cols, col_chunk):
        r = _mm(hb, w_ref[:, c0:c0 + col_chunk])
        if c0 < n_attn:
            attn_ref[0, :, c0:c0 + col_chunk] = r.astype(attn_ref.dtype)
        else:
            hg_ref[0, :, c0 - n_attn:c0 - n_attn + col_chunk] = r


def _in_projection(x, g, sc, sh, w_in_bf16, n_attn, tm):
    B, S, D = x.shape
    n_cols = w_in_bf16.shape[1]
    vec = pl.BlockSpec((1, 1, D), lambda b, i: (b, 0, 0))
    return pl.pallas_call(
        functools.partial(_inproj_kernel, n_attn=n_attn, col_chunk=512),
        grid=(B, S // tm),
        in_specs=[pl.BlockSpec((1, tm, D), lambda b, i: (b, i, 0)),
                  pl.BlockSpec((1, D), lambda b, i: (0, 0)),
                  vec, vec,
                  pl.BlockSpec((D, n_cols), lambda b, i: (0, 0))],
        out_specs=[pl.BlockSpec((1, tm, n_attn), lambda b, i: (b, i, 0)),
                   pl.BlockSpec((1, tm, n_cols - n_attn), lambda b, i: (b, i, 0))],
        out_shape=[jax.ShapeDtypeStruct((B, S, n_attn), jnp.bfloat16),
                   jax.ShapeDtypeStruct((B, S, n_cols - n_attn), jnp.float32)],
        compiler_params=pltpu.CompilerParams(
            dimension_semantics=("parallel", "parallel"), vmem_limit_bytes=VMEM_LIMIT),
        name="prenorm_inproj",
    )(x, g.reshape(1, D), sc, sh, w_in_bf16)


def _fold_lanes(x, op):
    out = x[:, 0:HEAD_W]
    for i in range(1, x.shape[1] // HEAD_W):
        out = op(out, x[:, i * HEAD_W:(i + 1) * HEAD_W])
    return out


def _attn_kernel(q_ref, k_ref, v_ref, lam_ref, g_ref, o_ref, s_scr, mx_scr, l_scr, acc_scr, *, tq, lam_init):
    qi = pl.program_id(2)
    q = q_ref[0]
    lane = lax.broadcasted_iota(jnp.int32, q.shape, 1)
    qs = q * jnp.asarray(1.0 / math.sqrt(DIFF_QK), q.dtype)
    zero = jnp.zeros_like(qs)
    q2 = jnp.concatenate([jnp.where(lane < DIFF_QK, qs, zero), jnp.where(lane >= DIFF_QK, qs, zero)], axis=0)
    n_rep = tq // HEAD_W

    mx_scr[...] = jnp.full(mx_scr.shape, _NEG_INF, jnp.float32)

    def scores(j, carry):
        s = _mm_nt(q2, k_ref[0, pl.ds(pl.multiple_of(j * tq, tq), tq), :])
        s_scr[j] = s
        mx_scr[...] = jnp.maximum(mx_scr[...], _fold_lanes(s, jnp.maximum))
        return carry

    lax.fori_loop(0, qi, scores, 0)
    s = _mm_nt(q2, k_ref[0, pl.ds(pl.multiple_of(qi * tq, tq), tq), :])
    row = lax.broadcasted_iota(jnp.int32, s.shape, 0) & (tq - 1)
    col = lax.broadcasted_iota(jnp.int32, s.shape, 1)
    s = jnp.where(col <= row, s, _NEG_INF)
    s_scr[qi] = s
    m = jnp.max(jnp.maximum(mx_scr[...], _fold_lanes(s, jnp.maximum)), axis=1, keepdims=True)
    mx_scr[...] = jnp.broadcast_to(m, mx_scr.shape)
    l_scr[...] = jnp.zeros(l_scr.shape, jnp.float32)
    acc_scr[...] = jnp.zeros(acc_scr.shape, jnp.float32)

    def contract(j, carry):
        mb = mx_scr[...]
        p = jnp.exp(s_scr[j] - jnp.concatenate([mb] * n_rep, axis=1))
        l_scr[...] += _fold_lanes(p, jnp.add)
        vb = v_ref[0, pl.ds(pl.multiple_of(j * tq, tq), tq), :]
        acc_scr[...] += _mm(p.astype(vb.dtype), vb)
        return carry

    lax.fori_loop(0, qi + 1, contract, 0)

    lq = lam_ref[...]
    lam = (jnp.exp(jnp.sum(lq[0:1] * lq[1:2], axis=1, keepdims=True))
           - jnp.exp(jnp.sum(lq[2:3] * lq[3:4], axis=1, keepdims=True)) + lam_init)
    o = acc_scr[...] / jnp.sum(l_scr[...], axis=1, keepdims=True)
    o = o[0:tq] - lam * o[tq:2 * tq]
    o = _rms(o, g_ref[...]) * (1.0 - lam_init)
    o_ref[0] = o.astype(o_ref.dtype)


def _diff_attention(attn_in, lam_qk, g_diff_sub, lam_init, tq):
    B, S, _ = attn_in.shape
    H = DIFF_HEADS
    assert tq % HEAD_W == 0 and tq & (tq - 1) == 0, tq
    kv_spec = lambda off: pl.BlockSpec((1, S, HEAD_W), lambda b, h, i: (b, 0, off + h))
    return pl.pallas_call(
        functools.partial(_attn_kernel, tq=tq, lam_init=lam_init),
        grid=(B, H, S // tq),
        in_specs=[pl.BlockSpec((1, tq, HEAD_W), lambda b, h, i: (b, i, h)),
                  kv_spec(H), kv_spec(2 * H),
                  pl.BlockSpec(lam_qk.shape, lambda b, h, i: (0, 0)),
                  pl.BlockSpec((1, HEAD_W), lambda b, h, i: (0, 0))],
        out_specs=pl.BlockSpec((1, tq, HEAD_W), lambda b, h, i: (b, i, h)),
        out_shape=jax.ShapeDtypeStruct((B, S, H * HEAD_W), jnp.bfloat16),
        scratch_shapes=[pltpu.VMEM((S // tq, 2 * tq, tq), jnp.float32),
                        pltpu.VMEM((2 * tq, HEAD_W), jnp.float32),
                        pltpu.VMEM((2 * tq, HEAD_W), jnp.float32),
                        pltpu.VMEM((2 * tq, HEAD_W), jnp.float32)],
        compiler_params=pltpu.CompilerParams(
            dimension_semantics=("parallel", "parallel", "arbitrary"), vmem_limit_bytes=VMEM_LIMIT),
        name="diff_attention",
    )(attn_in, attn_in, attn_in, lam_qk, g_diff_sub.reshape(1, HEAD_W))


def _hgrn_kernel(hq_ref, hf_ref, hi_ref, hgate_ref, lbt_ref, g_ref, o_ref, state_ref, *, layer, n_chunks):
    C, SUB = HGRN_CHUNK, HGRN_SUB
    n_sub = C // SUB
    n_heads = state_ref.shape[0]

    @pl.when(pl.program_id(1) == 0)
    def _():
        state_ref[...] = jnp.zeros(state_ref.shape, jnp.float32)

    th = lbt_ref[...]
    e = jnp.exp(th - jnp.max(th, axis=0, keepdims=True))
    lb_all = jnp.sum(e[0:layer + 1], axis=0, keepdims=True) / jnp.sum(e, axis=0, keepdims=True)

    r_io = lax.broadcasted_iota(jnp.int32, (C, C), 0)
    c_io = lax.broadcasted_iota(jnp.int32, (C, C), 1)
    tril = (c_io <= r_io).astype(jnp.float32)
    tloc = lax.broadcasted_iota(jnp.int32, (C, HEAD_W), 0) % SUB
    sub_col = lax.broadcasted_iota(jnp.int32, (SUB, C), 1)
    g_out = g_ref[...]

    def group_rows(x, s):
        return jnp.concatenate(
            [jnp.broadcast_to(x[i * SUB + s:i * SUB + s + 1, :], (SUB, HEAD_W)) for i in range(n_sub)], axis=0)

    def head_chunk(rows, hd):
        cols = slice(hd * HEAD_W, (hd + 1) * HEAD_W)
        lb = lb_all[:, cols]
        hq = hq_ref[0, rows, cols]
        f = lb + (1.0 - lb) * _sigmoid(hf_ref[0, rows, cols])
        glog = jnp.log(f)
        kk = 1.0 - f
        q = hq * _sigmoid(hq)
        v = hi_ref[0, rows, cols]
        b = _mm(tril, glog, _HI)
        st = state_ref[hd]

        bf = jnp.bfloat16
        vb = v.astype(bf)
        o = _mm_nt((q * jnp.exp(b)).astype(bf), st.astype(bf))

        o_sub = [jnp.zeros((SUB, HEAD_W), jnp.float32)]
        for i in range(1, n_sub):
            beta = b[i * SUB:i * SUB + 1, :]
            qt = q[i * SUB:(i + 1) * SUB, :] * jnp.exp(b[i * SUB:(i + 1) * SUB, :] - beta)
            kt = kk * jnp.exp(jnp.minimum(beta - b, 0.0))
            p = _mm_nt(qt.astype(bf), kt.astype(bf))
            p = jnp.where(sub_col < i * SUB, p, 0.0)
            o_sub.append(_mm(p.astype(bf), vb))
        o = o + jnp.concatenate(o_sub, axis=0)

        for s in range(SUB):
            b_s, k_s, v_s = group_rows(b, s), group_rows(kk, s), group_rows(v, s)
            w = q * k_s * jnp.exp(jnp.where(tloc >= s, b - b_s, _NEG_INF))
            o = o + jnp.sum(w, axis=1, keepdims=True) * v_s

        b_last = b[C - 1:C, :]
        kdec = kk * jnp.exp(b_last - b)
        state_ref[hd] = st * jnp.exp(b_last) + _mm_tn(vb, kdec.astype(bf))

        hgate = hgate_ref[0, rows, cols]
        y = _rms(o, g_out) * (hgate * _sigmoid(hgate))
        o_ref[0, rows, cols] = y.astype(o_ref.dtype)

    def chunk(ci, carry):
        rows = pl.ds(pl.multiple_of(ci * C, C), C)
        for hd in range(n_heads):
            head_chunk(rows, hd)
        return carry

    lax.fori_loop(0, n_chunks, chunk, 0)


def _hgrn(hg_in, lb_theta, g_hgrn_out, layer, sb):
    B, S, _ = hg_in.shape
    H = HGRN_HEADS
    W = H * HEAD_W
    spec = lambda off: pl.BlockSpec((1, sb, W), lambda b, i: (b, i, off))
    n_slots = lb_theta.shape[0]
    return pl.pallas_call(
        functools.partial(_hgrn_kernel, layer=layer, n_chunks=sb // HGRN_CHUNK),
        grid=(B, S // sb),
        in_specs=[spec(0), spec(1), spec(2), spec(3),
                  pl.BlockSpec((n_slots, W), lambda b, i: (0, 0)),
                  pl.BlockSpec((1, HEAD_W), lambda b, i: (0, 0))],
        out_specs=pl.BlockSpec((1, sb, W), lambda b, i: (b, i, 0)),
        out_shape=jax.ShapeDtypeStruct((B, S, W), jnp.bfloat16),
        scratch_shapes=[pltpu.VMEM((H, HEAD_W, HEAD_W), jnp.float32)],
        compiler_params=pltpu.CompilerParams(
            dimension_semantics=("parallel", "arbitrary"), vmem_limit_bytes=VMEM_LIMIT),
        name="hgrn2",
    )(hg_in, hg_in, hg_in, hg_in, lb_theta, g_hgrn_out.reshape(1, HEAD_W))


def _outproj_kernel(ao_ref, ho_ref, x_ref, wo_ref, gpost_ref, gt_ref, gpre_ref, sc_ref, sh_ref, wq_ref,
                    x1_ref, h2_ref, q_ref):
    n_a = ao_ref.shape[2]
    y = _mm(ao_ref[0], wo_ref[0:n_a, :]) + _mm(ho_ref[0], wo_ref[n_a:, :])
    x1 = x_ref[0] + gt_ref[0] * _rms(y, gpost_ref[...])
    x1_ref[0] = x1
    h2 = _rms(x1, gpre_ref[...]) * (1.0 + sc_ref[0]) + sh_ref[0]
    h2_ref[0] = h2
    q_ref[0] = _mm(h2.astype(jnp.bfloat16), wq_ref[...]).astype(q_ref.dtype)


def _out_projection(ao, ho, x, w_out_bf16, g_post, gt1, g_pre, sc2, sh2, w_pq_bf16, tm):
    B, S, D = x.shape
    n_a, n_h, n_q = ao.shape[2], ho.shape[2], w_pq_bf16.shape[1]
    vec = pl.BlockSpec((1, 1, D), lambda b, i: (b, 0, 0))
    par = pl.BlockSpec((1, D), lambda b, i: (0, 0))
    row = lambda n: pl.BlockSpec((1, tm, n), lambda b, i: (b, i, 0))
    return pl.pallas_call(
        _outproj_kernel,
        grid=(B, S // tm),
        in_specs=[row(n_a), row(n_h), row(D),
                  pl.BlockSpec((n_a + n_h, D), lambda b, i: (0, 0)),
                  par, vec, par, vec, vec,
                  pl.BlockSpec((D, n_q), lambda b, i: (0, 0))],
        out_specs=[row(D), row(D), row(n_q)],
        out_shape=[jax.ShapeDtypeStruct((B, S, D), jnp.float32),
                   jax.ShapeDtypeStruct((B, S, D), jnp.float32),
                   jax.ShapeDtypeStruct((B, S, n_q), jnp.bfloat16)],
        compiler_params=pltpu.CompilerParams(
            dimension_semantics=("parallel", "parallel"), vmem_limit_bytes=VMEM_LIMIT),
        name="outproj_norms_peerq",
    )(ao, ho, x, w_out_bf16, g_post.reshape(1, D), gt1, g_pre.reshape(1, D), sc2, sh2, w_pq_bf16)


def _pair_list():
    return [(a, b) for a in range(PEER_TOPK) for b in range(PEER_TOPK) if (a + 1) * (b + 1) <= PEER_TOPK]


def _topk_kernel(q_ref, keys_ref, idx_ref, gate_ref, v_scr, i_scr, cand_scr, cidx_scr, ts_scr, sel_scr):
    K = PEER_TOPK
    tb = q_ref.shape[0]
    pairs = _pair_list()
    n_cand = cand_scr.shape[0]
    kio = lax.broadcasted_iota(jnp.int32, (N_KEYS, tb), 0).astype(jnp.float32)
    pio = lax.broadcasted_iota(jnp.int32, (n_cand, tb), 0).astype(jnp.float32)

    for h in range(PEER_HEADS):
        for j in range(2):
            c0 = (h * 2 + j) * N_KEYS
            s = _mm_nt(keys_ref[h, j], q_ref[:, c0:c0 + N_KEYS])
            for r in range(K):
                m = jnp.max(s, axis=0, keepdims=True)
                am = jnp.min(jnp.where(s == m, kio, float(N_KEYS)), axis=0, keepdims=True)
                v_scr[j, r:r + 1, :] = m
                i_scr[j, r:r + 1, :] = am
                s = jnp.where(kio == am, _NEG_INF, s)
        v1, v2 = v_scr[0], v_scr[1]
        i1, i2 = i_scr[0], i_scr[1]
        cand_scr[...] = jnp.full(cand_scr.shape, _NEG_INF, jnp.float32)
        cidx_scr[...] = jnp.zeros(cidx_scr.shape, jnp.float32)
        off = 0
        for a in range(K):
            nb = sum(1 for (aa, _) in pairs if aa == a)
            cand_scr[off:off + nb, :] = v1[a:a + 1, :] + v2[0:nb, :]
            cidx_scr[off:off + nb, :] = i1[a:a + 1, :] * float(N_KEYS) + i2[0:nb, :]
            off += nb
        cand = cand_scr[...]
        cidx = cidx_scr[...]
        for r in range(K):
            m = jnp.max(cand, axis=0, keepdims=True)
            pos = jnp.min(jnp.where(cand == m, pio, float(n_cand)), axis=0, keepdims=True)
            hit = pio == pos
            ts_scr[r:r + 1, :] = m
            sel_scr[h * K + r:h * K + r + 1, :] = jnp.sum(jnp.where(hit, cidx, 0.0), axis=0, keepdims=True)
            cand = jnp.where(hit, _NEG_INF, cand)
        ts = ts_scr[...]
        e = jnp.exp(ts - ts[0:1, :])
        gate_ref[h * K:(h + 1) * K, :] = e / jnp.sum(e, axis=0, keepdims=True)
    idx_ref[...] = sel_scr[...].T.astype(jnp.int32)


def _peer_topk(q, sub_keys_bf16, tb):
    T = q.shape[0]
    n_sel = PEER_HEADS * PEER_TOPK
    n_cand = -(-len(_pair_list()) // 8) * 8
    return pl.pallas_call(
        _topk_kernel,
        grid=(T // tb,),
        in_specs=[pl.BlockSpec((tb, q.shape[1]), lambda i: (i, 0)),
                  pl.BlockSpec(sub_keys_bf16.shape, lambda i: (0, 0, 0, 0))],
        out_specs=[pl.BlockSpec((tb, n_sel), lambda i: (i, 0)),
                   pl.BlockSpec((n_sel, tb), lambda i: (0, i))],
        out_shape=[jax.ShapeDtypeStruct((T, n_sel), jnp.int32),
                   jax.ShapeDtypeStruct((n_sel, T), jnp.float32)],
        scratch_shapes=[pltpu.VMEM((2, PEER_TOPK, tb), jnp.float32),
                        pltpu.VMEM((2, PEER_TOPK, tb), jnp.float32),
                        pltpu.VMEM((n_cand, tb), jnp.float32),
                        pltpu.VMEM((n_cand, tb), jnp.float32),
                        pltpu.VMEM((PEER_TOPK, tb), jnp.float32),
                        pltpu.VMEM((n_sel, tb), jnp.float32)],
        compiler_params=pltpu.CompilerParams(
            dimension_semantics=("parallel",), vmem_limit_bytes=VMEM_LIMIT),
        name="peer_topk",
    )(q, sub_keys_bf16)


def _peer_kernel(idx_ref, h_ref, gate_ref, x1_ref, gt_ref, g_ref, uv_ref, o_ref, buf, sem, y_scr, *, n_slots):
    tb, D = h_ref.shape
    n_sel = idx_ref.shape[1]
    n_c = D // 2 // HEAD_W
    half = D // 2

    def unpack(words):
        lo = lax.bitcast_convert_type(words << 16, jnp.float32)
        hi = lax.bitcast_convert_type(words & jnp.uint32(0xFFFF0000), jnp.float32)
        return lo, hi

    def row_copy(t, k, slot):
        e = idx_ref[t, k]
        return pltpu.make_async_copy(uv_ref.at[e], buf.at[slot, :, k], sem.at[slot])

    def start_rows(t, k0, k1):
        slot = t % n_slots
        for k in range(k0, k1):
            row_copy(t, k, slot).start(priority=k % 2)

    def wait_token(t):
        slot = t % n_slots
        pltpu.make_async_copy(buf.at[slot], buf.at[slot], sem.at[slot]).wait()

    n_ahead = n_slots - 1
    for t in range(n_ahead):
        start_rows(t, 0, n_sel)

    lane = lax.broadcasted_iota(jnp.int32, gate_ref.shape, 1)
    per_piece = n_sel // (2 * n_c)

    def token(t, prefetch):
        def issue(piece):
            if prefetch:
                start_rows(t + n_ahead, piece * per_piece, (piece + 1) * per_piece)

        wait_token(t)
        slot = t % n_slots
        h = h_ref[pl.ds(t, 1), :]
        p = None
        for c in range(n_c):
            lo, hi = unpack(buf[slot, c])
            pc = lo * h[:, c * HEAD_W:(c + 1) * HEAD_W] + hi * h[:, half + c * HEAD_W:half + (c + 1) * HEAD_W]
            p = pc if p is None else p + pc
            issue(c)
        a = jnp.sum(p, axis=1, keepdims=True)
        gcol = jnp.sum(jnp.where(lane == t, gate_ref[...], 0.0), axis=1, keepdims=True)
        w = gcol * (0.5 * a * (1.0 + lax.erf(a * (1.0 / math.sqrt(2.0)))))
        y_lo, y_hi = [], []
        for c in range(n_c):
            lo, hi = unpack(buf[slot, n_c + c])
            y_lo.append(jnp.sum(w * lo, axis=0, keepdims=True))
            y_hi.append(jnp.sum(w * hi, axis=0, keepdims=True))
            issue(n_c + c)
        y_scr[pl.ds(t, 1), :] = jnp.concatenate(y_lo + y_hi, axis=1)

    def main_body(t, carry):
        token(t, True)
        return carry

    def tail_body(t, carry):
        token(t, False)
        return carry

    lax.fori_loop(0, tb - n_ahead, main_body, 0)
    lax.fori_loop(tb - n_ahead, tb, tail_body, 0)
    o_ref[...] = x1_ref[...] + gt_ref[0] * _rms(y_scr[...], g_ref[...])


def _pack_experts(u, v):
    E, D = u.shape

    def pack(w):
        bits = lax.bitcast_convert_type(w.astype(jnp.bfloat16), jnp.uint16).astype(jnp.uint32)
        return bits[:, :D // 2] | (bits[:, D // 2:] << 16)

    return jnp.concatenate([pack(u), pack(v)], axis=1).reshape(E, D // HEAD_W, HEAD_W)


def _peer_mix(idx, h2, gate_t, x1, gt2, g_post, uv, seq_len, tb, n_slots):
    T, D = h2.shape
    n_sel = idx.shape[1]
    blocks_per_seq = seq_len // tb
    rows = pl.BlockSpec((tb, D), lambda i: (i, 0))
    return pl.pallas_call(
        functools.partial(_peer_kernel, n_slots=n_slots),
        grid=(T // tb,),
        in_specs=[pl.BlockSpec((tb, n_sel), lambda i: (i, 0), memory_space=pltpu.SMEM),
                  rows,
                  pl.BlockSpec((n_sel, tb), lambda i: (0, i)),
                  rows,
                  pl.BlockSpec((1, 1, D), lambda i: (i // blocks_per_seq, 0, 0)),
                  pl.BlockSpec((1, D), lambda i: (0, 0)),
                  pl.BlockSpec(memory_space=pl.ANY)],
        out_specs=rows,
        out_shape=jax.ShapeDtypeStruct((T, D), jnp.float32),
        scratch_shapes=[pltpu.VMEM((n_slots, D // HEAD_W, n_sel, HEAD_W), jnp.uint32),
                        pltpu.SemaphoreType.DMA((n_slots,)),
                        pltpu.VMEM((tb, D), jnp.float32)],
        compiler_params=pltpu.CompilerParams(
            dimension_semantics=("arbitrary",), vmem_limit_bytes=VMEM_LIMIT),
        name="peer_gather_mix",
    )(idx, h2, gate_t, x1, gt2, g_post.reshape(1, D), uv)


def _pick(n, pref):
    t = min(n, pref)
    assert n % t == 0, (n, pref)
    return t


def kernel(x, c, w_ada, b_ada, g_pre_mix, g_post_mix, g_pre_ffn, g_post_ffn, w_in, lam_qk, g_diff_sub,
           lb_theta, g_hgrn_out, w_out, w_pq, sub_keys, expert_u, expert_v):
    B, S, D = x.shape
    depth = w_in.shape[0]
    n_attn = 3 * DIFF_HEADS * HEAD_W
    tm = _pick(S, 512)
    tq = _pick(S, 512)
    sb = _pick(S, 512)
    tb = _pick(S, 128)
    bf = jnp.bfloat16
    for l in range(depth):
        lam_init = 0.8 - 0.6 * math.exp(-0.3 * l)
        mod = _modulation(c, w_ada[l], b_ada[l]).reshape(N_MOD, B, 1, D)
        sh1, sc1, gt1, sh2, sc2, gt2 = (mod[i] for i in range(N_MOD))
        attn_in, hg_in = _in_projection(x, g_pre_mix[l], sc1, sh1, w_in[l].astype(bf), n_attn, tm)
        ao = _diff_attention(attn_in, lam_qk[l], g_diff_sub[l], lam_init, tq)
        ho = _hgrn(hg_in, lb_theta, g_hgrn_out[l], l, sb)
        x1, h2, q = _out_projection(ao, ho, x, w_out[l].astype(bf), g_post_mix[l], gt1, g_pre_ffn[l],
                                    sc2, sh2, w_pq[l].astype(bf), tm)
        idx, gate_t = _peer_topk(q.reshape(B * S, -1), sub_keys[l].astype(bf), tb)
        uv = _pack_experts(expert_u[l], expert_v[l])
        out = _peer_mix(idx, h2.reshape(B * S, D), gate_t, x1.reshape(B * S, D), gt2, g_post_ffn[l], uv,
                        S, tb, n_slots=8)
        x = out.reshape(B, S, D)
    return x
```

```python
import functools
import math

import jax
import jax.numpy as jnp
from jax import lax
from jax.experimental import pallas as pl
from jax.experimental.pallas import tpu as pltpu

EPS = 1e-6
N_MOD = 6
DIFF_HEADS = 4
DIFF_QK = 64
HEAD_W = 128
HGRN_HEADS = 4
HGRN_CHUNK = 64
HGRN_SUB = 16
PEER_HEADS = 8
N_KEYS = 128
PEER_TOPK = 16
VMEM_LIMIT = 56 * 1024 * 1024

_HI = lax.Precision.HIGHEST
_NEG_INF = float("-inf")


def _dot(a, b, dims, precision=None):
    return lax.dot_general(a, b, (dims, ((), ())), precision=precision,
                           preferred_element_type=jnp.float32)


def _mm(a, b, precision=None):
    return _dot(a, b, ((1,), (0,)), precision)


def _mm_nt(a, b, precision=None):
    return _dot(a, b, ((1,), (1,)), precision)


def _mm_tn(a, b, precision=None):
    return _dot(a, b, ((0,), (0,)), precision)


def _rms(x, g):
    return x * lax.rsqrt(jnp.mean(x * x, axis=-1, keepdims=True) + EPS) * g


def _sigmoid(x):
    return 1.0 / (1.0 + jnp.exp(-x))


def _mod_kernel(c_ref, w_ref, b_ref, o_ref):
    c = c_ref[...]
    ca = c * _sigmoid(c)
    o_ref[0] = _mm(ca, w_ref[...], _HI) + b_ref[...]


def _modulation(c, w_ada, b_ada):
    B, D = c.shape
    return pl.pallas_call(
        _mod_kernel,
        grid=(N_MOD,),
        in_specs=[pl.BlockSpec((B, D), lambda j: (0, 0)),
                  pl.BlockSpec((D, D), lambda j: (0, j)),
                  pl.BlockSpec((1, D), lambda j: (0, j))],
        out_specs=pl.BlockSpec((1, B, D), lambda j: (j, 0, 0)),
        out_shape=jax.ShapeDtypeStruct((N_MOD, B, D), jnp.float32),
        compiler_params=pltpu.CompilerParams(vmem_limit_bytes=VMEM_LIMIT),
        name="adaln_mod",
    )(c, w_ada, b_ada.reshape(1, N_MOD * D))


def _inproj_kernel(x_ref, g_ref, sc_ref, sh_ref, w_ref, attn_ref, hg_ref, *, n_attn, col_chunk):
    x = x_ref[0]
    h = _rms(x, g_ref[...]) * (1.0 + sc_ref[0]) + sh_ref[0]
    hb = h.astype(jnp.bfloat16)
    n_cols = w_ref.shape[1]
    for c0 in range(0, n_cols, col_chunk):
        r = _mm(hb, w_ref[:, c0:c0 + col_chunk])
        if c0 < n_attn:
            attn_ref[0, :, c0:c0 + col_chunk] = r.astype(attn_ref.dtype)
        else:
            hg_ref[0, :, c0 - n_attn:c0 - n_attn + col_chunk] = r


def _in_projection(x, g, sc, sh, w_in_bf16, n_attn, tm):
    B, S, D = x.shape
    n_cols = w_in_bf16.shape[1]
    vec = pl.BlockSpec((1, 1, D), lambda b, i: (b, 0, 0))
    return pl.pallas_call(
        functools.partial(_inproj_kernel, n_attn=n_attn, col_chunk=512),
        grid=(B, S // tm),
        in_specs=[pl.BlockSpec((1, tm, D), lambda b, i: (b, i, 0)),
                  pl.BlockSpec((1, D), lambda b, i: (0, 0)),
                  vec, vec,
                  pl.BlockSpec((D, n_cols), lambda b, i: (0, 0))],
        out_specs=[pl.BlockSpec((1, tm, n_attn), lambda b, i: (b, i, 0)),
                   pl.BlockSpec((1, tm, n_cols - n_attn), lambda b, i: (b, i, 0))],
        out_shape=[jax.ShapeDtypeStruct((B, S, n_attn), jnp.bfloat16),
                   jax.ShapeDtypeStruct((B, S, n_cols - n_attn), jnp.float32)],
        compiler_params=pltpu.CompilerParams(
            dimension_semantics=("parallel", "parallel"), vmem_limit_bytes=VMEM_LIMIT),
        name="prenorm_inproj",
    )(x, g.reshape(1, D), sc, sh, w_in_bf16)


def _fold_lanes(x, op):
    out = x[:, 0:HEAD_W]
    for i in range(1, x.shape[1] // HEAD_W):
        out = op(out, x[:, i * HEAD_W:(i + 1) * HEAD_W])
    return out


def _attn_kernel(q_ref, k_ref, v_ref, lam_ref, g_ref, o_ref, s_scr, mx_scr, l_scr, acc_scr, *, tq, lam_init):
    qi = pl.program_id(2)
    q = q_ref[0]
    lane = lax.broadcasted_iota(jnp.int32, q.shape, 1)
    qs = q * jnp.asarray(1.0 / math.sqrt(DIFF_QK), q.dtype)
    zero = jnp.zeros_like(qs)
    q2 = jnp.concatenate([jnp.where(lane < DIFF_QK, qs, zero), jnp.where(lane >= DIFF_QK, qs, zero)], axis=0)
    n_rep = tq // HEAD_W

    mx_scr[...] = jnp.full(mx_scr.shape, _NEG_INF, jnp.float32)

    def scores(j, carry):
        s = _mm_nt(q2, k_ref[0, pl.ds(pl.multiple_of(j * tq, tq), tq), :])
        s_scr[j] = s
        mx_scr[...] = jnp.maximum(mx_scr[...], _fold_lanes(s, jnp.maximum))
        return carry

    lax.fori_loop(0, qi, scores, 0)
    s = _mm_nt(q2, k_ref[0, pl.ds(pl.multiple_of(qi * tq, tq), tq), :])
    row = lax.broadcasted_iota(jnp.int32, s.shape, 0) & (tq - 1)
    col = lax.broadcasted_iota(jnp.int32, s.shape, 1)
    s = jnp.where(col <= row, s, _NEG_INF)
    s_scr[qi] = s
    m = jnp.max(jnp.maximum(mx_scr[...], _fold_lanes(s, jnp.maximum)), axis=1, keepdims=True)
    mx_scr[...] = jnp.broadcast_to(m, mx_scr.shape)
    l_scr[...] = jnp.zeros(l_scr.shape, jnp.float32)
    acc_scr[...] = jnp.zeros(acc_scr.shape, jnp.float32)

    def contract(j, carry):
        mb = mx_scr[...]
        p = jnp.exp(s_scr[j] - jnp.concatenate([mb] * n_rep, axis=1))
        l_scr[...] += _fold_lanes(p, jnp.add)
        vb = v_ref[0, pl.ds(pl.multiple_of(j * tq, tq), tq), :]
        acc_scr[...] += _mm(p.astype(vb.dtype), vb)
        return carry

    lax.fori_loop(0, qi + 1, contract, 0)

    lq = lam_ref[...]
    lam = (jnp.exp(jnp.sum(lq[0:1] * lq[1:2], axis=1, keepdims=True))
           - jnp.exp(jnp.sum(lq[2:3] * lq[3:4], axis=1, keepdims=True)) + lam_init)
    o = acc_scr[...] / jnp.sum(l_scr[...], axis=1, keepdims=True)
    o = o[0:tq] - lam * o[tq:2 * tq]
    o = _rms(o, g_ref[...]) * (1.0 - lam_init)
    o_ref[0] = o.astype(o_ref.dtype)


def _diff_attention(attn_in, lam_qk, g_diff_sub, lam_init, tq):
    B, S, _ = attn_in.shape
    H = DIFF_HEADS
    assert tq % HEAD_W == 0 and tq & (tq - 1) == 0, tq
    kv_spec = lambda off: pl.BlockSpec((1, S, HEAD_W), lambda b, h, i: (b, 0, off + h))
    return pl.pallas_call(
        functools.partial(_attn_kernel, tq=tq, lam_init=lam_init),
        grid=(B, H, S // tq),
        in_specs=[pl.BlockSpec((1, tq, HEAD_W), lambda b, h, i: (b, i, h)),
                  kv_spec(H), kv_spec(2 * H),
                  pl.BlockSpec(lam_qk.shape, lambda b, h, i: (0, 0)),
                  pl.BlockSpec((1, HEAD_W), lambda b, h, i: (0, 0))],
        out_specs=pl.BlockSpec((1, tq, HEAD_W), lambda b, h, i: (b, i, h)),
        out_shape=jax.ShapeDtypeStruct((B, S, H * HEAD_W), jnp.bfloat16),
        scratch_shapes=[pltpu.VMEM((S // tq, 2 * tq, tq), jnp.float32),
                        pltpu.VMEM((2 * tq, HEAD_W), jnp.float32),
                        pltpu.VMEM((2 * tq, HEAD_W), jnp.float32),
                        pltpu.VMEM((2 * tq, HEAD_W), jnp.float32)],
        compiler_params=pltpu.CompilerParams(
            dimension_semantics=("parallel", "parallel", "arbitrary"), vmem_limit_bytes=VMEM_LIMIT),
        name="diff_attention",
    )(attn_in, attn_in, attn_in, lam_qk, g_diff_sub.reshape(1, HEAD_W))


def _hgrn_kernel(hq_ref, hf_ref, hi_ref, hgate_ref, lbt_ref, g_ref, o_ref, state_ref, *, layer, n_chunks):
    C, SUB = HGRN_CHUNK, HGRN_SUB
    n_sub = C // SUB
    n_heads = state_ref.shape[0]

    @pl.when(pl.program_id(1) == 0)
    def _():
        state_ref[...] = jnp.zeros(state_ref.shape, jnp.float32)

    th = lbt_ref[...]
    e = jnp.exp(th - jnp.max(th, axis=0, keepdims=True))
    lb_all = jnp.sum(e[0:layer + 1], axis=0, keepdims=True) / jnp.sum(e, axis=0, keepdims=True)

    r_io = lax.broadcasted_iota(jnp.int32, (C, C), 0)
    c_io = lax.broadcasted_iota(jnp.int32, (C, C), 1)
    tril = (c_io <= r_io).astype(jnp.float32)
    tloc = lax.broadcasted_iota(jnp.int32, (C, HEAD_W), 0) % SUB
    sub_col = lax.broadcasted_iota(jnp.int32, (SUB, C), 1)
    g_out = g_ref[...]

    def group_rows(x, s):
        return jnp.concatenate(
            [jnp.broadcast_to(x[i * SUB + s:i * SUB + s + 1, :], (SUB, HEAD_W)) for i in range(n_sub)], axis=0)

    def head_chunk(rows, hd):
        cols = slice(hd * HEAD_W, (hd + 1) * HEAD_W)
        lb = lb_all[:, cols]
        hq = hq_ref[0, rows, cols]
        f = lb + (1.0 - lb) * _sigmoid(hf_ref[0, rows, cols])
        glog = jnp.log(f)
        kk = 1.0 - f
        q = hq * _sigmoid(hq)
        v = hi_ref[0, rows, cols]
        b = _mm(tril, glog, _HI)
        st = state_ref[hd]

        bf = jnp.bfloat16
        vb = v.astype(bf)
        o = _mm_nt((q * jnp.exp(b)).astype(bf), st.astype(bf))

        o_sub = [jnp.zeros((SUB, HEAD_W), jnp.float32)]
        for i in range(1, n_sub):
            beta = b[i * SUB:i * SUB + 1, :]
            qt = q[i * SUB:(i + 1) * SUB, :] * jnp.exp(b[i * SUB:(i + 1) * SUB, :] - beta)
            kt = kk * jnp.exp(jnp.minimum(beta - b, 0.0))
            p = _mm_nt(qt.astype(bf), kt.astype(bf))
            p = jnp.where(sub_col < i * SUB, p, 0.0)
            o_sub.append(_mm(p.astype(bf), vb))
        o = o + jnp.concatenate(o_sub, axis=0)

        for s in range(SUB):
            b_s, k_s, v_s = group_rows(b, s), group_rows(kk, s), group_rows(v, s)
            w = q * k_s * jnp.exp(jnp.where(tloc >= s, b - b_s, _NEG_INF))
            o = o + jnp.sum(w, axis=1, keepdims=True) * v_s

        b_last = b[C - 1:C, :]
        kdec = kk * jnp.exp(b_last - b)
        state_ref[hd] = st * jnp.exp(b_last) + _mm_tn(vb, kdec.astype(bf))

        hgate = hgate_ref[0, rows, cols]
        y = _rms(o, g_out) * (hgate * _sigmoid(hgate))
        o_ref[0, rows, cols] = y.astype(o_ref.dtype)

    def chunk(ci, carry):
        rows = pl.ds(pl.multiple_of(ci * C, C), C)
        for hd in range(n_heads):
            head_chunk(rows, hd)
        return carry

    lax.fori_loop(0, n_chunks, chunk, 0)


def _hgrn(hg_in, lb_theta, g_hgrn_out, layer, sb):
    B, S, _ = hg_in.shape
    H = HGRN_HEADS
    W = H * HEAD_W
    spec = lambda off: pl.BlockSpec((1, sb, W), lambda b, i: (b, i, off))
    n_slots = lb_theta.shape[0]
    return pl.pallas_call(
        functools.partial(_hgrn_kernel, layer=layer, n_chunks=sb // HGRN_CHUNK),
        grid=(B, S // sb),
        in_specs=[spec(0), spec(1), spec(2), spec(3),
                  pl.BlockSpec((n_slots, W), lambda b, i: (0, 0)),
                  pl.BlockSpec((1, HEAD_W), lambda b, i: (0, 0))],
        out_specs=pl.BlockSpec((1, sb, W), lambda b, i: (b, i, 0)),
        out_shape=jax.ShapeDtypeStruct((B, S, W), jnp.bfloat16),
        scratch_shapes=[pltpu.VMEM((H, HEAD_W, HEAD_W), jnp.float32)],
        compiler_params=pltpu.CompilerParams(
            dimension_semantics=("parallel", "arbitrary"), vmem_limit_bytes=VMEM_LIMIT),
        name="hgrn2",
    )(hg_in, hg_in, hg_in, hg_in, lb_theta, g_hgrn_out.reshape(1, HEAD_W))


def _outproj_kernel(ao_ref, ho_ref, x_ref, wo_ref, gpost_ref, gt_ref, gpre_ref, sc_ref, sh_ref, wq_ref,
                    x1_ref, h2_ref, q_ref):
    n_a = ao_ref.shape[2]
    y = _mm(ao_ref[0], wo_ref[0:n_a, :]) + _mm(ho_ref[0], wo_ref[n_a:, :])
    x1 = x_ref[0] + gt_ref[0] * _rms(y, gpost_ref[...])
    x1_ref[0] = x1
    h2 = _rms(x1, gpre_ref[...]) * (1.0 + sc_ref[0]) + sh_ref[0]
    h2_ref[0] = h2
    q_ref[0] = _mm(h2.astype(jnp.bfloat16), wq_ref[...]).astype(q_ref.dtype)


def _out_projection(ao, ho, x, w_out_bf16, g_post, gt1, g_pre, sc2, sh2, w_pq_bf16, tm):
    B, S, D = x.shape
    n_a, n_h, n_q = ao.shape[2], ho.shape[2], w_pq_bf16.shape[1]
    vec = pl.BlockSpec((1, 1, D), lambda b, i: (b, 0, 0))
    par = pl.BlockSpec((1, D), lambda b, i: (0, 0))
    row = lambda n: pl.BlockSpec((1, tm, n), lambda b, i: (b, i, 0))
    return pl.pallas_call(
        _outproj_kernel,
        grid=(B, S // tm),
        in_specs=[row(n_a), row(n_h), row(D),
                  pl.BlockSpec((n_a + n_h, D), lambda b, i: (0, 0)),
                  par, vec, par, vec, vec,
                  pl.BlockSpec((D, n_q), lambda b, i: (0, 0))],
        out_specs=[row(D), row(D), row(n_q)],
        out_shape=[jax.ShapeDtypeStruct((B, S, D), jnp.float32),
                   jax.ShapeDtypeStruct((B, S, D), jnp.float32),
                   jax.ShapeDtypeStruct((B, S, n_q), jnp.bfloat16)],
        compiler_params=pltpu.CompilerParams(
            dimension_semantics=("parallel", "parallel"), vmem_limit_bytes=VMEM_LIMIT),
        name="outproj_norms_peerq",
    )(ao, ho, x, w_out_bf16, g_post.reshape(1, D), gt1, g_pre.reshape(1, D), sc2, sh2, w_pq_bf16)


def _pair_list():
    return [(a, b) for a in range(PEER_TOPK) for b in range(PEER_TOPK) if (a + 1) * (b + 1) <= PEER_TOPK]


def _topk_kernel(q_ref, keys_ref, idx_ref, gate_ref, v_scr, i_scr, cand_scr, cidx_scr, ts_scr, sel_scr):
    K = PEER_TOPK
    tb = q_ref.shape[0]
    pairs = _pair_list()
    n_cand = cand_scr.shape[0]
    kio = lax.broadcasted_iota(jnp.int32, (N_KEYS, tb), 0).astype(jnp.float32)
    pio = lax.broadcasted_iota(jnp.int32, (n_cand, tb), 0).astype(jnp.float32)

    for h in range(PEER_HEADS):
        for j in range(2):
            c0 = (h * 2 + j) * N_KEYS
            s = _mm_nt(keys_ref[h, j], q_ref[:, c0:c0 + N_KEYS])
            for r in range(K):
                m = jnp.max(s, axis=0, keepdims=True)
                am = jnp.min(jnp.where(s == m, kio, float(N_KEYS)), axis=0, keepdims=True)
                v_scr[j, r:r + 1, :] = m
                i_scr[j, r:r + 1, :] = am
                s = jnp.where(kio == am, _NEG_INF, s)
        v1, v2 = v_scr[0], v_scr[1]
        i1, i2 = i_scr[0], i_scr[1]
        cand_scr[...] = jnp.full(cand_scr.shape, _NEG_INF, jnp.float32)
        cidx_scr[...] = jnp.zeros(cidx_scr.shape, jnp.float32)
        off = 0
        for a in range(K):
            nb = sum(1 for (aa, _) in pairs if aa == a)
            cand_scr[off:off + nb, :] = v1[a:a + 1, :] + v2[0:nb, :]
            cidx_scr[off:off + nb, :] = i1[a:a + 1, :] * float(N_KEYS) + i2[0:nb, :]
            off += nb
        cand = cand_scr[...]
        cidx = cidx_scr[...]
        for r in range(K):
            m = jnp.max(cand, axis=0, keepdims=True)
            pos = jnp.min(jnp.where(cand == m, pio, float(n_cand)), axis=0, keepdims=True)
            hit = pio == pos
            ts_scr[r:r + 1, :] = m
            sel_scr[h * K + r:h * K + r + 1, :] = jnp.sum(jnp.where(hit, cidx, 0.0), axis=0, keepdims=True)
            cand = jnp.where(hit, _NEG_INF, cand)
        ts = ts_scr[...]
        e = jnp.exp(ts - ts[0:1, :])
        gate_ref[h * K:(h + 1) * K, :] = e / jnp.sum(e, axis=0, keepdims=True)
    idx_ref[...] = sel_scr[...].T.astype(jnp.int32)


def _peer_topk(q, sub_keys_bf16, tb):
    T = q.shape[0]
    n_sel = PEER_HEADS * PEER_TOPK
    n_cand = -(-len(_pair_list()) // 8) * 8
    return pl.pallas_call(
        _topk_kernel,
        grid=(T // tb,),
        in_specs=[pl.BlockSpec((tb, q.shape[1]), lambda i: (i, 0)),
                  pl.BlockSpec(sub_keys_bf16.shape, lambda i: (0, 0, 0, 0))],
        out_specs=[pl.BlockSpec((tb, n_sel), lambda i: (i, 0)),
                   pl.BlockSpec((n_sel, tb), lambda i: (0, i))],
        out_shape=[jax.ShapeDtypeStruct((T, n_sel), jnp.int32),
                   jax.ShapeDtypeStruct((n_sel, T), jnp.float32)],
        scratch_shapes=[pltpu.VMEM((2, PEER_TOPK, tb), jnp.float32),
                        pltpu.VMEM((2, PEER_TOPK, tb), jnp.float32),
                        pltpu.VMEM((n_cand, tb), jnp.float32),
                        pltpu.VMEM((n_cand, tb), jnp.float32),
                        pltpu.VMEM((PEER_TOPK, tb), jnp.float32),
                        pltpu.VMEM((n_sel, tb), jnp.float32)],
        compiler_params=pltpu.CompilerParams(
            dimension_semantics=("parallel",), vmem_limit_bytes=VMEM_LIMIT),
        name="peer_topk",
    )(q, sub_keys_bf16)


def _peer_kernel(idx_ref, h_ref, gate_ref, x1_ref, gt_ref, g_ref, uv_ref, o_ref, buf, sem, y_scr, *, n_slots):
    tb, D = h_ref.shape
    n_sel = idx_ref.shape[1]
    n_c = D // 2 // HEAD_W
    n_t = 2 * n_c
    pitch = buf.shape[1] // n_sel
    half = D // 2

    def unpack(words):
        lo = lax.bitcast_convert_type(words << 16, jnp.float32)
        hi = lax.bitcast_convert_type(words & jnp.uint32(0xFFFF0000), jnp.float32)
        return lo, hi

    def row_copy(t, k, slot):
        e = idx_ref[t, k]
        return pltpu.make_async_copy(uv_ref.at[e], buf.at[slot, pl.ds(k * pitch, n_t), :], sem.at[slot])

    def tile(slot, c):
        return buf[slot, pl.ds(c, n_sel, stride=pitch), :]

    def start_rows(t, k0, k1):
        slot = t % n_slots
        for k in range(k0, k1):
            row_copy(t, k, slot).start(priority=k % 2)

    def wait_token(t):
        slot = t % n_slots
        done = buf.at[slot, pl.ds(0, n_sel * n_t), :]
        pltpu.make_async_copy(done, done, sem.at[slot]).wait()

    n_ahead = n_slots - 1
    for t in range(n_ahead):
        start_rows(t, 0, n_sel)

    lane = lax.broadcasted_iota(jnp.int32, gate_ref.shape, 1)
    per_piece = n_sel // (2 * n_c)

    def token(t, prefetch):
        def issue(piece):
            if prefetch:
                start_rows(t + n_ahead, piece * per_piece, (piece + 1) * per_piece)

        wait_token(t)
        slot = t % n_slots
        h = h_ref[pl.ds(t, 1), :]
        p = None
        for c in range(n_c):
            lo, hi = unpack(tile(slot, c))
            pc = lo * h[:, c * HEAD_W:(c + 1) * HEAD_W] + hi * h[:, half + c * HEAD_W:half + (c + 1) * HEAD_W]
            p = pc if p is None else p + pc
            issue(c)
        a = jnp.sum(p, axis=1, keepdims=True)
        gcol = jnp.sum(jnp.where(lane == t, gate_ref[...], 0.0), axis=1, keepdims=True)
        w = gcol * (0.5 * a * (1.0 + lax.erf(a * (1.0 / math.sqrt(2.0)))))
        y_lo, y_hi = [], []
        for c in range(n_c):
            lo, hi = unpack(tile(slot, n_c + c))
            y_lo.append(jnp.sum(w * lo, axis=0, keepdims=True))
            y_hi.append(jnp.sum(w * hi, axis=0, keepdims=True))
            issue(n_c + c)
        y_scr[pl.ds(t, 1), :] = jnp.concatenate(y_lo + y_hi, axis=1)

    def main_body(t, carry):
        token(t, True)
        return carry

    def tail_body(t, carry):
        token(t, False)
        return carry

    lax.fori_loop(0, tb - n_ahead, main_body, 0)
    lax.fori_loop(tb - n_ahead, tb, tail_body, 0)
    o_ref[...] = x1_ref[...] + gt_ref[0] * _rms(y_scr[...], g_ref[...])


def _pack_experts(u, v):
    E, D = u.shape

    def pack(w):
        bits = lax.bitcast_convert_type(w.astype(jnp.bfloat16), jnp.uint16).astype(jnp.uint32)
        return bits[:, :D // 2] | (bits[:, D // 2:] << 16)

    return jnp.concatenate([pack(u), pack(v)], axis=1).reshape(E, D // HEAD_W, HEAD_W)


def _peer_mix(idx, h2, gate_t, x1, gt2, g_post, uv, seq_len, tb, n_slots):
    T, D = h2.shape
    n_sel = idx.shape[1]
    blocks_per_seq = seq_len // tb
    rows = pl.BlockSpec((tb, D), lambda i: (i, 0))
    return pl.pallas_call(
        functools.partial(_peer_kernel, n_slots=n_slots),
        grid=(T // tb,),
        in_specs=[pl.BlockSpec((tb, n_sel), lambda i: (i, 0), memory_space=pltpu.SMEM),
                  rows,
                  pl.BlockSpec((n_sel, tb), lambda i: (0, i)),
                  rows,
                  pl.BlockSpec((1, 1, D), lambda i: (i // blocks_per_seq, 0, 0)),
                  pl.BlockSpec((1, D), lambda i: (0, 0)),
                  pl.BlockSpec(memory_space=pl.ANY)],
        out_specs=rows,
        out_shape=jax.ShapeDtypeStruct((T, D), jnp.float32),
        scratch_shapes=[pltpu.VMEM((n_slots, n_sel * (D // HEAD_W + 1), HEAD_W), jnp.uint32),
                        pltpu.SemaphoreType.DMA((n_slots,)),
                        pltpu.VMEM((tb, D), jnp.float32)],
        compiler_params=pltpu.CompilerParams(
            dimension_semantics=("arbitrary",), vmem_limit_bytes=VMEM_LIMIT),
        name="peer_gather_mix",
    )(idx, h2, gate_t, x1, gt2, g_post.reshape(1, D), uv)


def _pick(n, pref):
    t = min(n, pref)
    assert n % t == 0, (n, pref)
    return t


def kernel(x, c, w_ada, b_ada, g_pre_mix, g_post_mix, g_pre_ffn, g_post_ffn, w_in, lam_qk, g_diff_sub,
           lb_theta, g_hgrn_out, w_out, w_pq, sub_keys, expert_u, expert_v):
    B, S, D = x.shape
    depth = w_in.shape[0]
    n_attn = 3 * DIFF_HEADS * HEAD_W
    tm = _pick(S, 512)
    tq = _pick(S, 512)
    sb = _pick(S, 512)
    tb = _pick(S, 128)
    bf = jnp.bfloat16
    for l in range(depth):
        lam_init = 0.8 - 0.6 * math.exp(-0.3 * l)
        mod = _modulation(c, w_ada[l], b_ada[l]).reshape(N_MOD, B, 1, D)
        sh1, sc1, gt1, sh2, sc2, gt2 = (mod[i] for i in range(N_MOD))
        attn_in, hg_in = _in_projection(x, g_pre_mix[l], sc1, sh1, w_in[l].astype(bf), n_attn, tm)
        ao = _diff_attention(attn_in, lam_qk[l], g_diff_sub[l], lam_init, tq)
        ho = _hgrn(hg_in, lb_theta, g_hgrn_out[l], l, sb)
        x1, h2, q = _out_projection(ao, ho, x, w_out[l].astype(bf), g_post_mix[l], gt1, g_pre_ffn[l],
                                    sc2, sh2, w_pq[l].astype(bf), tm)
        idx, gate_t = _peer_topk(q.reshape(B * S, -1), sub_keys[l].astype(bf), tb)
        uv = _pack_experts(expert_u[l], expert_v[l])
        out = _peer_mix(idx, h2.reshape(B * S, D), gate_t, x1.reshape(B * S, D), gt2, g_post_ffn[l], uv,
                        S, tb, n_slots=8)
        x = out.reshape(B, S, D)
    return x
```

```python
import functools
import math

import jax
import jax.numpy as jnp
from jax import lax
from jax.experimental import pallas as pl
from jax.experimental.pallas import tpu as pltpu

EPS = 1e-6
N_MOD = 6
DIFF_HEADS = 4
DIFF_QK = 64
HEAD_W = 128
HGRN_HEADS = 4
HGRN_CHUNK = 64
HGRN_SUB = 16
PEER_HEADS = 8
N_KEYS = 128
PEER_TOPK = 16
VMEM_LIMIT = 56 * 1024 * 1024

_HI = lax.Precision.HIGHEST
_NEG_INF = float("-inf")


def _dot(a, b, dims, precision=None):
    return lax.dot_general(a, b, (dims, ((), ())), precision=precision,
                           preferred_element_type=jnp.float32)


def _mm(a, b, precision=None):
    return _dot(a, b, ((1,), (0,)), precision)


def _mm_nt(a, b, precision=None):
    return _dot(a, b, ((1,), (1,)), precision)


def _mm_tn(a, b, precision=None):
    return _dot(a, b, ((0,), (0,)), precision)


def _rms(x, g):
    return x * lax.rsqrt(jnp.mean(x * x, axis=-1, keepdims=True) + EPS) * g


def _sigmoid(x):
    return 1.0 / (1.0 + jnp.exp(-x))


def _mod_kernel(c_ref, w_ref, b_ref, o_ref):
    c = c_ref[...]
    ca = c * _sigmoid(c)
    o_ref[0] = _mm(ca, w_ref[...], _HI) + b_ref[...]


def _modulation(c, w_ada, b_ada):
    B, D = c.shape
    return pl.pallas_call(
        _mod_kernel,
        grid=(N_MOD,),
        in_specs=[pl.BlockSpec((B, D), lambda j: (0, 0)),
                  pl.BlockSpec((D, D), lambda j: (0, j)),
                  pl.BlockSpec((1, D), lambda j: (0, j))],
        out_specs=pl.BlockSpec((1, B, D), lambda j: (j, 0, 0)),
        out_shape=jax.ShapeDtypeStruct((N_MOD, B, D), jnp.float32),
        compiler_params=pltpu.CompilerParams(vmem_limit_bytes=VMEM_LIMIT),
        name="adaln_mod",
    )(c, w_ada, b_ada.reshape(1, N_MOD * D))


def _inproj_kernel(x_ref, g_ref, sc_ref, sh_ref, w_ref, attn_ref, hg_ref, *, n_attn, col_chunk):
    x = x_ref[0]
    h = _rms(x, g_ref[...]) * (1.0 + sc_ref[0]) + sh_ref[0]
    hb = h.astype(jnp.bfloat16)
    n_cols = w_ref.shape[1]
    for c0 in range(0, n_cols, col_chunk):
        r = _mm(hb, w_ref[:, c0:c0 + col_chunk])
        if c0 < n_attn:
            attn_ref[0, :, c0:c0 + col_chunk] = r.astype(attn_ref.dtype)
        else:
            hg_ref[0, :, c0 - n_attn:c0 - n_attn + col_chunk] = r


def _in_projection(x, g, sc, sh, w_in_bf16, n_attn, tm):
    B, S, D = x.shape
    n_cols = w_in_bf16.shape[1]
    vec = pl.BlockSpec((1, 1, D), lambda b, i: (b, 0, 0))
    return pl.pallas_call(
        functools.partial(_inproj_kernel, n_attn=n_attn, col_chunk=512),
        grid=(B, S // tm),
        in_specs=[pl.BlockSpec((1, tm, D), lambda b, i: (b, i, 0)),
                  pl.BlockSpec((1, D), lambda b, i: (0, 0)),
                  vec, vec,
                  pl.BlockSpec((D, n_cols), lambda b, i: (0, 0))],
        out_specs=[pl.BlockSpec((1, tm, n_attn), lambda b, i: (b, i, 0)),
                   pl.BlockSpec((1, tm, n_cols - n_attn), lambda b, i: (b, i, 0))],
        out_shape=[jax.ShapeDtypeStruct((B, S, n_attn), jnp.bfloat16),
                   jax.ShapeDtypeStruct((B, S, n_cols - n_attn), jnp.float32)],
        compiler_params=pltpu.CompilerParams(
            dimension_semantics=("parallel", "parallel"), vmem_limit_bytes=VMEM_LIMIT),
        name="prenorm_inproj",
    )(x, g.reshape(1, D), sc, sh, w_in_bf16)


def _fold_lanes(x, op):
    out = x[:, 0:HEAD_W]
    for i in range(1, x.shape[1] // HEAD_W):
        out = op(out, x[:, i * HEAD_W:(i + 1) * HEAD_W])
    return out


def _attn_kernel(q_ref, k_ref, v_ref, lam_ref, g_ref, o_ref, s_scr, mx_scr, l_scr, acc_scr, *, tq, lam_init):
    qi = pl.program_id(2)
    q = q_ref[0]
    lane = lax.broadcasted_iota(jnp.int32, q.shape, 1)
    qs = q * jnp.asarray(1.0 / math.sqrt(DIFF_QK), q.dtype)
    zero = jnp.zeros_like(qs)
    q2 = jnp.concatenate([jnp.where(lane < DIFF_QK, qs, zero), jnp.where(lane >= DIFF_QK, qs, zero)], axis=0)
    n_rep = tq // HEAD_W

    mx_scr[...] = jnp.full(mx_scr.shape, _NEG_INF, jnp.float32)

    def scores(j, carry):
        s = _mm_nt(q2, k_ref[0, pl.ds(pl.multiple_of(j * tq, tq), tq), :])
        s_scr[j] = s
        mx_scr[...] = jnp.maximum(mx_scr[...], _fold_lanes(s, jnp.maximum))
        return carry

    lax.fori_loop(0, qi, scores, 0)
    s = _mm_nt(q2, k_ref[0, pl.ds(pl.multiple_of(qi * tq, tq), tq), :])
    row = lax.broadcasted_iota(jnp.int32, s.shape, 0) & (tq - 1)
    col = lax.broadcasted_iota(jnp.int32, s.shape, 1)
    s = jnp.where(col <= row, s, _NEG_INF)
    s_scr[qi] = s
    m = jnp.max(jnp.maximum(mx_scr[...], _fold_lanes(s, jnp.maximum)), axis=1, keepdims=True)
    mx_scr[...] = jnp.broadcast_to(m, mx_scr.shape)
    l_scr[...] = jnp.zeros(l_scr.shape, jnp.float32)
    acc_scr[...] = jnp.zeros(acc_scr.shape, jnp.float32)

    def contract(j, carry):
        mb = mx_scr[...]
        p = jnp.exp(s_scr[j] - jnp.concatenate([mb] * n_rep, axis=1))
        l_scr[...] += _fold_lanes(p, jnp.add)
        vb = v_ref[0, pl.ds(pl.multiple_of(j * tq, tq), tq), :]
        acc_scr[...] += _mm(p.astype(vb.dtype), vb)
        return carry

    lax.fori_loop(0, qi + 1, contract, 0)

    lq = lam_ref[...]
    lam = (jnp.exp(jnp.sum(lq[0:1] * lq[1:2], axis=1, keepdims=True))
           - jnp.exp(jnp.sum(lq[2:3] * lq[3:4], axis=1, keepdims=True)) + lam_init)
    o = acc_scr[...] / jnp.sum(l_scr[...], axis=1, keepdims=True)
    o = o[0:tq] - lam * o[tq:2 * tq]
    o = _rms(o, g_ref[...]) * (1.0 - lam_init)
    o_ref[0] = o.astype(o_ref.dtype)


def _diff_attention(attn_in, lam_qk, g_diff_sub, lam_init, tq):
    B, S, _ = attn_in.shape
    H = DIFF_HEADS
    assert tq % HEAD_W == 0 and tq & (tq - 1) == 0, tq
    kv_spec = lambda off: pl.BlockSpec((1, S, HEAD_W), lambda b, h, i: (b, 0, off + h))
    return pl.pallas_call(
        functools.partial(_attn_kernel, tq=tq, lam_init=lam_init),
        grid=(B, H, S // tq),
        in_specs=[pl.BlockSpec((1, tq, HEAD_W), lambda b, h, i: (b, i, h)),
                  kv_spec(H), kv_spec(2 * H),
                  pl.BlockSpec(lam_qk.shape, lambda b, h, i: (0, 0)),
                  pl.BlockSpec((1, HEAD_W), lambda b, h, i: (0, 0))],
        out_specs=pl.BlockSpec((1, tq, HEAD_W), lambda b, h, i: (b, i, h)),
        out_shape=jax.ShapeDtypeStruct((B, S, H * HEAD_W), jnp.bfloat16),
        scratch_shapes=[pltpu.VMEM((S // tq, 2 * tq, tq), jnp.float32),
                        pltpu.VMEM((2 * tq, HEAD_W), jnp.float32),
                        pltpu.VMEM((2 * tq, HEAD_W), jnp.float32),
                        pltpu.VMEM((2 * tq, HEAD_W), jnp.float32)],
        compiler_params=pltpu.CompilerParams(
            dimension_semantics=("parallel", "parallel", "arbitrary"), vmem_limit_bytes=VMEM_LIMIT),
        name="diff_attention",
    )(attn_in, attn_in, attn_in, lam_qk, g_diff_sub.reshape(1, HEAD_W))


def _hgrn_kernel(hq_ref, hf_ref, hi_ref, hgate_ref, lbt_ref, g_ref, o_ref, state_ref, *, layer, n_chunks):
    C, SUB = HGRN_CHUNK, HGRN_SUB
    n_sub = C // SUB
    n_heads = state_ref.shape[0]

    @pl.when(pl.program_id(1) == 0)
    def _():
        state_ref[...] = jnp.zeros(state_ref.shape, jnp.float32)

    th = lbt_ref[...]
    e = jnp.exp(th - jnp.max(th, axis=0, keepdims=True))
    lb_all = jnp.sum(e[0:layer + 1], axis=0, keepdims=True) / jnp.sum(e, axis=0, keepdims=True)

    r_io = lax.broadcasted_iota(jnp.int32, (C, C), 0)
    c_io = lax.broadcasted_iota(jnp.int32, (C, C), 1)
    tril = (c_io <= r_io).astype(jnp.float32)
    tloc = lax.broadcasted_iota(jnp.int32, (C, HEAD_W), 0) % SUB
    sub_col = lax.broadcasted_iota(jnp.int32, (SUB, C), 1)
    g_out = g_ref[...]

    def group_rows(x, s):
        return jnp.concatenate(
            [jnp.broadcast_to(x[i * SUB + s:i * SUB + s + 1, :], (SUB, HEAD_W)) for i in range(n_sub)], axis=0)

    def head_chunk(rows, hd):
        cols = slice(hd * HEAD_W, (hd + 1) * HEAD_W)
        lb = lb_all[:, cols]
        hq = hq_ref[0, rows, cols]
        f = lb + (1.0 - lb) * _sigmoid(hf_ref[0, rows, cols])
        glog = jnp.log(f)
        kk = 1.0 - f
        q = hq * _sigmoid(hq)
        v = hi_ref[0, rows, cols]
        b = _mm(tril, glog, _HI)
        st = state_ref[hd]

        bf = jnp.bfloat16
        vb = v.astype(bf)
        o = _mm_nt((q * jnp.exp(b)).astype(bf), st.astype(bf))

        o_sub = [jnp.zeros((SUB, HEAD_W), jnp.float32)]
        for i in range(1, n_sub):
            beta = b[i * SUB:i * SUB + 1, :]
            qt = q[i * SUB:(i + 1) * SUB, :] * jnp.exp(b[i * SUB:(i + 1) * SUB, :] - beta)
            kt = kk * jnp.exp(jnp.minimum(beta - b, 0.0))
            p = _mm_nt(qt.astype(bf), kt.astype(bf))
            p = jnp.where(sub_col < i * SUB, p, 0.0)
            o_sub.append(_mm(p.astype(bf), vb))
        o = o + jnp.concatenate(o_sub, axis=0)

        for s in range(SUB):
            b_s, k_s, v_s = group_rows(b, s), group_rows(kk, s), group_rows(v, s)
            w = q * k_s * jnp.exp(jnp.where(tloc >= s, b - b_s, _NEG_INF))
            o = o + jnp.sum(w, axis=1, keepdims=True) * v_s

        b_last = b[C - 1:C, :]
        kdec = kk * jnp.exp(b_last - b)
        state_ref[hd] = st * jnp.exp(b_last) + _mm_tn(vb, kdec.astype(bf))

        hgate = hgate_ref[0, rows, cols]
        y = _rms(o, g_out) * (hgate * _sigmoid(hgate))
        o_ref[0, rows, cols] = y.astype(o_ref.dtype)

    def chunk(ci, carry):
        rows = pl.ds(pl.multiple_of(ci * C, C), C)
        for hd in range(n_heads):
            head_chunk(rows, hd)
        return carry

    lax.fori_loop(0, n_chunks, chunk, 0)


def _hgrn(hg_in, lb_theta, g_hgrn_out, layer, sb):
    B, S, _ = hg_in.shape
    H = HGRN_HEADS
    W = H * HEAD_W
    spec = lambda off: pl.BlockSpec((1, sb, W), lambda b, i: (b, i, off))
    n_slots = lb_theta.shape[0]
    return pl.pallas_call(
        functools.partial(_hgrn_kernel, layer=layer, n_chunks=sb // HGRN_CHUNK),
        grid=(B, S // sb),
        in_specs=[spec(0), spec(1), spec(2), spec(3),
                  pl.BlockSpec((n_slots, W), lambda b, i: (0, 0)),
                  pl.BlockSpec((1, HEAD_W), lambda b, i: (0, 0))],
        out_specs=pl.BlockSpec((1, sb, W), lambda b, i: (b, i, 0)),
        out_shape=jax.ShapeDtypeStruct((B, S, W), jnp.bfloat16),
        scratch_shapes=[pltpu.VMEM((H, HEAD_W, HEAD_W), jnp.float32)],
        compiler_params=pltpu.CompilerParams(
            dimension_semantics=("parallel", "arbitrary"), vmem_limit_bytes=VMEM_LIMIT),
        name="hgrn2",
    )(hg_in, hg_in, hg_in, hg_in, lb_theta, g_hgrn_out.reshape(1, HEAD_W))


def _outproj_kernel(ao_ref, ho_ref, x_ref, wo_ref, gpost_ref, gt_ref, gpre_ref, sc_ref, sh_ref, wq_ref,
                    x1_ref, h2_ref, q_ref):
    n_a = ao_ref.shape[2]
    y = _mm(ao_ref[0], wo_ref[0:n_a, :]) + _mm(ho_ref[0], wo_ref[n_a:, :])
    x1 = x_ref[0] + gt_ref[0] * _rms(y, gpost_ref[...])
    x1_ref[0] = x1
    h2 = _rms(x1, gpre_ref[...]) * (1.0 + sc_ref[0]) + sh_ref[0]
    h2_ref[0] = h2
    q_ref[0] = _mm(h2.astype(jnp.bfloat16), wq_ref[...]).astype(q_ref.dtype)


def _out_projection(ao, ho, x, w_out_bf16, g_post, gt1, g_pre, sc2, sh2, w_pq_bf16, tm):
    B, S, D = x.shape
    n_a, n_h, n_q = ao.shape[2], ho.shape[2], w_pq_bf16.shape[1]
    vec = pl.BlockSpec((1, 1, D), lambda b, i: (b, 0, 0))
    par = pl.BlockSpec((1, D), lambda b, i: (0, 0))
    row = lambda n: pl.BlockSpec((1, tm, n), lambda b, i: (b, i, 0))
    return pl.pallas_call(
        _outproj_kernel,
        grid=(B, S // tm),
        in_specs=[row(n_a), row(n_h), row(D),
                  pl.BlockSpec((n_a + n_h, D), lambda b, i: (0, 0)),
                  par, vec, par, vec, vec,
                  pl.BlockSpec((D, n_q), lambda b, i: (0, 0))],
        out_specs=[row(D), row(D), row(n_q)],
        out_shape=[jax.ShapeDtypeStruct((B, S, D), jnp.float32),
                   jax.ShapeDtypeStruct((B, S, D), jnp.float32),
                   jax.ShapeDtypeStruct((B, S, n_q), jnp.bfloat16)],
        compiler_params=pltpu.CompilerParams(
            dimension_semantics=("parallel", "parallel"), vmem_limit_bytes=VMEM_LIMIT),
        name="outproj_norms_peerq",
    )(ao, ho, x, w_out_bf16, g_post.reshape(1, D), gt1, g_pre.reshape(1, D), sc2, sh2, w_pq_bf16)


def _pair_list():
    return [(a, b) for a in range(PEER_TOPK) for b in range(PEER_TOPK) if (a + 1) * (b + 1) <= PEER_TOPK]


def _topk_kernel(q_ref, keys_ref, idx_ref, gate_ref, v_scr, i_scr, cand_scr, cidx_scr, ts_scr, sel_scr):
    K = PEER_TOPK
    tb = q_ref.shape[0]
    pairs = _pair_list()
    n_cand = cand_scr.shape[0]
    kio = lax.broadcasted_iota(jnp.int32, (N_KEYS, tb), 0).astype(jnp.float32)
    pio = lax.broadcasted_iota(jnp.int32, (n_cand, tb), 0).astype(jnp.float32)

    for h in range(PEER_HEADS):
        for j in range(2):
            c0 = (h * 2 + j) * N_KEYS
            s = _mm_nt(keys_ref[h, j], q_ref[:, c0:c0 + N_KEYS])
            for r in range(K):
                m = jnp.max(s, axis=0, keepdims=True)
                am = jnp.min(jnp.where(s == m, kio, float(N_KEYS)), axis=0, keepdims=True)
                v_scr[j, r:r + 1, :] = m
                i_scr[j, r:r + 1, :] = am
                s = jnp.where(kio == am, _NEG_INF, s)
        v1, v2 = v_scr[0], v_scr[1]
        i1, i2 = i_scr[0], i_scr[1]
        cand_scr[...] = jnp.full(cand_scr.shape, _NEG_INF, jnp.float32)
        cidx_scr[...] = jnp.zeros(cidx_scr.shape, jnp.float32)
        off = 0
        for a in range(K):
            nb = sum(1 for (aa, _) in pairs if aa == a)
            cand_scr[off:off + nb, :] = v1[a:a + 1, :] + v2[0:nb, :]
            cidx_scr[off:off + nb, :] = i1[a:a + 1, :] * float(N_KEYS) + i2[0:nb, :]
            off += nb
        cand = cand_scr[...]
        cidx = cidx_scr[...]
        for r in range(K):
            m = jnp.max(cand, axis=0, keepdims=True)
            pos = jnp.min(jnp.where(cand == m, pio, float(n_cand)), axis=0, keepdims=True)
            hit = pio == pos
            ts_scr[r:r + 1, :] = m
            sel_scr[h * K + r:h * K + r + 1, :] = jnp.sum(jnp.where(hit, cidx, 0.0), axis=0, keepdims=True)
            cand = jnp.where(hit, _NEG_INF, cand)
        ts = ts_scr[...]
        e = jnp.exp(ts - ts[0:1, :])
        gate_ref[h * K:(h + 1) * K, :] = e / jnp.sum(e, axis=0, keepdims=True)
    idx_ref[...] = sel_scr[...].T.astype(jnp.int32)


def _peer_topk(q, sub_keys_bf16, tb):
    T = q.shape[0]
    n_sel = PEER_HEADS * PEER_TOPK
    n_cand = -(-len(_pair_list()) // 8) * 8
    return pl.pallas_call(
        _topk_kernel,
        grid=(T // tb,),
        in_specs=[pl.BlockSpec((tb, q.shape[1]), lambda i: (i, 0)),
                  pl.BlockSpec(sub_keys_bf16.shape, lambda i: (0, 0, 0, 0))],
        out_specs=[pl.BlockSpec((tb, n_sel), lambda i: (i, 0)),
                   pl.BlockSpec((n_sel, tb), lambda i: (0, i))],
        out_shape=[jax.ShapeDtypeStruct((T, n_sel), jnp.int32),
                   jax.ShapeDtypeStruct((n_sel, T), jnp.float32)],
        scratch_shapes=[pltpu.VMEM((2, PEER_TOPK, tb), jnp.float32),
                        pltpu.VMEM((2, PEER_TOPK, tb), jnp.float32),
                        pltpu.VMEM((n_cand, tb), jnp.float32),
                        pltpu.VMEM((n_cand, tb), jnp.float32),
                        pltpu.VMEM((PEER_TOPK, tb), jnp.float32),
                        pltpu.VMEM((n_sel, tb), jnp.float32)],
        compiler_params=pltpu.CompilerParams(
            dimension_semantics=("parallel",), vmem_limit_bytes=VMEM_LIMIT),
        name="peer_topk",
    )(q, sub_keys_bf16)


def _peer_kernel(idx_ref, h_ref, gate_ref, x1_ref, gt_ref, g_ref, uv_ref, o_ref, buf, sem, y_scr, *, n_slots):
    tb, D = h_ref.shape
    n_sel = idx_ref.shape[1]
    n_c = D // 2 // HEAD_W
    n_t = 2 * n_c
    pitch = buf.shape[1] // n_sel
    half = D // 2

    def unpack(words):
        lo = lax.bitcast_convert_type(words << 16, jnp.float32)
        hi = lax.bitcast_convert_type(words & jnp.uint32(0xFFFF0000), jnp.float32)
        return lo, hi

    def row_copy(t, k, slot):
        e = idx_ref[t, k]
        return pltpu.make_async_copy(uv_ref.at[e], buf.at[slot, pl.ds(k * pitch, n_t), :], sem.at[slot])

    def tile(slot, c):
        return buf[slot, pl.ds(c, n_sel, stride=pitch), :]

    def start_rows(t, slot, k0, k1):
        for k in range(k0, k1):
            row_copy(t, k, slot).start(priority=k % 2)

    def wait_slot(slot):
        done = buf.at[slot, pl.ds(0, n_sel * n_t), :]
        pltpu.make_async_copy(done, done, sem.at[slot]).wait()

    n_ahead = n_slots - 1
    for t in range(n_ahead):
        start_rows(t, t, 0, n_sel)

    lane = lax.broadcasted_iota(jnp.int32, gate_ref.shape, 1)
    per_piece = n_sel // (2 * n_c)

    def token(t, slot, prefetch):
        def issue(piece):
            if prefetch:
                start_rows(t + n_ahead, (slot + n_ahead) % n_slots, piece * per_piece, (piece + 1) * per_piece)

        wait_slot(slot)
        h = h_ref[pl.ds(t, 1), :]
        p = None
        for c in range(n_c):
            lo, hi = unpack(tile(slot, c))
            pc = lo * h[:, c * HEAD_W:(c + 1) * HEAD_W] + hi * h[:, half + c * HEAD_W:half + (c + 1) * HEAD_W]
            p = pc if p is None else p + pc
            issue(c)
        a = jnp.sum(p, axis=1, keepdims=True)
        gcol = jnp.sum(jnp.where(lane == t, gate_ref[...], 0.0), axis=1, keepdims=True)
        w = gcol * (0.5 * a * (1.0 + lax.erf(a * (1.0 / math.sqrt(2.0)))))
        y_lo, y_hi = [], []
        for c in range(n_c):
            lo, hi = unpack(tile(slot, n_c + c))
            y_lo.append(jnp.sum(w * lo, axis=0, keepdims=True))
            y_hi.append(jnp.sum(w * hi, axis=0, keepdims=True))
            issue(n_c + c)
        y_scr[pl.ds(t, 1), :] = jnp.concatenate(y_lo + y_hi, axis=1)

    n_main = tb - n_ahead
    n_groups = n_main // n_slots

    def group(g, carry):
        for j in range(n_slots):
            token(g * n_slots + j, j, True)
        return carry

    lax.fori_loop(0, n_groups, group, 0)
    for t in range(n_groups * n_slots, tb):
        token(t, t % n_slots, t < n_main)
    o_ref[...] = x1_ref[...] + gt_ref[0] * _rms(y_scr[...], g_ref[...])


def _pack_experts(u, v):
    E, D = u.shape

    def pack(w):
        bits = lax.bitcast_convert_type(w.astype(jnp.bfloat16), jnp.uint16).astype(jnp.uint32)
        return bits[:, :D // 2] | (bits[:, D // 2:] << 16)

    return jnp.concatenate([pack(u), pack(v)], axis=1).reshape(E, D // HEAD_W, HEAD_W)


def _peer_mix(idx, h2, gate_t, x1, gt2, g_post, uv, seq_len, tb, n_slots):
    T, D = h2.shape
    n_sel = idx.shape[1]
    blocks_per_seq = seq_len // tb
    rows = pl.BlockSpec((tb, D), lambda i: (i, 0))
    return pl.pallas_call(
        functools.partial(_peer_kernel, n_slots=n_slots),
        grid=(T // tb,),
        in_specs=[pl.BlockSpec((tb, n_sel), lambda i: (i, 0), memory_space=pltpu.SMEM),
                  rows,
                  pl.BlockSpec((n_sel, tb), lambda i: (0, i)),
                  rows,
                  pl.BlockSpec((1, 1, D), lambda i: (i // blocks_per_seq, 0, 0)),
                  pl.BlockSpec((1, D), lambda i: (0, 0)),
                  pl.BlockSpec(memory_space=pl.ANY)],
        out_specs=rows,
        out_shape=jax.ShapeDtypeStruct((T, D), jnp.float32),
        scratch_shapes=[pltpu.VMEM((n_slots, n_sel * (D // HEAD_W + 1), HEAD_W), jnp.uint32),
                        pltpu.SemaphoreType.DMA((n_slots,)),
                        pltpu.VMEM((tb, D), jnp.float32)],
        compiler_params=pltpu.CompilerParams(
            dimension_semantics=("arbitrary",), vmem_limit_bytes=VMEM_LIMIT),
        name="peer_gather_mix",
    )(idx, h2, gate_t, x1, gt2, g_post.reshape(1, D), uv)


def _pick(n, pref):
    t = min(n, pref)
    assert n % t == 0, (n, pref)
    return t


def kernel(x, c, w_ada, b_ada, g_pre_mix, g_post_mix, g_pre_ffn, g_post_ffn, w_in, lam_qk, g_diff_sub,
           lb_theta, g_hgrn_out, w_out, w_pq, sub_keys, expert_u, expert_v):
    B, S, D = x.shape
    depth = w_in.shape[0]
    n_attn = 3 * DIFF_HEADS * HEAD_W
    tm = _pick(S, 512)
    tq = _pick(S, 512)
    sb = _pick(S, 512)
    tb = _pick(S, 128)
    bf = jnp.bfloat16
    for l in range(depth):
        lam_init = 0.8 - 0.6 * math.exp(-0.3 * l)
        mod = _modulation(c, w_ada[l], b_ada[l]).reshape(N_MOD, B, 1, D)
        sh1, sc1, gt1, sh2, sc2, gt2 = (mod[i] for i in range(N_MOD))
        attn_in, hg_in = _in_projection(x, g_pre_mix[l], sc1, sh1, w_in[l].astype(bf), n_attn, tm)
        ao = _diff_attention(attn_in, lam_qk[l], g_diff_sub[l], lam_init, tq)
        ho = _hgrn(hg_in, lb_theta, g_hgrn_out[l], l, sb)
        x1, h2, q = _out_projection(ao, ho, x, w_out[l].astype(bf), g_post_mix[l], gt1, g_pre_ffn[l],
                                    sc2, sh2, w_pq[l].astype(bf), tm)
        idx, gate_t = _peer_topk(q.reshape(B * S, -1), sub_keys[l].astype(bf), tb)
        uv = _pack_experts(expert_u[l], expert_v[l])
        out = _peer_mix(idx, h2.reshape(B * S, D), gate_t, x1.reshape(B * S, D), gt2, g_post_ffn[l], uv,
                        S, tb, n_slots=8)
        x = out.reshape(B, S, D)
    return x
```

```python
import functools
import math

import jax
import jax.numpy as jnp
from jax import lax
from jax.experimental import pallas as pl
from jax.experimental.pallas import tpu as pltpu

EPS = 1e-6
N_MOD = 6
DIFF_HEADS = 4
DIFF_QK = 64
HEAD_W = 128
HGRN_HEADS = 4
HGRN_CHUNK = 64
HGRN_SUB = 16
PEER_HEADS = 8
N_KEYS = 128
PEER_TOPK = 16
VMEM_LIMIT = 56 * 1024 * 1024

_HI = lax.Precision.HIGHEST
_NEG_INF = float("-inf")


def _dot(a, b, dims, precision=None):
    return lax.dot_general(a, b, (dims, ((), ())), precision=precision,
                           preferred_element_type=jnp.float32)


def _mm(a, b, precision=None):
    return _dot(a, b, ((1,), (0,)), precision)


def _mm_nt(a, b, precision=None):
    return _dot(a, b, ((1,), (1,)), precision)


def _mm_tn(a, b, precision=None):
    return _dot(a, b, ((0,), (0,)), precision)


def _rms(x, g):
    return x * lax.rsqrt(jnp.mean(x * x, axis=-1, keepdims=True) + EPS) * g


def _sigmoid(x):
    return 1.0 / (1.0 + jnp.exp(-x))


def _mod_kernel(c_ref, w_ref, b_ref, o_ref):
    c = c_ref[...]
    ca = c * _sigmoid(c)
    o_ref[0] = _mm(ca, w_ref[...], _HI) + b_ref[...]


def _modulation(c, w_ada, b_ada):
    B, D = c.shape
    return pl.pallas_call(
        _mod_kernel,
        grid=(N_MOD,),
        in_specs=[pl.BlockSpec((B, D), lambda j: (0, 0)),
                  pl.BlockSpec((D, D), lambda j: (0, j)),
                  pl.BlockSpec((1, D), lambda j: (0, j))],
        out_specs=pl.BlockSpec((1, B, D), lambda j: (j, 0, 0)),
        out_shape=jax.ShapeDtypeStruct((N_MOD, B, D), jnp.float32),
        compiler_params=pltpu.CompilerParams(vmem_limit_bytes=VMEM_LIMIT),
        name="adaln_mod",
    )(c, w_ada, b_ada.reshape(1, N_MOD * D))


def _inproj_kernel(x_ref, g_ref, sc_ref, sh_ref, w_ref, attn_ref, hg_ref, *, n_attn, col_chunk):
    x = x_ref[0]
    h = _rms(x, g_ref[...]) * (1.0 + sc_ref[0]) + sh_ref[0]
    hb = h.astype(jnp.bfloat16)
    n_cols = w_ref.shape[1]
    for c0 in range(0, n_cols, col_chunk):
        r = _mm(hb, w_ref[:, c0:c0 + col_chunk])
        if c0 < n_attn:
            attn_ref[0, :, c0:c0 + col_chunk] = r.astype(attn_ref.dtype)
        else:
            hg_ref[0, :, c0 - n_attn:c0 - n_attn + col_chunk] = r


def _in_projection(x, g, sc, sh, w_in_bf16, n_attn, tm):
    B, S, D = x.shape
    n_cols = w_in_bf16.shape[1]
    vec = pl.BlockSpec((1, 1, D), lambda b, i: (b, 0, 0))
    return pl.pallas_call(
        functools.partial(_inproj_kernel, n_attn=n_attn, col_chunk=512),
        grid=(B, S // tm),
        in_specs=[pl.BlockSpec((1, tm, D), lambda b, i: (b, i, 0)),
                  pl.BlockSpec((1, D), lambda b, i: (0, 0)),
                  vec, vec,
                  pl.BlockSpec((D, n_cols), lambda b, i: (0, 0))],
        out_specs=[pl.BlockSpec((1, tm, n_attn), lambda b, i: (b, i, 0)),
                   pl.BlockSpec((1, tm, n_cols - n_attn), lambda b, i: (b, i, 0))],
        out_shape=[jax.ShapeDtypeStruct((B, S, n_attn), jnp.bfloat16),
                   jax.ShapeDtypeStruct((B, S, n_cols - n_attn), jnp.float32)],
        compiler_params=pltpu.CompilerParams(
            dimension_semantics=("parallel", "parallel"), vmem_limit_bytes=VMEM_LIMIT),
        name="prenorm_inproj",
    )(x, g.reshape(1, D), sc, sh, w_in_bf16)


def _fold_lanes(x, op):
    out = x[:, 0:HEAD_W]
    for i in range(1, x.shape[1] // HEAD_W):
        out = op(out, x[:, i * HEAD_W:(i + 1) * HEAD_W])
    return out


def _attn_kernel(q_ref, k_ref, v_ref, lam_ref, g_ref, o_ref, s_scr, mx_scr, l_scr, acc_scr, *, tq, lam_init):
    qi = pl.program_id(2)
    q = q_ref[0]
    lane = lax.broadcasted_iota(jnp.int32, q.shape, 1)
    qs = q * jnp.asarray(1.0 / math.sqrt(DIFF_QK), q.dtype)
    zero = jnp.zeros_like(qs)
    q2 = jnp.concatenate([jnp.where(lane < DIFF_QK, qs, zero), jnp.where(lane >= DIFF_QK, qs, zero)], axis=0)
    n_rep = tq // HEAD_W

    mx_scr[...] = jnp.full(mx_scr.shape, _NEG_INF, jnp.float32)

    def scores(j, carry):
        s = _mm_nt(q2, k_ref[0, pl.ds(pl.multiple_of(j * tq, tq), tq), :])
        s_scr[j] = s
        mx_scr[...] = jnp.maximum(mx_scr[...], _fold_lanes(s, jnp.maximum))
        return carry

    lax.fori_loop(0, qi, scores, 0)
    s = _mm_nt(q2, k_ref[0, pl.ds(pl.multiple_of(qi * tq, tq), tq), :])
    row = lax.broadcasted_iota(jnp.int32, s.shape, 0) & (tq - 1)
    col = lax.broadcasted_iota(jnp.int32, s.shape, 1)
    s = jnp.where(col <= row, s, _NEG_INF)
    s_scr[qi] = s
    m = jnp.max(jnp.maximum(mx_scr[...], _fold_lanes(s, jnp.maximum)), axis=1, keepdims=True)
    mx_scr[...] = jnp.broadcast_to(m, mx_scr.shape)
    l_scr[...] = jnp.zeros(l_scr.shape, jnp.float32)
    acc_scr[...] = jnp.zeros(acc_scr.shape, jnp.float32)

    def contract(j, carry):
        mb = mx_scr[...]
        p = jnp.exp(s_scr[j] - jnp.concatenate([mb] * n_rep, axis=1))
        l_scr[...] += _fold_lanes(p, jnp.add)
        vb = v_ref[0, pl.ds(pl.multiple_of(j * tq, tq), tq), :]
        acc_scr[...] += _mm(p.astype(vb.dtype), vb)
        return carry

    lax.fori_loop(0, qi + 1, contract, 0)

    lq = lam_ref[...]
    lam = (jnp.exp(jnp.sum(lq[0:1] * lq[1:2], axis=1, keepdims=True))
           - jnp.exp(jnp.sum(lq[2:3] * lq[3:4], axis=1, keepdims=True)) + lam_init)
    o = acc_scr[...] / jnp.sum(l_scr[...], axis=1, keepdims=True)
    o = o[0:tq] - lam * o[tq:2 * tq]
    o = _rms(o, g_ref[...]) * (1.0 - lam_init)
    o_ref[0] = o.astype(o_ref.dtype)


def _diff_attention(attn_in, lam_qk, g_diff_sub, lam_init, tq):
    B, S, _ = attn_in.shape
    H = DIFF_HEADS
    assert tq % HEAD_W == 0 and tq & (tq - 1) == 0, tq
    kv_spec = lambda off: pl.BlockSpec((1, S, HEAD_W), lambda b, h, i: (b, 0, off + h))
    return pl.pallas_call(
        functools.partial(_attn_kernel, tq=tq, lam_init=lam_init),
        grid=(B, H, S // tq),
        in_specs=[pl.BlockSpec((1, tq, HEAD_W), lambda b, h, i: (b, i, h)),
                  kv_spec(H), kv_spec(2 * H),
                  pl.BlockSpec(lam_qk.shape, lambda b, h, i: (0, 0)),
                  pl.BlockSpec((1, HEAD_W), lambda b, h, i: (0, 0))],
        out_specs=pl.BlockSpec((1, tq, HEAD_W), lambda b, h, i: (b, i, h)),
        out_shape=jax.ShapeDtypeStruct((B, S, H * HEAD_W), jnp.bfloat16),
        scratch_shapes=[pltpu.VMEM((S // tq, 2 * tq, tq), jnp.float32),
                        pltpu.VMEM((2 * tq, HEAD_W), jnp.float32),
                        pltpu.VMEM((2 * tq, HEAD_W), jnp.float32),
                        pltpu.VMEM((2 * tq, HEAD_W), jnp.float32)],
        compiler_params=pltpu.CompilerParams(
            dimension_semantics=("parallel", "parallel", "arbitrary"), vmem_limit_bytes=VMEM_LIMIT),
        name="diff_attention",
    )(attn_in, attn_in, attn_in, lam_qk, g_diff_sub.reshape(1, HEAD_W))


def _hgrn_kernel(hq_ref, hf_ref, hi_ref, hgate_ref, lbt_ref, g_ref, o_ref, state_ref, *, layer, n_chunks):
    C, SUB = HGRN_CHUNK, HGRN_SUB
    n_sub = C // SUB
    n_heads = state_ref.shape[0]

    @pl.when(pl.program_id(1) == 0)
    def _():
        state_ref[...] = jnp.zeros(state_ref.shape, jnp.float32)

    th = lbt_ref[...]
    e = jnp.exp(th - jnp.max(th, axis=0, keepdims=True))
    lb_all = jnp.sum(e[0:layer + 1], axis=0, keepdims=True) / jnp.sum(e, axis=0, keepdims=True)

    r_io = lax.broadcasted_iota(jnp.int32, (C, C), 0)
    c_io = lax.broadcasted_iota(jnp.int32, (C, C), 1)
    tril = (c_io <= r_io).astype(jnp.float32)
    tloc = lax.broadcasted_iota(jnp.int32, (C, HEAD_W), 0) % SUB
    sub_col = lax.broadcasted_iota(jnp.int32, (SUB, C), 1)
    g_out = g_ref[...]

    def group_rows(x, s):
        return jnp.concatenate(
            [jnp.broadcast_to(x[i * SUB + s:i * SUB + s + 1, :], (SUB, HEAD_W)) for i in range(n_sub)], axis=0)

    def head_chunk(rows, hd):
        cols = slice(hd * HEAD_W, (hd + 1) * HEAD_W)
        lb = lb_all[:, cols]
        hq = hq_ref[0, rows, cols]
        f = lb + (1.0 - lb) * _sigmoid(hf_ref[0, rows, cols])
        glog = jnp.log(f)
        kk = 1.0 - f
        q = hq * _sigmoid(hq)
        v = hi_ref[0, rows, cols]
        b = _mm(tril, glog, _HI)
        st = state_ref[hd]

        bf = jnp.bfloat16
        vb = v.astype(bf)
        o = _mm_nt((q * jnp.exp(b)).astype(bf), st.astype(bf))

        o_sub = [jnp.zeros((SUB, HEAD_W), jnp.float32)]
        for i in range(1, n_sub):
            beta = b[i * SUB:i * SUB + 1, :]
            qt = q[i * SUB:(i + 1) * SUB, :] * jnp.exp(b[i * SUB:(i + 1) * SUB, :] - beta)
            kt = kk * jnp.exp(jnp.minimum(beta - b, 0.0))
            p = _mm_nt(qt.astype(bf), kt.astype(bf))
            p = jnp.where(sub_col < i * SUB, p, 0.0)
            o_sub.append(_mm(p.astype(bf), vb))
        o = o + jnp.concatenate(o_sub, axis=0)

        for s in range(SUB):
            b_s, k_s, v_s = group_rows(b, s), group_rows(kk, s), group_rows(v, s)
            w = q * k_s * jnp.exp(jnp.where(tloc >= s, b - b_s, _NEG_INF))
            o = o + jnp.sum(w, axis=1, keepdims=True) * v_s

        b_last = b[C - 1:C, :]
        kdec = kk * jnp.exp(b_last - b)
        state_ref[hd] = st * jnp.exp(b_last) + _mm_tn(vb, kdec.astype(bf))

        hgate = hgate_ref[0, rows, cols]
        y = _rms(o, g_out) * (hgate * _sigmoid(hgate))
        o_ref[0, rows, cols] = y.astype(o_ref.dtype)

    def chunk(ci, carry):
        rows = pl.ds(pl.multiple_of(ci * C, C), C)
        for hd in range(n_heads):
            head_chunk(rows, hd)
        return carry

    lax.fori_loop(0, n_chunks, chunk, 0)


def _hgrn(hg_in, lb_theta, g_hgrn_out, layer, sb):
    B, S, _ = hg_in.shape
    H = HGRN_HEADS
    W = H * HEAD_W
    spec = lambda off: pl.BlockSpec((1, sb, W), lambda b, i: (b, i, off))
    n_slots = lb_theta.shape[0]
    return pl.pallas_call(
        functools.partial(_hgrn_kernel, layer=layer, n_chunks=sb // HGRN_CHUNK),
        grid=(B, S // sb),
        in_specs=[spec(0), spec(1), spec(2), spec(3),
                  pl.BlockSpec((n_slots, W), lambda b, i: (0, 0)),
                  pl.BlockSpec((1, HEAD_W), lambda b, i: (0, 0))],
        out_specs=pl.BlockSpec((1, sb, W), lambda b, i: (b, i, 0)),
        out_shape=jax.ShapeDtypeStruct((B, S, W), jnp.bfloat16),
        scratch_shapes=[pltpu.VMEM((H, HEAD_W, HEAD_W), jnp.float32)],
        compiler_params=pltpu.CompilerParams(
            dimension_semantics=("parallel", "arbitrary"), vmem_limit_bytes=VMEM_LIMIT),
        name="hgrn2",
    )(hg_in, hg_in, hg_in, hg_in, lb_theta, g_hgrn_out.reshape(1, HEAD_W))


def _outproj_kernel(ao_ref, ho_ref, x_ref, wo_ref, gpost_ref, gt_ref, gpre_ref, sc_ref, sh_ref, wq_ref,
                    x1_ref, h2_ref, q_ref):
    n_a = ao_ref.shape[2]
    y = _mm(ao_ref[0], wo_ref[0:n_a, :]) + _mm(ho_ref[0], wo_ref[n_a:, :])
    x1 = x_ref[0] + gt_ref[0] * _rms(y, gpost_ref[...])
    x1_ref[0] = x1
    h2 = _rms(x1, gpre_ref[...]) * (1.0 + sc_ref[0]) + sh_ref[0]
    h2_ref[0] = h2
    q_ref[0] = _mm(h2.astype(jnp.bfloat16), wq_ref[...]).astype(q_ref.dtype)


def _out_projection(ao, ho, x, w_out_bf16, g_post, gt1, g_pre, sc2, sh2, w_pq_bf16, tm):
    B, S, D = x.shape
    n_a, n_h, n_q = ao.shape[2], ho.shape[2], w_pq_bf16.shape[1]
    vec = pl.BlockSpec((1, 1, D), lambda b, i: (b, 0, 0))
    par = pl.BlockSpec((1, D), lambda b, i: (0, 0))
    row = lambda n: pl.BlockSpec((1, tm, n), lambda b, i: (b, i, 0))
    return pl.pallas_call(
        _outproj_kernel,
        grid=(B, S // tm),
        in_specs=[row(n_a), row(n_h), row(D),
                  pl.BlockSpec((n_a + n_h, D), lambda b, i: (0, 0)),
                  par, vec, par, vec, vec,
                  pl.BlockSpec((D, n_q), lambda b, i: (0, 0))],
        out_specs=[row(D), row(D), row(n_q)],
        out_shape=[jax.ShapeDtypeStruct((B, S, D), jnp.float32),
                   jax.ShapeDtypeStruct((B, S, D), jnp.float32),
                   jax.ShapeDtypeStruct((B, S, n_q), jnp.bfloat16)],
        compiler_params=pltpu.CompilerParams(
            dimension_semantics=("parallel", "parallel"), vmem_limit_bytes=VMEM_LIMIT),
        name="outproj_norms_peerq",
    )(ao, ho, x, w_out_bf16, g_post.reshape(1, D), gt1, g_pre.reshape(1, D), sc2, sh2, w_pq_bf16)


def _pair_list():
    return [(a, b) for a in range(PEER_TOPK) for b in range(PEER_TOPK) if (a + 1) * (b + 1) <= PEER_TOPK]


def _topk_kernel(q_ref, keys_ref, idx_ref, gate_ref, v_scr, i_scr, cand_scr, cidx_scr, ts_scr, sel_scr):
    K = PEER_TOPK
    tb = q_ref.shape[0]
    pairs = _pair_list()
    n_cand = cand_scr.shape[0]
    kio = lax.broadcasted_iota(jnp.int32, (N_KEYS, tb), 0).astype(jnp.float32)
    pio = lax.broadcasted_iota(jnp.int32, (n_cand, tb), 0).astype(jnp.float32)

    for h in range(PEER_HEADS):
        for j in range(2):
            c0 = (h * 2 + j) * N_KEYS
            s = _mm_nt(keys_ref[h, j], q_ref[:, c0:c0 + N_KEYS])
            for r in range(K):
                m = jnp.max(s, axis=0, keepdims=True)
                am = jnp.min(jnp.where(s == m, kio, float(N_KEYS)), axis=0, keepdims=True)
                v_scr[j, r:r + 1, :] = m
                i_scr[j, r:r + 1, :] = am
                s = jnp.where(kio == am, _NEG_INF, s)
        v1, v2 = v_scr[0], v_scr[1]
        i1, i2 = i_scr[0], i_scr[1]
        cand_scr[...] = jnp.full(cand_scr.shape, _NEG_INF, jnp.float32)
        cidx_scr[...] = jnp.zeros(cidx_scr.shape, jnp.float32)
        off = 0
        for a in range(K):
            nb = sum(1 for (aa, _) in pairs if aa == a)
            cand_scr[off:off + nb, :] = v1[a:a + 1, :] + v2[0:nb, :]
            cidx_scr[off:off + nb, :] = i1[a:a + 1, :] * float(N_KEYS) + i2[0:nb, :]
            off += nb
        cand = cand_scr[...]
        cidx = cidx_scr[...]
        for r in range(K):
            m = jnp.max(cand, axis=0, keepdims=True)
            pos = jnp.min(jnp.where(cand == m, pio, float(n_cand)), axis=0, keepdims=True)
            hit = pio == pos
            ts_scr[r:r + 1, :] = m
            sel_scr[h * K + r:h * K + r + 1, :] = jnp.sum(jnp.where(hit, cidx, 0.0), axis=0, keepdims=True)
            cand = jnp.where(hit, _NEG_INF, cand)
        ts = ts_scr[...]
        e = jnp.exp(ts - ts[0:1, :])
        gate_ref[h * K:(h + 1) * K, :] = e / jnp.sum(e, axis=0, keepdims=True)
    idx_ref[...] = sel_scr[...].T.astype(jnp.int32)


def _peer_topk(q, sub_keys_bf16, tb):
    T = q.shape[0]
    n_sel = PEER_HEADS * PEER_TOPK
    n_cand = -(-len(_pair_list()) // 8) * 8
    return pl.pallas_call(
        _topk_kernel,
        grid=(T // tb,),
        in_specs=[pl.BlockSpec((tb, q.shape[1]), lambda i: (i, 0)),
                  pl.BlockSpec(sub_keys_bf16.shape, lambda i: (0, 0, 0, 0))],
        out_specs=[pl.BlockSpec((tb, n_sel), lambda i: (i, 0)),
                   pl.BlockSpec((n_sel, tb), lambda i: (0, i))],
        out_shape=[jax.ShapeDtypeStruct((T, n_sel), jnp.int32),
                   jax.ShapeDtypeStruct((n_sel, T), jnp.float32)],
        scratch_shapes=[pltpu.VMEM((2, PEER_TOPK, tb), jnp.float32),
                        pltpu.VMEM((2, PEER_TOPK, tb), jnp.float32),
                        pltpu.VMEM((n_cand, tb), jnp.float32),
                        pltpu.VMEM((n_cand, tb), jnp.float32),
                        pltpu.VMEM((PEER_TOPK, tb), jnp.float32),
                        pltpu.VMEM((n_sel, tb), jnp.float32)],
        compiler_params=pltpu.CompilerParams(
            dimension_semantics=("parallel",), vmem_limit_bytes=VMEM_LIMIT),
        name="peer_topk",
    )(q, sub_keys_bf16)


def _peer_kernel(idx_ref, h_ref, gate_ref, x1_ref, gt_ref, g_ref, uv_ref, o_ref, buf, sem, y_scr, *, n_slots):
    tb, D = h_ref.shape
    n_sel = idx_ref.shape[1]
    n_c = D // 2 // HEAD_W
    n_t = 2 * n_c
    pitch = buf.shape[1] // n_sel
    half = D // 2

    def unpack(words):
        lo = lax.bitcast_convert_type(words << 16, jnp.float32)
        hi = lax.bitcast_convert_type(words & jnp.uint32(0xFFFF0000), jnp.float32)
        return lo, hi

    def row_copy(t, k, slot):
        e = idx_ref[t, k]
        return pltpu.make_async_copy(uv_ref.at[e], buf.at[slot, pl.ds(k * pitch, n_t), :], sem.at[slot])

    def tile(slot, c):
        return buf[slot, pl.ds(c, n_sel, stride=pitch), :]

    def start_rows(t, slot, k0, k1):
        for k in range(k0, k1):
            row_copy(t, k, slot).start(priority=k % 2)

    def wait_slot(slot):
        done = buf.at[slot, pl.ds(0, n_sel * n_t), :]
        pltpu.make_async_copy(done, done, sem.at[slot]).wait()

    n_ahead = n_slots - 1
    for t in range(n_ahead):
        start_rows(t, t, 0, n_sel)

    lane = lax.broadcasted_iota(jnp.int32, gate_ref.shape, 1)
    per_piece = n_sel // (2 * n_c)

    def token(t, slot, prefetch):
        def issue(piece):
            if prefetch:
                start_rows(t + n_ahead, (slot + n_ahead) % n_slots, piece * per_piece, (piece + 1) * per_piece)

        wait_slot(slot)
        h = h_ref[pl.ds(t, 1), :]
        p = None
        for c in range(n_c):
            lo, hi = unpack(tile(slot, c))
            pc = lo * h[:, c * HEAD_W:(c + 1) * HEAD_W] + hi * h[:, half + c * HEAD_W:half + (c + 1) * HEAD_W]
            p = pc if p is None else p + pc
            issue(c)
        a = jnp.sum(p, axis=1, keepdims=True)
        gcol = jnp.sum(jnp.where(lane == t, gate_ref[...], 0.0), axis=1, keepdims=True)
        w = gcol * (0.5 * a * (1.0 + lax.erf(a * (1.0 / math.sqrt(2.0)))))
        y_lo, y_hi = [], []
        for c in range(n_c):
            lo, hi = unpack(tile(slot, n_c + c))
            y_lo.append(jnp.sum(w * lo, axis=0, keepdims=True))
            y_hi.append(jnp.sum(w * hi, axis=0, keepdims=True))
            issue(n_c + c)
        y_scr[pl.ds(t, 1), :] = jnp.concatenate(y_lo + y_hi, axis=1)

    n_main = tb - n_ahead
    n_groups = n_main // n_slots

    def group(g, carry):
        for j in range(n_slots):
            token(g * n_slots + j, j, True)
        return carry

    lax.fori_loop(0, n_groups, group, 0)
    for t in range(n_groups * n_slots, tb):
        token(t, t % n_slots, t < n_main)
    o_ref[...] = x1_ref[...] + gt_ref[0] * _rms(y_scr[...], g_ref[...])


def _pack_experts(u, v):
    E, D = u.shape

    def pack(w):
        bits = lax.bitcast_convert_type(w.astype(jnp.bfloat16), jnp.uint16).astype(jnp.uint32)
        return bits[:, :D // 2] | (bits[:, D // 2:] << 16)

    return jnp.concatenate([pack(u), pack(v)], axis=1).reshape(E, D // HEAD_W, HEAD_W)


def _peer_mix(idx, h2, gate_t, x1, gt2, g_post, uv, seq_len, tb, n_slots):
    T, D = h2.shape
    n_sel = idx.shape[1]
    blocks_per_seq = seq_len // tb
    rows = pl.BlockSpec((tb, D), lambda i: (i, 0))
    return pl.pallas_call(
        functools.partial(_peer_kernel, n_slots=n_slots),
        grid=(T // tb,),
        in_specs=[pl.BlockSpec((tb, n_sel), lambda i: (i, 0), memory_space=pltpu.SMEM),
                  rows,
                  pl.BlockSpec((n_sel, tb), lambda i: (0, i)),
                  rows,
                  pl.BlockSpec((1, 1, D), lambda i: (i // blocks_per_seq, 0, 0)),
                  pl.BlockSpec((1, D), lambda i: (0, 0)),
                  pl.BlockSpec(memory_space=pl.ANY)],
        out_specs=rows,
        out_shape=jax.ShapeDtypeStruct((T, D), jnp.float32),
        scratch_shapes=[pltpu.VMEM((n_slots, n_sel * (D // HEAD_W + 1), HEAD_W), jnp.uint32),
                        pltpu.SemaphoreType.DMA((n_slots,)),
                        pltpu.VMEM((tb, D), jnp.float32)],
        compiler_params=pltpu.CompilerParams(
            dimension_semantics=("arbitrary",), vmem_limit_bytes=VMEM_LIMIT),
        name="peer_gather_mix",
    )(idx, h2, gate_t, x1, gt2, g_post.reshape(1, D), uv)


def _pick(n, pref):
    t = min(n, pref)
    assert n % t == 0, (n, pref)
    return t


def kernel(x, c, w_ada, b_ada, g_pre_mix, g_post_mix, g_pre_ffn, g_post_ffn, w_in, lam_qk, g_diff_sub,
           lb_theta, g_hgrn_out, w_out, w_pq, sub_keys, expert_u, expert_v):
    B, S, D = x.shape
    depth = w_in.shape[0]
    n_attn = 3 * DIFF_HEADS * HEAD_W
    tm = _pick(S, 512)
    tq = _pick(S, 512)
    sb = _pick(S, 512)
    tb = _pick(S, 128)
    bf = jnp.bfloat16
    for l in range(depth):
        lam_init = 0.8 - 0.6 * math.exp(-0.3 * l)
        mod = _modulation(c, w_ada[l], b_ada[l]).reshape(N_MOD, B, 1, D)
        sh1, sc1, gt1, sh2, sc2, gt2 = (mod[i] for i in range(N_MOD))
        attn_in, hg_in = _in_projection(x, g_pre_mix[l], sc1, sh1, w_in[l].astype(bf), n_attn, tm)
        ao = _diff_attention(attn_in, lam_qk[l], g_diff_sub[l], lam_init, tq)
        ho = _hgrn(hg_in, lb_theta, g_hgrn_out[l], l, sb)
        x1, h2, q = _out_projection(ao, ho, x, w_out[l].astype(bf), g_post_mix[l], gt1, g_pre_ffn[l],
                                    sc2, sh2, w_pq[l].astype(bf), tm)
        idx, gate_t = _peer_topk(q.reshape(B * S, -1), sub_keys[l].astype(bf), tb)
        uv = _pack_experts(expert_u[l], expert_v[l])
        out = _peer_mix(idx, h2.reshape(B * S, D), gate_t, x1.reshape(B * S, D), gt2, g_post_ffn[l], uv,
                        S, _pick(S, 256), n_slots=8)
        x = out.reshape(B, S, D)
    return x
```

```python
import functools
import math

import jax
import jax.numpy as jnp
from jax import lax
from jax.experimental import pallas as pl
from jax.experimental.pallas import tpu as pltpu

EPS = 1e-6
N_MOD = 6
DIFF_HEADS = 4
DIFF_QK = 64
HEAD_W = 128
HGRN_HEADS = 4
HGRN_CHUNK = 64
HGRN_SUB = 16
PEER_HEADS = 8
N_KEYS = 128
PEER_TOPK = 16
VMEM_LIMIT = 56 * 1024 * 1024

_HI = lax.Precision.HIGHEST
_NEG_INF = float("-inf")


def _dot(a, b, dims, precision=None):
    return lax.dot_general(a, b, (dims, ((), ())), precision=precision,
                           preferred_element_type=jnp.float32)


def _mm(a, b, precision=None):
    return _dot(a, b, ((1,), (0,)), precision)


def _mm_nt(a, b, precision=None):
    return _dot(a, b, ((1,), (1,)), precision)


def _mm_tn(a, b, precision=None):
    return _dot(a, b, ((0,), (0,)), precision)


def _rms(x, g):
    return x * lax.rsqrt(jnp.mean(x * x, axis=-1, keepdims=True) + EPS) * g


def _sigmoid(x):
    return 1.0 / (1.0 + jnp.exp(-x))


def _mod_kernel(c_ref, w_ref, b_ref, o_ref):
    c = c_ref[...]
    ca = c * _sigmoid(c)
    o_ref[0] = _mm(ca, w_ref[...], _HI) + b_ref[...]


def _modulation(c, w_ada, b_ada):
    B, D = c.shape
    return pl.pallas_call(
        _mod_kernel,
        grid=(N_MOD,),
        in_specs=[pl.BlockSpec((B, D), lambda j: (0, 0)),
                  pl.BlockSpec((D, D), lambda j: (0, j)),
                  pl.BlockSpec((1, D), lambda j: (0, j))],
        out_specs=pl.BlockSpec((1, B, D), lambda j: (j, 0, 0)),
        out_shape=jax.ShapeDtypeStruct((N_MOD, B, D), jnp.float32),
        compiler_params=pltpu.CompilerParams(vmem_limit_bytes=VMEM_LIMIT),
        name="adaln_mod",
    )(c, w_ada, b_ada.reshape(1, N_MOD * D))


def _inproj_kernel(x_ref, g_ref, sc_ref, sh_ref, w_ref, attn_ref, hg_ref, *, n_attn, col_chunk):
    x = x_ref[0]
    h = _rms(x, g_ref[...]) * (1.0 + sc_ref[0]) + sh_ref[0]
    hb = h.astype(jnp.bfloat16)
    n_cols = w_ref.shape[1]
    for c0 in range(0, n_cols, col_chunk):
        r = _mm(hb, w_ref[:, c0:c0 + col_chunk])
        if c0 < n_attn:
            attn_ref[0, :, c0:c0 + col_chunk] = r.astype(attn_ref.dtype)
        else:
            hg_ref[0, :, c0 - n_attn:c0 - n_attn + col_chunk] = r


def _in_projection(x, g, sc, sh, w_in_bf16, n_attn, tm):
    B, S, D = x.shape
    n_cols = w_in_bf16.shape[1]
    vec = pl.BlockSpec((1, 1, D), lambda b, i: (b, 0, 0))
    return pl.pallas_call(
        functools.partial(_inproj_kernel, n_attn=n_attn, col_chunk=512),
        grid=(B, S // tm),
        in_specs=[pl.BlockSpec((1, tm, D), lambda b, i: (b, i, 0)),
                  pl.BlockSpec((1, D), lambda b, i: (0, 0)),
                  vec, vec,
                  pl.BlockSpec((D, n_cols), lambda b, i: (0, 0))],
        out_specs=[pl.BlockSpec((1, tm, n_attn), lambda b, i: (b, i, 0)),
                   pl.BlockSpec((1, tm, n_cols - n_attn), lambda b, i: (b, i, 0))],
        out_shape=[jax.ShapeDtypeStruct((B, S, n_attn), jnp.bfloat16),
                   jax.ShapeDtypeStruct((B, S, n_cols - n_attn), jnp.float32)],
        compiler_params=pltpu.CompilerParams(
            dimension_semantics=("parallel", "parallel"), vmem_limit_bytes=VMEM_LIMIT),
        name="prenorm_inproj",
    )(x, g.reshape(1, D), sc, sh, w_in_bf16)


def _fold_lanes(x, op):
    out = x[:, 0:HEAD_W]
    for i in range(1, x.shape[1] // HEAD_W):
        out = op(out, x[:, i * HEAD_W:(i + 1) * HEAD_W])
    return out


def _attn_kernel(q_ref, k_ref, v_ref, lam_ref, g_ref, o_ref, s_scr, mx_scr, l_scr, acc_scr, *, tq, lam_init):
    qi = pl.program_id(2)
    q = q_ref[0]
    lane = lax.broadcasted_iota(jnp.int32, q.shape, 1)
    qs = q * jnp.asarray(1.0 / math.sqrt(DIFF_QK), q.dtype)
    zero = jnp.zeros_like(qs)
    q2 = jnp.concatenate([jnp.where(lane < DIFF_QK, qs, zero), jnp.where(lane >= DIFF_QK, qs, zero)], axis=0)
    n_rep = tq // HEAD_W

    mx_scr[...] = jnp.full(mx_scr.shape, _NEG_INF, jnp.float32)

    def scores(j, carry):
        s = _mm_nt(q2, k_ref[0, pl.ds(pl.multiple_of(j * tq, tq), tq), :])
        s_scr[j] = s
        mx_scr[...] = jnp.maximum(mx_scr[...], _fold_lanes(s, jnp.maximum))
        return carry

    lax.fori_loop(0, qi, scores, 0)
    s = _mm_nt(q2, k_ref[0, pl.ds(pl.multiple_of(qi * tq, tq), tq), :])
    row = lax.broadcasted_iota(jnp.int32, s.shape, 0) & (tq - 1)
    col = lax.broadcasted_iota(jnp.int32, s.shape, 1)
    s = jnp.where(col <= row, s, _NEG_INF)
    s_scr[qi] = s
    m = jnp.max(jnp.maximum(mx_scr[...], _fold_lanes(s, jnp.maximum)), axis=1, keepdims=True)
    mx_scr[...] = jnp.broadcast_to(m, mx_scr.shape)
    l_scr[...] = jnp.zeros(l_scr.shape, jnp.float32)
    acc_scr[...] = jnp.zeros(acc_scr.shape, jnp.float32)

    def contract(j, carry):
        mb = mx_scr[...]
        p = jnp.exp(s_scr[j] - jnp.concatenate([mb] * n_rep, axis=1))
        l_scr[...] += _fold_lanes(p, jnp.add)
        vb = v_ref[0, pl.ds(pl.multiple_of(j * tq, tq), tq), :]
        acc_scr[...] += _mm(p.astype(vb.dtype), vb)
        return carry

    lax.fori_loop(0, qi + 1, contract, 0)

    lq = lam_ref[...]
    lam = (jnp.exp(jnp.sum(lq[0:1] * lq[1:2], axis=1, keepdims=True))
           - jnp.exp(jnp.sum(lq[2:3] * lq[3:4], axis=1, keepdims=True)) + lam_init)
    o = acc_scr[...] / jnp.sum(l_scr[...], axis=1, keepdims=True)
    o = o[0:tq] - lam * o[tq:2 * tq]
    o = _rms(o, g_ref[...]) * (1.0 - lam_init)
    o_ref[0] = o.astype(o_ref.dtype)


def _diff_attention(attn_in, lam_qk, g_diff_sub, lam_init, tq):
    B, S, _ = attn_in.shape
    H = DIFF_HEADS
    assert tq % HEAD_W == 0 and tq & (tq - 1) == 0, tq
    kv_spec = lambda off: pl.BlockSpec((1, S, HEAD_W), lambda b, h, i: (b, 0, off + h))
    return pl.pallas_call(
        functools.partial(_attn_kernel, tq=tq, lam_init=lam_init),
        grid=(B, H, S // tq),
        in_specs=[pl.BlockSpec((1, tq, HEAD_W), lambda b, h, i: (b, i, h)),
                  kv_spec(H), kv_spec(2 * H),
                  pl.BlockSpec(lam_qk.shape, lambda b, h, i: (0, 0)),
                  pl.BlockSpec((1, HEAD_W), lambda b, h, i: (0, 0))],
        out_specs=pl.BlockSpec((1, tq, HEAD_W), lambda b, h, i: (b, i, h)),
        out_shape=jax.ShapeDtypeStruct((B, S, H * HEAD_W), jnp.bfloat16),
        scratch_shapes=[pltpu.VMEM((S // tq, 2 * tq, tq), jnp.float32),
                        pltpu.VMEM((2 * tq, HEAD_W), jnp.float32),
                        pltpu.VMEM((2 * tq, HEAD_W), jnp.float32),
                        pltpu.VMEM((2 * tq, HEAD_W), jnp.float32)],
        compiler_params=pltpu.CompilerParams(
            dimension_semantics=("parallel", "parallel", "arbitrary"), vmem_limit_bytes=VMEM_LIMIT),
        name="diff_attention",
    )(attn_in, attn_in, attn_in, lam_qk, g_diff_sub.reshape(1, HEAD_W))


def _hgrn_kernel(hq_ref, hf_ref, hi_ref, hgate_ref, lbt_ref, g_ref, o_ref, state_ref, *, layer, n_chunks):
    C, SUB = HGRN_CHUNK, HGRN_SUB
    n_sub = C // SUB
    n_heads = state_ref.shape[0]

    @pl.when(pl.program_id(1) == 0)
    def _():
        state_ref[...] = jnp.zeros(state_ref.shape, jnp.float32)

    th = lbt_ref[...]
    e = jnp.exp(th - jnp.max(th, axis=0, keepdims=True))
    lb_all = jnp.sum(e[0:layer + 1], axis=0, keepdims=True) / jnp.sum(e, axis=0, keepdims=True)

    r_io = lax.broadcasted_iota(jnp.int32, (C, C), 0)
    c_io = lax.broadcasted_iota(jnp.int32, (C, C), 1)
    tril = (c_io <= r_io).astype(jnp.float32)
    tloc = lax.broadcasted_iota(jnp.int32, (C, HEAD_W), 0) % SUB
    sub_col = lax.broadcasted_iota(jnp.int32, (SUB, C), 1)
    g_out = g_ref[...]

    def group_rows(x, s):
        return jnp.concatenate(
            [jnp.broadcast_to(x[i * SUB + s:i * SUB + s + 1, :], (SUB, HEAD_W)) for i in range(n_sub)], axis=0)

    def head_chunk(rows, hd):
        cols = slice(hd * HEAD_W, (hd + 1) * HEAD_W)
        lb = lb_all[:, cols]
        hq = hq_ref[0, rows, cols]
        f = lb + (1.0 - lb) * _sigmoid(hf_ref[0, rows, cols])
        glog = jnp.log(f)
        kk = 1.0 - f
        q = hq * _sigmoid(hq)
        v = hi_ref[0, rows, cols]
        b = _mm(tril, glog, _HI)
        st = state_ref[hd]

        bf = jnp.bfloat16
        vb = v.astype(bf)
        o = _mm_nt((q * jnp.exp(b)).astype(bf), st.astype(bf))

        o_sub = [jnp.zeros((SUB, HEAD_W), jnp.float32)]
        for i in range(1, n_sub):
            beta = b[i * SUB:i * SUB + 1, :]
            qt = q[i * SUB:(i + 1) * SUB, :] * jnp.exp(b[i * SUB:(i + 1) * SUB, :] - beta)
            kt = kk * jnp.exp(jnp.minimum(beta - b, 0.0))
            p = _mm_nt(qt.astype(bf), kt.astype(bf))
            p = jnp.where(sub_col < i * SUB, p, 0.0)
            o_sub.append(_mm(p.astype(bf), vb))
        o = o + jnp.concatenate(o_sub, axis=0)

        for s in range(SUB):
            b_s, k_s, v_s = group_rows(b, s), group_rows(kk, s), group_rows(v, s)
            w = q * k_s * jnp.exp(jnp.where(tloc >= s, b - b_s, _NEG_INF))
            o = o + jnp.sum(w, axis=1, keepdims=True) * v_s

        b_last = b[C - 1:C, :]
        kdec = kk * jnp.exp(b_last - b)
        state_ref[hd] = st * jnp.exp(b_last) + _mm_tn(vb, kdec.astype(bf))

        hgate = hgate_ref[0, rows, cols]
        y = _rms(o, g_out) * (hgate * _sigmoid(hgate))
        o_ref[0, rows, cols] = y.astype(o_ref.dtype)

    per_trip = 2 if n_chunks % 2 == 0 else 1

    def chunk(ci, carry):
        for sub in range(per_trip):
            rows = pl.ds(pl.multiple_of((ci * per_trip + sub) * C, C), C)
            for hd in range(n_heads):
                head_chunk(rows, hd)
        return carry

    lax.fori_loop(0, n_chunks // per_trip, chunk, 0)


def _hgrn(hg_in, lb_theta, g_hgrn_out, layer, sb):
    B, S, _ = hg_in.shape
    H = HGRN_HEADS
    W = H * HEAD_W
    spec = lambda off: pl.BlockSpec((1, sb, W), lambda b, i: (b, i, off))
    n_slots = lb_theta.shape[0]
    return pl.pallas_call(
        functools.partial(_hgrn_kernel, layer=layer, n_chunks=sb // HGRN_CHUNK),
        grid=(B, S // sb),
        in_specs=[spec(0), spec(1), spec(2), spec(3),
                  pl.BlockSpec((n_slots, W), lambda b, i: (0, 0)),
                  pl.BlockSpec((1, HEAD_W), lambda b, i: (0, 0))],
        out_specs=pl.BlockSpec((1, sb, W), lambda b, i: (b, i, 0)),
        out_shape=jax.ShapeDtypeStruct((B, S, W), jnp.bfloat16),
        scratch_shapes=[pltpu.VMEM((H, HEAD_W, HEAD_W), jnp.float32)],
        compiler_params=pltpu.CompilerParams(
            dimension_semantics=("parallel", "arbitrary"), vmem_limit_bytes=VMEM_LIMIT),
        name="hgrn2",
    )(hg_in, hg_in, hg_in, hg_in, lb_theta, g_hgrn_out.reshape(1, HEAD_W))


def _outproj_kernel(ao_ref, ho_ref, x_ref, wo_ref, gpost_ref, gt_ref, gpre_ref, sc_ref, sh_ref, wq_ref,
                    x1_ref, h2_ref, q_ref):
    n_a = ao_ref.shape[2]
    y = _mm(ao_ref[0], wo_ref[0:n_a, :]) + _mm(ho_ref[0], wo_ref[n_a:, :])
    x1 = x_ref[0] + gt_ref[0] * _rms(y, gpost_ref[...])
    x1_ref[0] = x1
    h2 = _rms(x1, gpre_ref[...]) * (1.0 + sc_ref[0]) + sh_ref[0]
    h2_ref[0] = h2
    q_ref[0] = _mm(h2.astype(jnp.bfloat16), wq_ref[...]).astype(q_ref.dtype)


def _out_projection(ao, ho, x, w_out_bf16, g_post, gt1, g_pre, sc2, sh2, w_pq_bf16, tm):
    B, S, D = x.shape
    n_a, n_h, n_q = ao.shape[2], ho.shape[2], w_pq_bf16.shape[1]
    vec = pl.BlockSpec((1, 1, D), lambda b, i: (b, 0, 0))
    par = pl.BlockSpec((1, D), lambda b, i: (0, 0))
    row = lambda n: pl.BlockSpec((1, tm, n), lambda b, i: (b, i, 0))
    return pl.pallas_call(
        _outproj_kernel,
        grid=(B, S // tm),
        in_specs=[row(n_a), row(n_h), row(D),
                  pl.BlockSpec((n_a + n_h, D), lambda b, i: (0, 0)),
                  par, vec, par, vec, vec,
                  pl.BlockSpec((D, n_q), lambda b, i: (0, 0))],
        out_specs=[row(D), row(D), row(n_q)],
        out_shape=[jax.ShapeDtypeStruct((B, S, D), jnp.float32),
                   jax.ShapeDtypeStruct((B, S, D), jnp.float32),
                   jax.ShapeDtypeStruct((B, S, n_q), jnp.bfloat16)],
        compiler_params=pltpu.CompilerParams(
            dimension_semantics=("parallel", "parallel"), vmem_limit_bytes=VMEM_LIMIT),
        name="outproj_norms_peerq",
    )(ao, ho, x, w_out_bf16, g_post.reshape(1, D), gt1, g_pre.reshape(1, D), sc2, sh2, w_pq_bf16)


def _pair_list():
    return [(a, b) for a in range(PEER_TOPK) for b in range(PEER_TOPK) if (a + 1) * (b + 1) <= PEER_TOPK]


def _topk_kernel(q_ref, keys_ref, idx_ref, gate_ref, v_scr, i_scr, cand_scr, cidx_scr, ts_scr, sel_scr):
    K = PEER_TOPK
    tb = q_ref.shape[0]
    pairs = _pair_list()
    n_cand = cand_scr.shape[0]
    kio = lax.broadcasted_iota(jnp.int32, (N_KEYS, tb), 0).astype(jnp.float32)
    pio = lax.broadcasted_iota(jnp.int32, (n_cand, tb), 0).astype(jnp.float32)

    for h in range(PEER_HEADS):
        for j in range(2):
            c0 = (h * 2 + j) * N_KEYS
            s = _mm_nt(keys_ref[h, j], q_ref[:, c0:c0 + N_KEYS])
            for r in range(K):
                m = jnp.max(s, axis=0, keepdims=True)
                am = jnp.min(jnp.where(s == m, kio, float(N_KEYS)), axis=0, keepdims=True)
                v_scr[j, r:r + 1, :] = m
                i_scr[j, r:r + 1, :] = am
                s = jnp.where(kio == am, _NEG_INF, s)
        v1, v2 = v_scr[0], v_scr[1]
        i1, i2 = i_scr[0], i_scr[1]
        cand_scr[...] = jnp.full(cand_scr.shape, _NEG_INF, jnp.float32)
        cidx_scr[...] = jnp.zeros(cidx_scr.shape, jnp.float32)
        off = 0
        for a in range(K):
            nb = sum(1 for (aa, _) in pairs if aa == a)
            cand_scr[off:off + nb, :] = v1[a:a + 1, :] + v2[0:nb, :]
            cidx_scr[off:off + nb, :] = i1[a:a + 1, :] * float(N_KEYS) + i2[0:nb, :]
            off += nb
        cand = cand_scr[...]
        cidx = cidx_scr[...]
        for r in range(K):
            m = jnp.max(cand, axis=0, keepdims=True)
            pos = jnp.min(jnp.where(cand == m, pio, float(n_cand)), axis=0, keepdims=True)
            hit = pio == pos
            ts_scr[r:r + 1, :] = m
            sel_scr[h * K + r:h * K + r + 1, :] = jnp.sum(jnp.where(hit, cidx, 0.0), axis=0, keepdims=True)
            cand = jnp.where(hit, _NEG_INF, cand)
        ts = ts_scr[...]
        e = jnp.exp(ts - ts[0:1, :])
        gate_ref[h * K:(h + 1) * K, :] = e / jnp.sum(e, axis=0, keepdims=True)
    idx_ref[...] = sel_scr[...].T.astype(jnp.int32)


def _peer_topk(q, sub_keys_bf16, tb):
    T = q.shape[0]
    n_sel = PEER_HEADS * PEER_TOPK
    n_cand = -(-len(_pair_list()) // 8) * 8
    return pl.pallas_call(
        _topk_kernel,
        grid=(T // tb,),
        in_specs=[pl.BlockSpec((tb, q.shape[1]), lambda i: (i, 0)),
                  pl.BlockSpec(sub_keys_bf16.shape, lambda i: (0, 0, 0, 0))],
        out_specs=[pl.BlockSpec((tb, n_sel), lambda i: (i, 0)),
                   pl.BlockSpec((n_sel, tb), lambda i: (0, i))],
        out_shape=[jax.ShapeDtypeStruct((T, n_sel), jnp.int32),
                   jax.ShapeDtypeStruct((n_sel, T), jnp.float32)],
        scratch_shapes=[pltpu.VMEM((2, PEER_TOPK, tb), jnp.float32),
                        pltpu.VMEM((2, PEER_TOPK, tb), jnp.float32),
                        pltpu.VMEM((n_cand, tb), jnp.float32),
                        pltpu.VMEM((n_cand, tb), jnp.float32),
                        pltpu.VMEM((PEER_TOPK, tb), jnp.float32),
                        pltpu.VMEM((n_sel, tb), jnp.float32)],
        compiler_params=pltpu.CompilerParams(
            dimension_semantics=("parallel",), vmem_limit_bytes=VMEM_LIMIT),
        name="peer_topk",
    )(q, sub_keys_bf16)


def _peer_kernel(idx_ref, h_ref, gate_ref, x1_ref, gt_ref, g_ref, uv_ref, o_ref, buf, sem, y_scr, *, n_slots):
    tb, D = h_ref.shape
    n_sel = idx_ref.shape[1]
    n_c = D // 2 // HEAD_W
    n_t = 2 * n_c
    pitch = buf.shape[1] // n_sel
    half = D // 2

    def unpack(words):
        lo = lax.bitcast_convert_type(words << 16, jnp.float32)
        hi = lax.bitcast_convert_type(words & jnp.uint32(0xFFFF0000), jnp.float32)
        return lo, hi

    def row_copy(t, k, slot):
        e = idx_ref[t, k]
        return pltpu.make_async_copy(uv_ref.at[e], buf.at[slot, pl.ds(k * pitch, n_t), :], sem.at[slot])

    def tile(slot, c):
        return buf[slot, pl.ds(c, n_sel, stride=pitch), :]

    def start_rows(t, slot, k0, k1):
        for k in range(k0, k1):
            row_copy(t, k, slot).start(priority=k % 2)

    def wait_slot(slot):
        done = buf.at[slot, pl.ds(0, n_sel * n_t), :]
        pltpu.make_async_copy(done, done, sem.at[slot]).wait()

    n_ahead = n_slots - 1
    for t in range(n_ahead):
        start_rows(t, t, 0, n_sel)

    lane = lax.broadcasted_iota(jnp.int32, gate_ref.shape, 1)
    per_piece = n_sel // (2 * n_c)

    def token(t, slot, prefetch):
        def issue(piece):
            if prefetch:
                start_rows(t + n_ahead, (slot + n_ahead) % n_slots, piece * per_piece, (piece + 1) * per_piece)

        wait_slot(slot)
        h = h_ref[pl.ds(t, 1), :]
        p = None
        for c in range(n_c):
            lo, hi = unpack(tile(slot, c))
            pc = lo * h[:, c * HEAD_W:(c + 1) * HEAD_W] + hi * h[:, half + c * HEAD_W:half + (c + 1) * HEAD_W]
            p = pc if p is None else p + pc
            issue(c)
        a = jnp.sum(p, axis=1, keepdims=True)
        gcol = jnp.sum(jnp.where(lane == t, gate_ref[...], 0.0), axis=1, keepdims=True)
        w = gcol * (0.5 * a * (1.0 + lax.erf(a * (1.0 / math.sqrt(2.0)))))
        y_lo, y_hi = [], []
        for c in range(n_c):
            lo, hi = unpack(tile(slot, n_c + c))
            y_lo.append(jnp.sum(w * lo, axis=0, keepdims=True))
            y_hi.append(jnp.sum(w * hi, axis=0, keepdims=True))
            issue(n_c + c)
        y_scr[pl.ds(t, 1), :] = jnp.concatenate(y_lo + y_hi, axis=1)

    n_main = tb - n_ahead
    n_groups = n_main // n_slots

    def group(g, carry):
        for j in range(n_slots):
            token(g * n_slots + j, j, True)
        return carry

    lax.fori_loop(0, n_groups, group, 0)
    for t in range(n_groups * n_slots, tb):
        token(t, t % n_slots, t < n_main)
    o_ref[...] = x1_ref[...] + gt_ref[0] * _rms(y_scr[...], g_ref[...])


def _pack_experts(u, v):
    E, D = u.shape

    def pack(w):
        bits = lax.bitcast_convert_type(w.astype(jnp.bfloat16), jnp.uint16).astype(jnp.uint32)
        return bits[:, :D // 2] | (bits[:, D // 2:] << 16)

    return jnp.concatenate([pack(u), pack(v)], axis=1).reshape(E, D // HEAD_W, HEAD_W)


def _peer_mix(idx, h2, gate_t, x1, gt2, g_post, uv, seq_len, tb, n_slots):
    T, D = h2.shape
    n_sel = idx.shape[1]
    blocks_per_seq = seq_len // tb
    rows = pl.BlockSpec((tb, D), lambda i: (i, 0))
    return pl.pallas_call(
        functools.partial(_peer_kernel, n_slots=n_slots),
        grid=(T // tb,),
        in_specs=[pl.BlockSpec((tb, n_sel), lambda i: (i, 0), memory_space=pltpu.SMEM),
                  rows,
                  pl.BlockSpec((n_sel, tb), lambda i: (0, i)),
                  rows,
                  pl.BlockSpec((1, 1, D), lambda i: (i // blocks_per_seq, 0, 0)),
                  pl.BlockSpec((1, D), lambda i: (0, 0)),
                  pl.BlockSpec(memory_space=pl.ANY)],
        out_specs=rows,
        out_shape=jax.ShapeDtypeStruct((T, D), jnp.float32),
        scratch_shapes=[pltpu.VMEM((n_slots, n_sel * (D // HEAD_W + 1), HEAD_W), jnp.uint32),
                        pltpu.SemaphoreType.DMA((n_slots,)),
                        pltpu.VMEM((tb, D), jnp.float32)],
        compiler_params=pltpu.CompilerParams(
            dimension_semantics=("arbitrary",), vmem_limit_bytes=VMEM_LIMIT),
        name="peer_gather_mix",
    )(idx, h2, gate_t, x1, gt2, g_post.reshape(1, D), uv)


def _pick(n, pref):
    t = min(n, pref)
    assert n % t == 0, (n, pref)
    return t


def kernel(x, c, w_ada, b_ada, g_pre_mix, g_post_mix, g_pre_ffn, g_post_ffn, w_in, lam_qk, g_diff_sub,
           lb_theta, g_hgrn_out, w_out, w_pq, sub_keys, expert_u, expert_v):
    B, S, D = x.shape
    depth = w_in.shape[0]
    n_attn = 3 * DIFF_HEADS * HEAD_W
    tm = _pick(S, 512)
    tq = _pick(S, 512)
    sb = _pick(S, 512)
    tb = _pick(S, 128)
    bf = jnp.bfloat16
    for l in range(depth):
        lam_init = 0.8 - 0.6 * math.exp(-0.3 * l)
        mod = _modulation(c, w_ada[l], b_ada[l]).reshape(N_MOD, B, 1, D)
        sh1, sc1, gt1, sh2, sc2, gt2 = (mod[i] for i in range(N_MOD))
        attn_in, hg_in = _in_projection(x, g_pre_mix[l], sc1, sh1, w_in[l].astype(bf), n_attn, tm)
        ao = _diff_attention(attn_in, lam_qk[l], g_diff_sub[l], lam_init, tq)
        ho = _hgrn(hg_in, lb_theta, g_hgrn_out[l], l, sb)
        x1, h2, q = _out_projection(ao, ho, x, w_out[l].astype(bf), g_post_mix[l], gt1, g_pre_ffn[l],
                                    sc2, sh2, w_pq[l].astype(bf), tm)
        idx, gate_t = _peer_topk(q.reshape(B * S, -1), sub_keys[l].astype(bf), tb)
        uv = _pack_experts(expert_u[l], expert_v[l])
        out = _peer_mix(idx, h2.reshape(B * S, D), gate_t, x1.reshape(B * S, D), gt2, g_post_ffn[l], uv,
                        S, _pick(S, 256), n_slots=6)
        x = out.reshape(B, S, D)
    return x
```

```python
import functools
import math

import jax
import jax.numpy as jnp
from jax import lax
from jax.experimental import pallas as pl
from jax.experimental.pallas import tpu as pltpu

EPS = 1e-6
N_MOD = 6
DIFF_HEADS = 4
DIFF_QK = 64
HEAD_W = 128
HGRN_HEADS = 4
HGRN_CHUNK = 64
HGRN_SUB = 16
PEER_HEADS = 8
N_KEYS = 128
PEER_TOPK = 16
VMEM_LIMIT = 56 * 1024 * 1024

_HI = lax.Precision.HIGHEST
_NEG_INF = float("-inf")
LOG2E = 1.4426950408889634


def _dot(a, b, dims, precision=None):
    return lax.dot_general(a, b, (dims, ((), ())), precision=precision,
                           preferred_element_type=jnp.float32)


def _mm(a, b, precision=None):
    return _dot(a, b, ((1,), (0,)), precision)


def _mm_nt(a, b, precision=None):
    return _dot(a, b, ((1,), (1,)), precision)


def _mm_tn(a, b, precision=None):
    return _dot(a, b, ((0,), (0,)), precision)


def _rms(x, g):
    return x * lax.rsqrt(jnp.mean(x * x, axis=-1, keepdims=True) + EPS) * g


def _sigmoid(x):
    return 1.0 / (1.0 + jnp.exp(-x))


def _mod_kernel(c_ref, w_ref, b_ref, o_ref):
    c = c_ref[...]
    ca = c * _sigmoid(c)
    o_ref[0] = _mm(ca, w_ref[...], _HI) + b_ref[...]


def _modulation(c, w_ada, b_ada):
    B, D = c.shape
    return pl.pallas_call(
        _mod_kernel,
        grid=(N_MOD,),
        in_specs=[pl.BlockSpec((B, D), lambda j: (0, 0)),
                  pl.BlockSpec((D, D), lambda j: (0, j)),
                  pl.BlockSpec((1, D), lambda j: (0, j))],
        out_specs=pl.BlockSpec((1, B, D), lambda j: (j, 0, 0)),
        out_shape=jax.ShapeDtypeStruct((N_MOD, B, D), jnp.float32),
        compiler_params=pltpu.CompilerParams(vmem_limit_bytes=VMEM_LIMIT),
        name="adaln_mod",
    )(c, w_ada, b_ada.reshape(1, N_MOD * D))


def _inproj_kernel(x_ref, g_ref, sc_ref, sh_ref, w_ref, attn_ref, hg_ref, *, n_attn, col_chunk):
    x = x_ref[0]
    h = _rms(x, g_ref[...]) * (1.0 + sc_ref[0]) + sh_ref[0]
    hb = h.astype(jnp.bfloat16)
    n_cols = w_ref.shape[1]
    for c0 in range(0, n_cols, col_chunk):
        r = _mm(hb, w_ref[:, c0:c0 + col_chunk])
        if c0 < n_attn:
            attn_ref[0, :, c0:c0 + col_chunk] = r.astype(attn_ref.dtype)
        else:
            hg_ref[0, :, c0 - n_attn:c0 - n_attn + col_chunk] = r


def _in_projection(x, g, sc, sh, w_in_bf16, n_attn, tm):
    B, S, D = x.shape
    n_cols = w_in_bf16.shape[1]
    vec = pl.BlockSpec((1, 1, D), lambda b, i: (b, 0, 0))
    return pl.pallas_call(
        functools.partial(_inproj_kernel, n_attn=n_attn, col_chunk=512),
        grid=(B, S // tm),
        in_specs=[pl.BlockSpec((1, tm, D), lambda b, i: (b, i, 0)),
                  pl.BlockSpec((1, D), lambda b, i: (0, 0)),
                  vec, vec,
                  pl.BlockSpec((D, n_cols), lambda b, i: (0, 0))],
        out_specs=[pl.BlockSpec((1, tm, n_attn), lambda b, i: (b, i, 0)),
                   pl.BlockSpec((1, tm, n_cols - n_attn), lambda b, i: (b, i, 0))],
        out_shape=[jax.ShapeDtypeStruct((B, S, n_attn), jnp.bfloat16),
                   jax.ShapeDtypeStruct((B, S, n_cols - n_attn), jnp.float32)],
        compiler_params=pltpu.CompilerParams(
            dimension_semantics=("parallel", "parallel"), vmem_limit_bytes=VMEM_LIMIT),
        name="prenorm_inproj",
    )(x, g.reshape(1, D), sc, sh, w_in_bf16)


def _fold_lanes(x, op):
    out = x[:, 0:HEAD_W]
    for i in range(1, x.shape[1] // HEAD_W):
        out = op(out, x[:, i * HEAD_W:(i + 1) * HEAD_W])
    return out


def _attn_kernel(q_ref, k_ref, v_ref, lam_ref, g_ref, o_ref, s_scr, mx_scr, l_scr, acc_scr, *, tq, lam_init):
    qi = pl.program_id(2)
    q = q_ref[0]
    lane = lax.broadcasted_iota(jnp.int32, q.shape, 1)
    qs = q * jnp.asarray(1.0 / math.sqrt(DIFF_QK), q.dtype)
    zero = jnp.zeros_like(qs)
    q2 = jnp.concatenate([jnp.where(lane < DIFF_QK, qs, zero), jnp.where(lane >= DIFF_QK, qs, zero)], axis=0)
    n_rep = tq // HEAD_W

    mx_scr[...] = jnp.full(mx_scr.shape, _NEG_INF, jnp.float32)

    def scores(j, carry):
        s = _mm_nt(q2, k_ref[0, pl.ds(pl.multiple_of(j * tq, tq), tq), :]) * LOG2E
        s_scr[j] = s
        mx_scr[...] = jnp.maximum(mx_scr[...], _fold_lanes(s, jnp.maximum))
        return carry

    lax.fori_loop(0, qi, scores, 0)
    s = _mm_nt(q2, k_ref[0, pl.ds(pl.multiple_of(qi * tq, tq), tq), :]) * LOG2E
    row = lax.broadcasted_iota(jnp.int32, s.shape, 0) & (tq - 1)
    col = lax.broadcasted_iota(jnp.int32, s.shape, 1)
    s = jnp.where(col <= row, s, _NEG_INF)
    s_scr[qi] = s
    m = jnp.max(jnp.maximum(mx_scr[...], _fold_lanes(s, jnp.maximum)), axis=1, keepdims=True)
    mx_scr[...] = jnp.broadcast_to(m, mx_scr.shape)
    l_scr[...] = jnp.zeros(l_scr.shape, jnp.float32)
    acc_scr[...] = jnp.zeros(acc_scr.shape, jnp.float32)

    def contract(j, carry):
        mb = mx_scr[...]
        p = jnp.exp2(s_scr[j] - jnp.concatenate([mb] * n_rep, axis=1))
        l_scr[...] += _fold_lanes(p, jnp.add)
        vb = v_ref[0, pl.ds(pl.multiple_of(j * tq, tq), tq), :]
        acc_scr[...] += _mm(p.astype(vb.dtype), vb)
        return carry

    lax.fori_loop(0, qi + 1, contract, 0)

    lq = lam_ref[...]
    lam = (jnp.exp(jnp.sum(lq[0:1] * lq[1:2], axis=1, keepdims=True))
           - jnp.exp(jnp.sum(lq[2:3] * lq[3:4], axis=1, keepdims=True)) + lam_init)
    o = acc_scr[...] / jnp.sum(l_scr[...], axis=1, keepdims=True)
    o = o[0:tq] - lam * o[tq:2 * tq]
    o = _rms(o, g_ref[...]) * (1.0 - lam_init)
    o_ref[0] = o.astype(o_ref.dtype)


def _diff_attention(attn_in, lam_qk, g_diff_sub, lam_init, tq):
    B, S, _ = attn_in.shape
    H = DIFF_HEADS
    assert tq % HEAD_W == 0 and tq & (tq - 1) == 0, tq
    kv_spec = lambda off: pl.BlockSpec((1, S, HEAD_W), lambda b, h, i: (b, 0, off + h))
    return pl.pallas_call(
        functools.partial(_attn_kernel, tq=tq, lam_init=lam_init),
        grid=(B, H, S // tq),
        in_specs=[pl.BlockSpec((1, tq, HEAD_W), lambda b, h, i: (b, i, h)),
                  kv_spec(H), kv_spec(2 * H),
                  pl.BlockSpec(lam_qk.shape, lambda b, h, i: (0, 0)),
                  pl.BlockSpec((1, HEAD_W), lambda b, h, i: (0, 0))],
        out_specs=pl.BlockSpec((1, tq, HEAD_W), lambda b, h, i: (b, i, h)),
        out_shape=jax.ShapeDtypeStruct((B, S, H * HEAD_W), jnp.bfloat16),
        scratch_shapes=[pltpu.VMEM((S // tq, 2 * tq, tq), jnp.float32),
                        pltpu.VMEM((2 * tq, HEAD_W), jnp.float32),
                        pltpu.VMEM((2 * tq, HEAD_W), jnp.float32),
                        pltpu.VMEM((2 * tq, HEAD_W), jnp.float32)],
        compiler_params=pltpu.CompilerParams(
            dimension_semantics=("parallel", "parallel", "arbitrary"), vmem_limit_bytes=VMEM_LIMIT),
        name="diff_attention",
    )(attn_in, attn_in, attn_in, lam_qk, g_diff_sub.reshape(1, HEAD_W))


def _hgrn_kernel(hq_ref, hf_ref, hi_ref, hgate_ref, lbt_ref, g_ref, o_ref, state_ref, *, layer, n_chunks):
    C, SUB = HGRN_CHUNK, HGRN_SUB
    n_sub = C // SUB
    n_heads = state_ref.shape[0]

    @pl.when(pl.program_id(1) == 0)
    def _():
        state_ref[...] = jnp.zeros(state_ref.shape, jnp.float32)

    th = lbt_ref[...]
    e = jnp.exp(th - jnp.max(th, axis=0, keepdims=True))
    lb_all = jnp.sum(e[0:layer + 1], axis=0, keepdims=True) / jnp.sum(e, axis=0, keepdims=True)

    r_io = lax.broadcasted_iota(jnp.int32, (C, C), 0)
    c_io = lax.broadcasted_iota(jnp.int32, (C, C), 1)
    tril = (c_io <= r_io).astype(jnp.float32)
    tloc = lax.broadcasted_iota(jnp.int32, (C, HEAD_W), 0) % SUB
    sub_col = lax.broadcasted_iota(jnp.int32, (SUB, C), 1)
    g_out = g_ref[...]

    def group_rows(x, s):
        return jnp.concatenate(
            [jnp.broadcast_to(x[i * SUB + s:i * SUB + s + 1, :], (SUB, HEAD_W)) for i in range(n_sub)], axis=0)

    def head_chunk(rows, hd):
        cols = slice(hd * HEAD_W, (hd + 1) * HEAD_W)
        lb = lb_all[:, cols]
        hq = hq_ref[0, rows, cols]
        f = lb + (1.0 - lb) * _sigmoid(hf_ref[0, rows, cols])
        glog = jnp.log(f)
        kk = 1.0 - f
        q = hq * _sigmoid(hq)
        v = hi_ref[0, rows, cols]
        b = _mm(tril, glog, _HI)
        st = state_ref[hd]

        bf = jnp.bfloat16
        vb = v.astype(bf)
        o = _mm_nt((q * jnp.exp(b)).astype(bf), st.astype(bf))

        o_sub = [jnp.zeros((SUB, HEAD_W), jnp.float32)]
        for i in range(1, n_sub):
            beta = b[i * SUB:i * SUB + 1, :]
            qt = q[i * SUB:(i + 1) * SUB, :] * jnp.exp(b[i * SUB:(i + 1) * SUB, :] - beta)
            kt = kk * jnp.exp(jnp.minimum(beta - b, 0.0))
            p = _mm_nt(qt.astype(bf), kt.astype(bf))
            p = jnp.where(sub_col < i * SUB, p, 0.0)
            o_sub.append(_mm(p.astype(bf), vb))
        o = o + jnp.concatenate(o_sub, axis=0)

        for s in range(SUB):
            b_s, k_s, v_s = group_rows(b, s), group_rows(kk, s), group_rows(v, s)
            w = q * k_s * jnp.exp(jnp.where(tloc >= s, b - b_s, _NEG_INF))
            o = o + jnp.sum(w, axis=1, keepdims=True) * v_s

        b_last = b[C - 1:C, :]
        kdec = kk * jnp.exp(b_last - b)
        state_ref[hd] = st * jnp.exp(b_last) + _mm_tn(vb, kdec.astype(bf))

        hgate = hgate_ref[0, rows, cols]
        y = _rms(o, g_out) * (hgate * _sigmoid(hgate))
        o_ref[0, rows, cols] = y.astype(o_ref.dtype)

    per_trip = 2 if n_chunks % 2 == 0 else 1

    def chunk(ci, carry):
        for sub in range(per_trip):
            rows = pl.ds(pl.multiple_of((ci * per_trip + sub) * C, C), C)
            for hd in range(n_heads):
                head_chunk(rows, hd)
        return carry

    lax.fori_loop(0, n_chunks // per_trip, chunk, 0)


def _hgrn(hg_in, lb_theta, g_hgrn_out, layer, sb):
    B, S, _ = hg_in.shape
    H = HGRN_HEADS
    W = H * HEAD_W
    spec = lambda off: pl.BlockSpec((1, sb, W), lambda b, i: (b, i, off))
    n_slots = lb_theta.shape[0]
    return pl.pallas_call(
        functools.partial(_hgrn_kernel, layer=layer, n_chunks=sb // HGRN_CHUNK),
        grid=(B, S // sb),
        in_specs=[spec(0), spec(1), spec(2), spec(3),
                  pl.BlockSpec((n_slots, W), lambda b, i: (0, 0)),
                  pl.BlockSpec((1, HEAD_W), lambda b, i: (0, 0))],
        out_specs=pl.BlockSpec((1, sb, W), lambda b, i: (b, i, 0)),
        out_shape=jax.ShapeDtypeStruct((B, S, W), jnp.bfloat16),
        scratch_shapes=[pltpu.VMEM((H, HEAD_W, HEAD_W), jnp.float32)],
        compiler_params=pltpu.CompilerParams(
            dimension_semantics=("parallel", "arbitrary"), vmem_limit_bytes=VMEM_LIMIT),
        name="hgrn2",
    )(hg_in, hg_in, hg_in, hg_in, lb_theta, g_hgrn_out.reshape(1, HEAD_W))


def _outproj_kernel(ao_ref, ho_ref, x_ref, wo_ref, gpost_ref, gt_ref, gpre_ref, sc_ref, sh_ref, wq_ref,
                    x1_ref, h2_ref, q_ref):
    n_a = ao_ref.shape[2]
    y = _mm(ao_ref[0], wo_ref[0:n_a, :]) + _mm(ho_ref[0], wo_ref[n_a:, :])
    x1 = x_ref[0] + gt_ref[0] * _rms(y, gpost_ref[...])
    x1_ref[0] = x1
    h2 = _rms(x1, gpre_ref[...]) * (1.0 + sc_ref[0]) + sh_ref[0]
    h2_ref[0] = h2
    q_ref[0] = _mm(h2.astype(jnp.bfloat16), wq_ref[...]).astype(q_ref.dtype)


def _out_projection(ao, ho, x, w_out_bf16, g_post, gt1, g_pre, sc2, sh2, w_pq_bf16, tm):
    B, S, D = x.shape
    n_a, n_h, n_q = ao.shape[2], ho.shape[2], w_pq_bf16.shape[1]
    vec = pl.BlockSpec((1, 1, D), lambda b, i: (b, 0, 0))
    par = pl.BlockSpec((1, D), lambda b, i: (0, 0))
    row = lambda n: pl.BlockSpec((1, tm, n), lambda b, i: (b, i, 0))
    return pl.pallas_call(
        _outproj_kernel,
        grid=(B, S // tm),
        in_specs=[row(n_a), row(n_h), row(D),
                  pl.BlockSpec((n_a + n_h, D), lambda b, i: (0, 0)),
                  par, vec, par, vec, vec,
                  pl.BlockSpec((D, n_q), lambda b, i: (0, 0))],
        out_specs=[row(D), row(D), row(n_q)],
        out_shape=[jax.ShapeDtypeStruct((B, S, D), jnp.float32),
                   jax.ShapeDtypeStruct((B, S, D), jnp.float32),
                   jax.ShapeDtypeStruct((B, S, n_q), jnp.bfloat16)],
        compiler_params=pltpu.CompilerParams(
            dimension_semantics=("parallel", "parallel"), vmem_limit_bytes=VMEM_LIMIT),
        name="outproj_norms_peerq",
    )(ao, ho, x, w_out_bf16, g_post.reshape(1, D), gt1, g_pre.reshape(1, D), sc2, sh2, w_pq_bf16)


def _pair_list():
    return [(a, b) for a in range(PEER_TOPK) for b in range(PEER_TOPK) if (a + 1) * (b + 1) <= PEER_TOPK]


def _topk_kernel(q_ref, keys_ref, idx_ref, gate_ref, v_scr, i_scr, cand_scr, cidx_scr, ts_scr, sel_scr):
    K = PEER_TOPK
    tb = q_ref.shape[0]
    pairs = _pair_list()
    n_cand = cand_scr.shape[0]
    kio = lax.broadcasted_iota(jnp.int32, (N_KEYS, tb), 0).astype(jnp.float32)
    pio = lax.broadcasted_iota(jnp.int32, (n_cand, tb), 0).astype(jnp.float32)

    for h in range(PEER_HEADS):
        for j in range(2):
            c0 = (h * 2 + j) * N_KEYS
            s = _mm_nt(keys_ref[h, j], q_ref[:, c0:c0 + N_KEYS])
            for r in range(K):
                m = jnp.max(s, axis=0, keepdims=True)
                am = jnp.min(jnp.where(s == m, kio, float(N_KEYS)), axis=0, keepdims=True)
                v_scr[j, r:r + 1, :] = m
                i_scr[j, r:r + 1, :] = am
                s = jnp.where(kio == am, _NEG_INF, s)
        v1, v2 = v_scr[0], v_scr[1]
        i1, i2 = i_scr[0], i_scr[1]
        cand_scr[...] = jnp.full(cand_scr.shape, _NEG_INF, jnp.float32)
        cidx_scr[...] = jnp.zeros(cidx_scr.shape, jnp.float32)
        off = 0
        for a in range(K):
            nb = sum(1 for (aa, _) in pairs if aa == a)
            cand_scr[off:off + nb, :] = v1[a:a + 1, :] + v2[0:nb, :]
            cidx_scr[off:off + nb, :] = i1[a:a + 1, :] * float(N_KEYS) + i2[0:nb, :]
            off += nb
        cand = cand_scr[...]
        cidx = cidx_scr[...]
        for r in range(K):
            m = jnp.max(cand, axis=0, keepdims=True)
            pos = jnp.min(jnp.where(cand == m, pio, float(n_cand)), axis=0, keepdims=True)
            hit = pio == pos
            ts_scr[r:r + 1, :] = m
            sel_scr[h * K + r:h * K + r + 1, :] = jnp.sum(jnp.where(hit, cidx, 0.0), axis=0, keepdims=True)
            cand = jnp.where(hit, _NEG_INF, cand)
        ts = ts_scr[...]
        e = jnp.exp(ts - ts[0:1, :])
        gate_ref[h * K:(h + 1) * K, :] = e / jnp.sum(e, axis=0, keepdims=True)
    idx_ref[...] = sel_scr[...].T.astype(jnp.int32)


def _peer_topk(q, sub_keys_bf16, tb):
    T = q.shape[0]
    n_sel = PEER_HEADS * PEER_TOPK
    n_cand = -(-len(_pair_list()) // 8) * 8
    return pl.pallas_call(
        _topk_kernel,
        grid=(T // tb,),
        in_specs=[pl.BlockSpec((tb, q.shape[1]), lambda i: (i, 0)),
                  pl.BlockSpec(sub_keys_bf16.shape, lambda i: (0, 0, 0, 0))],
        out_specs=[pl.BlockSpec((tb, n_sel), lambda i: (i, 0)),
                   pl.BlockSpec((n_sel, tb), lambda i: (0, i))],
        out_shape=[jax.ShapeDtypeStruct((T, n_sel), jnp.int32),
                   jax.ShapeDtypeStruct((n_sel, T), jnp.float32)],
        scratch_shapes=[pltpu.VMEM((2, PEER_TOPK, tb), jnp.float32),
                        pltpu.VMEM((2, PEER_TOPK, tb), jnp.float32),
                        pltpu.VMEM((n_cand, tb), jnp.float32),
                        pltpu.VMEM((n_cand, tb), jnp.float32),
                        pltpu.VMEM((PEER_TOPK, tb), jnp.float32),
                        pltpu.VMEM((n_sel, tb), jnp.float32)],
        compiler_params=pltpu.CompilerParams(
            dimension_semantics=("parallel",), vmem_limit_bytes=VMEM_LIMIT),
        name="peer_topk",
    )(q, sub_keys_bf16)


def _peer_kernel(idx_ref, h_ref, gate_ref, x1_ref, gt_ref, g_ref, uv_ref, o_ref, buf, sem, y_scr, *, n_slots):
    tb, D = h_ref.shape
    n_sel = idx_ref.shape[1]
    n_c = D // HEAD_W
    n_t = 2 * n_c
    pitch = buf.shape[1] // n_sel

    def row_copy(t, k, slot):
        e = idx_ref[t, k]
        return pltpu.make_async_copy(uv_ref.at[e], buf.at[slot, pl.ds(k * pitch, n_t), :], sem.at[slot])

    def tile(slot, c):
        return buf[slot, pl.ds(c, n_sel, stride=pitch), :]

    def start_rows(t, slot, k0, k1):
        for k in range(k0, k1):
            row_copy(t, k, slot).start(priority=k % 2)

    def wait_slot(slot):
        done = buf.at[slot, pl.ds(0, n_sel * n_t), :]
        pltpu.make_async_copy(done, done, sem.at[slot]).wait()

    n_ahead = n_slots - 1
    for t in range(n_ahead):
        start_rows(t, t, 0, n_sel)

    lane = lax.broadcasted_iota(jnp.int32, gate_ref.shape, 1)
    per_piece = n_sel // n_t

    def token(t, slot, prefetch):
        def issue(piece):
            if prefetch:
                start_rows(t + n_ahead, (slot + n_ahead) % n_slots, piece * per_piece, (piece + 1) * per_piece)

        wait_slot(slot)
        h = h_ref[pl.ds(t, 1), :]
        p = None
        for c in range(n_c):
            pc = tile(slot, c) * h[:, c * HEAD_W:(c + 1) * HEAD_W]
            p = pc if p is None else p + pc
            issue(c)
        a = jnp.sum(p, axis=1, keepdims=True)
        gcol = jnp.sum(jnp.where(lane == t, gate_ref[...], 0.0), axis=1, keepdims=True)
        w = gcol * (0.5 * a * (1.0 + lax.erf(a * (1.0 / math.sqrt(2.0)))))
        y = []
        for c in range(n_c):
            y.append(jnp.sum(w * tile(slot, n_c + c), axis=0, keepdims=True))
            issue(n_c + c)
        y_scr[pl.ds(t, 1), :] = jnp.concatenate(y, axis=1)

    n_main = tb - n_ahead
    n_groups = n_main // n_slots

    def group(g, carry):
        for j in range(n_slots):
            token(g * n_slots + j, j, True)
        return carry

    lax.fori_loop(0, n_groups, group, 0)
    for t in range(n_groups * n_slots, tb):
        token(t, t % n_slots, t < n_main)
    o_ref[...] = x1_ref[...] + gt_ref[0] * _rms(y_scr[...], g_ref[...])


def _pack_experts(u, v):
    E, D = u.shape
    return jnp.concatenate([u, v], axis=1).reshape(E, 2 * D // HEAD_W, HEAD_W)


def _peer_mix(idx, h2, gate_t, x1, gt2, g_post, uv, seq_len, tb, n_slots):
    T, D = h2.shape
    n_sel = idx.shape[1]
    blocks_per_seq = seq_len // tb
    rows = pl.BlockSpec((tb, D), lambda i: (i, 0))
    return pl.pallas_call(
        functools.partial(_peer_kernel, n_slots=n_slots),
        grid=(T // tb,),
        in_specs=[pl.BlockSpec((tb, n_sel), lambda i: (i, 0), memory_space=pltpu.SMEM),
                  rows,
                  pl.BlockSpec((n_sel, tb), lambda i: (0, i)),
                  rows,
                  pl.BlockSpec((1, 1, D), lambda i: (i // blocks_per_seq, 0, 0)),
                  pl.BlockSpec((1, D), lambda i: (0, 0)),
                  pl.BlockSpec(memory_space=pl.ANY)],
        out_specs=rows,
        out_shape=jax.ShapeDtypeStruct((T, D), jnp.float32),
        scratch_shapes=[pltpu.VMEM((n_slots, n_sel * (2 * D // HEAD_W + 1), HEAD_W), uv.dtype),
                        pltpu.SemaphoreType.DMA((n_slots,)),
                        pltpu.VMEM((tb, D), jnp.float32)],
        compiler_params=pltpu.CompilerParams(
            dimension_semantics=("arbitrary",), vmem_limit_bytes=VMEM_LIMIT),
        name="peer_gather_mix",
    )(idx, h2, gate_t, x1, gt2, g_post.reshape(1, D), uv)


def _pick(n, pref):
    t = min(n, pref)
    assert n % t == 0, (n, pref)
    return t


def kernel(x, c, w_ada, b_ada, g_pre_mix, g_post_mix, g_pre_ffn, g_post_ffn, w_in, lam_qk, g_diff_sub,
           lb_theta, g_hgrn_out, w_out, w_pq, sub_keys, expert_u, expert_v):
    B, S, D = x.shape
    depth = w_in.shape[0]
    n_attn = 3 * DIFF_HEADS * HEAD_W
    tm = _pick(S, 512)
    tq = _pick(S, 512)
    sb = _pick(S, 512)
    tb = _pick(S, 128)
    bf = jnp.bfloat16
    for l in range(depth):
        lam_init = 0.8 - 0.6 * math.exp(-0.3 * l)
        mod = _modulation(c, w_ada[l], b_ada[l]).reshape(N_MOD, B, 1, D)
        sh1, sc1, gt1, sh2, sc2, gt2 = (mod[i] for i in range(N_MOD))
        attn_in, hg_in = _in_projection(x, g_pre_mix[l], sc1, sh1, w_in[l].astype(bf), n_attn, tm)
        ao = _diff_attention(attn_in, lam_qk[l], g_diff_sub[l], lam_init, tq)
        ho = _hgrn(hg_in, lb_theta, g_hgrn_out[l], l, sb)
        x1, h2, q = _out_projection(ao, ho, x, w_out[l].astype(bf), g_post_mix[l], gt1, g_pre_ffn[l],
                                    sc2, sh2, w_pq[l].astype(bf), tm)
        idx, gate_t = _peer_topk(q.reshape(B * S, -1), sub_keys[l].astype(bf), tb)
        uv = _pack_experts(expert_u[l], expert_v[l])
        out = _peer_mix(idx, h2.reshape(B * S, D), gate_t, x1.reshape(B * S, D), gt2, g_post_ffn[l], uv,
                        S, _pick(S, 256), n_slots=6)
        x = out.reshape(B, S, D)
    return x
```

```python
import functools
import math

import jax
import jax.numpy as jnp
from jax import lax
from jax.experimental import pallas as pl
from jax.experimental.pallas import tpu as pltpu

EPS = 1e-6
N_MOD = 6
DIFF_HEADS = 4
DIFF_QK = 64
HEAD_W = 128
HGRN_HEADS = 4
HGRN_CHUNK = 64
HGRN_SUB = 16
PEER_HEADS = 8
N_KEYS = 128
PEER_TOPK = 16
VMEM_LIMIT = 56 * 1024 * 1024

_HI = lax.Precision.HIGHEST
_NEG_INF = float("-inf")
LOG2E = 1.4426950408889634


def _dot(a, b, dims, precision=None):
    return lax.dot_general(a, b, (dims, ((), ())), precision=precision,
                           preferred_element_type=jnp.float32)


def _mm(a, b, precision=None):
    return _dot(a, b, ((1,), (0,)), precision)


def _mm_nt(a, b, precision=None):
    return _dot(a, b, ((1,), (1,)), precision)


def _mm_tn(a, b, precision=None):
    return _dot(a, b, ((0,), (0,)), precision)


def _rms(x, g):
    return x * lax.rsqrt(jnp.mean(x * x, axis=-1, keepdims=True) + EPS) * g


def _sigmoid(x):
    return 1.0 / (1.0 + jnp.exp(-x))


def _mod_kernel(c_ref, w_ref, b_ref, o_ref):
    c = c_ref[...]
    ca = c * _sigmoid(c)
    o_ref[0] = _mm(ca, w_ref[...], _HI) + b_ref[...]


def _modulation(c, w_ada, b_ada):
    B, D = c.shape
    return pl.pallas_call(
        _mod_kernel,
        grid=(N_MOD,),
        in_specs=[pl.BlockSpec((B, D), lambda j: (0, 0)),
                  pl.BlockSpec((D, D), lambda j: (0, j)),
                  pl.BlockSpec((1, D), lambda j: (0, j))],
        out_specs=pl.BlockSpec((1, B, D), lambda j: (j, 0, 0)),
        out_shape=jax.ShapeDtypeStruct((N_MOD, B, D), jnp.float32),
        compiler_params=pltpu.CompilerParams(vmem_limit_bytes=VMEM_LIMIT),
        name="adaln_mod",
    )(c, w_ada, b_ada.reshape(1, N_MOD * D))


def _inproj_kernel(x_ref, g_ref, sc_ref, sh_ref, w_ref, attn_ref, hg_ref, *, n_attn, col_chunk):
    x = x_ref[0]
    h = _rms(x, g_ref[...]) * (1.0 + sc_ref[0]) + sh_ref[0]
    hb = h.astype(jnp.bfloat16)
    n_cols = w_ref.shape[1]
    for c0 in range(0, n_cols, col_chunk):
        r = _mm(hb, w_ref[:, c0:c0 + col_chunk])
        if c0 < n_attn:
            attn_ref[0, :, c0:c0 + col_chunk] = r.astype(attn_ref.dtype)
        else:
            hg_ref[0, :, c0 - n_attn:c0 - n_attn + col_chunk] = r


def _in_projection(x, g, sc, sh, w_in_bf16, n_attn, tm):
    B, S, D = x.shape
    n_cols = w_in_bf16.shape[1]
    vec = pl.BlockSpec((1, 1, D), lambda b, i: (b, 0, 0))
    return pl.pallas_call(
        functools.partial(_inproj_kernel, n_attn=n_attn, col_chunk=512),
        grid=(B, S // tm),
        in_specs=[pl.BlockSpec((1, tm, D), lambda b, i: (b, i, 0)),
                  pl.BlockSpec((1, D), lambda b, i: (0, 0)),
                  vec, vec,
                  pl.BlockSpec((D, n_cols), lambda b, i: (0, 0))],
        out_specs=[pl.BlockSpec((1, tm, n_attn), lambda b, i: (b, i, 0)),
                   pl.BlockSpec((1, tm, n_cols - n_attn), lambda b, i: (b, i, 0))],
        out_shape=[jax.ShapeDtypeStruct((B, S, n_attn), jnp.bfloat16),
                   jax.ShapeDtypeStruct((B, S, n_cols - n_attn), jnp.float32)],
        compiler_params=pltpu.CompilerParams(
            dimension_semantics=("parallel", "parallel"), vmem_limit_bytes=VMEM_LIMIT),
        name="prenorm_inproj",
    )(x, g.reshape(1, D), sc, sh, w_in_bf16)


def _fold_lanes(x, op):
    out = x[:, 0:HEAD_W]
    for i in range(1, x.shape[1] // HEAD_W):
        out = op(out, x[:, i * HEAD_W:(i + 1) * HEAD_W])
    return out


def _attn_kernel(q_ref, k_ref, v_ref, lam_ref, g_ref, o_ref, s_scr, mx_scr, l_scr, acc_scr, *, tq, lam_init):
    qi = pl.program_id(2)
    q = q_ref[0]
    lane = lax.broadcasted_iota(jnp.int32, q.shape, 1)
    qs = q * jnp.asarray(1.0 / math.sqrt(DIFF_QK), q.dtype)
    zero = jnp.zeros_like(qs)
    q2 = jnp.concatenate([jnp.where(lane < DIFF_QK, qs, zero), jnp.where(lane >= DIFF_QK, qs, zero)], axis=0)
    n_rep = tq // HEAD_W

    mx_scr[...] = jnp.full(mx_scr.shape, _NEG_INF, jnp.float32)

    def scores(j, carry):
        s = _mm_nt(q2, k_ref[0, pl.ds(pl.multiple_of(j * tq, tq), tq), :]) * LOG2E
        s_scr[j] = s
        mx_scr[...] = jnp.maximum(mx_scr[...], _fold_lanes(s, jnp.maximum))
        return carry

    lax.fori_loop(0, qi, scores, 0)
    s = _mm_nt(q2, k_ref[0, pl.ds(pl.multiple_of(qi * tq, tq), tq), :]) * LOG2E
    row = lax.broadcasted_iota(jnp.int32, s.shape, 0) & (tq - 1)
    col = lax.broadcasted_iota(jnp.int32, s.shape, 1)
    s = jnp.where(col <= row, s, _NEG_INF)
    s_scr[qi] = s
    m = jnp.max(jnp.maximum(mx_scr[...], _fold_lanes(s, jnp.maximum)), axis=1, keepdims=True)
    mx_scr[...] = jnp.broadcast_to(m, mx_scr.shape)
    l_scr[...] = jnp.zeros(l_scr.shape, jnp.float32)
    acc_scr[...] = jnp.zeros(acc_scr.shape, jnp.float32)

    def contract(j, carry):
        mb = mx_scr[...]
        p = jnp.exp2(s_scr[j] - jnp.concatenate([mb] * n_rep, axis=1))
        l_scr[...] += _fold_lanes(p, jnp.add)
        vb = v_ref[0, pl.ds(pl.multiple_of(j * tq, tq), tq), :]
        acc_scr[...] += _mm(p.astype(vb.dtype), vb)
        return carry

    lax.fori_loop(0, qi + 1, contract, 0)

    lq = lam_ref[...]
    lam = (jnp.exp(jnp.sum(lq[0:1] * lq[1:2], axis=1, keepdims=True))
           - jnp.exp(jnp.sum(lq[2:3] * lq[3:4], axis=1, keepdims=True)) + lam_init)
    o = acc_scr[...] / jnp.sum(l_scr[...], axis=1, keepdims=True)
    o = o[0:tq] - lam * o[tq:2 * tq]
    o = _rms(o, g_ref[...]) * (1.0 - lam_init)
    o_ref[0] = o.astype(o_ref.dtype)


def _diff_attention(attn_in, lam_qk, g_diff_sub, lam_init, tq):
    B, S, _ = attn_in.shape
    H = DIFF_HEADS
    assert tq % HEAD_W == 0 and tq & (tq - 1) == 0, tq
    kv_spec = lambda off: pl.BlockSpec((1, S, HEAD_W), lambda b, h, i: (b, 0, off + h))
    return pl.pallas_call(
        functools.partial(_attn_kernel, tq=tq, lam_init=lam_init),
        grid=(B, H, S // tq),
        in_specs=[pl.BlockSpec((1, tq, HEAD_W), lambda b, h, i: (b, i, h)),
                  kv_spec(H), kv_spec(2 * H),
                  pl.BlockSpec(lam_qk.shape, lambda b, h, i: (0, 0)),
                  pl.BlockSpec((1, HEAD_W), lambda b, h, i: (0, 0))],
        out_specs=pl.BlockSpec((1, tq, HEAD_W), lambda b, h, i: (b, i, h)),
        out_shape=jax.ShapeDtypeStruct((B, S, H * HEAD_W), jnp.bfloat16),
        scratch_shapes=[pltpu.VMEM((S // tq, 2 * tq, tq), jnp.float32),
                        pltpu.VMEM((2 * tq, HEAD_W), jnp.float32),
                        pltpu.VMEM((2 * tq, HEAD_W), jnp.float32),
                        pltpu.VMEM((2 * tq, HEAD_W), jnp.float32)],
        compiler_params=pltpu.CompilerParams(
            dimension_semantics=("parallel", "parallel", "arbitrary"), vmem_limit_bytes=VMEM_LIMIT),
        name="diff_attention",
    )(attn_in, attn_in, attn_in, lam_qk, g_diff_sub.reshape(1, HEAD_W))


def _hgrn_kernel(hq_ref, hf_ref, hi_ref, hgate_ref, lbt_ref, g_ref, o_ref, state_ref, *, layer, n_chunks):
    C, SUB = HGRN_CHUNK, HGRN_SUB
    n_sub = C // SUB
    n_heads = state_ref.shape[0]

    @pl.when(pl.program_id(1) == 0)
    def _():
        state_ref[...] = jnp.zeros(state_ref.shape, jnp.float32)

    th = lbt_ref[...]
    e = jnp.exp(th - jnp.max(th, axis=0, keepdims=True))
    lb_all = jnp.sum(e[0:layer + 1], axis=0, keepdims=True) / jnp.sum(e, axis=0, keepdims=True)

    r_io = lax.broadcasted_iota(jnp.int32, (C, C), 0)
    c_io = lax.broadcasted_iota(jnp.int32, (C, C), 1)
    tril = (c_io <= r_io).astype(jnp.float32)
    tloc = lax.broadcasted_iota(jnp.int32, (C, HEAD_W), 0) % SUB
    sub_col = lax.broadcasted_iota(jnp.int32, (SUB, C), 1)
    g_out = g_ref[...]

    def group_rows(x, s):
        return jnp.concatenate(
            [jnp.broadcast_to(x[i * SUB + s:i * SUB + s + 1, :], (SUB, HEAD_W)) for i in range(n_sub)], axis=0)

    def head_chunk(rows, hd):
        cols = slice(hd * HEAD_W, (hd + 1) * HEAD_W)
        lb = lb_all[:, cols]
        hq = hq_ref[0, rows, cols]
        f = lb + (1.0 - lb) * _sigmoid(hf_ref[0, rows, cols])
        glog = jnp.log(f)
        kk = 1.0 - f
        q = hq * _sigmoid(hq)
        v = hi_ref[0, rows, cols]
        b = _mm(tril, glog, _HI)
        st = state_ref[hd]

        bf = jnp.bfloat16
        vb = v.astype(bf)
        o = _mm_nt((q * jnp.exp(b)).astype(bf), st.astype(bf))

        o_sub = [jnp.zeros((SUB, HEAD_W), jnp.float32)]
        for i in range(1, n_sub):
            beta = b[i * SUB:i * SUB + 1, :]
            qt = q[i * SUB:(i + 1) * SUB, :] * jnp.exp(b[i * SUB:(i + 1) * SUB, :] - beta)
            kt = kk * jnp.exp(jnp.minimum(beta - b, 0.0))
            p = _mm_nt(qt.astype(bf), kt.astype(bf))
            p = jnp.where(sub_col < i * SUB, p, 0.0)
            o_sub.append(_mm(p.astype(bf), vb))
        o = o + jnp.concatenate(o_sub, axis=0)

        for s in range(SUB):
            b_s, k_s, v_s = group_rows(b, s), group_rows(kk, s), group_rows(v, s)
            w = q * k_s * jnp.exp(jnp.where(tloc >= s, b - b_s, _NEG_INF))
            o = o + jnp.sum(w, axis=1, keepdims=True) * v_s

        b_last = b[C - 1:C, :]
        kdec = kk * jnp.exp(b_last - b)
        state_ref[hd] = st * jnp.exp(b_last) + _mm_tn(vb, kdec.astype(bf))

        hgate = hgate_ref[0, rows, cols]
        y = _rms(o, g_out) * (hgate * _sigmoid(hgate))
        o_ref[0, rows, cols] = y.astype(o_ref.dtype)

    per_trip = 2 if n_chunks % 2 == 0 else 1

    def chunk(ci, carry):
        for sub in range(per_trip):
            rows = pl.ds(pl.multiple_of((ci * per_trip + sub) * C, C), C)
            for hd in range(n_heads):
                head_chunk(rows, hd)
        return carry

    lax.fori_loop(0, n_chunks // per_trip, chunk, 0)


def _hgrn(hg_in, lb_theta, g_hgrn_out, layer, sb):
    B, S, _ = hg_in.shape
    H = HGRN_HEADS
    W = H * HEAD_W
    spec = lambda off: pl.BlockSpec((1, sb, W), lambda b, i: (b, i, off))
    n_slots = lb_theta.shape[0]
    return pl.pallas_call(
        functools.partial(_hgrn_kernel, layer=layer, n_chunks=sb // HGRN_CHUNK),
        grid=(B, S // sb),
        in_specs=[spec(0), spec(1), spec(2), spec(3),
                  pl.BlockSpec((n_slots, W), lambda b, i: (0, 0)),
                  pl.BlockSpec((1, HEAD_W), lambda b, i: (0, 0))],
        out_specs=pl.BlockSpec((1, sb, W), lambda b, i: (b, i, 0)),
        out_shape=jax.ShapeDtypeStruct((B, S, W), jnp.bfloat16),
        scratch_shapes=[pltpu.VMEM((H, HEAD_W, HEAD_W), jnp.float32)],
        compiler_params=pltpu.CompilerParams(
            dimension_semantics=("parallel", "arbitrary"), vmem_limit_bytes=VMEM_LIMIT),
        name="hgrn2",
    )(hg_in, hg_in, hg_in, hg_in, lb_theta, g_hgrn_out.reshape(1, HEAD_W))


def _outproj_kernel(ao_ref, ho_ref, x_ref, wo_ref, gpost_ref, gt_ref, gpre_ref, sc_ref, sh_ref, wq_ref,
                    x1_ref, h2_ref, q_ref):
    n_a = ao_ref.shape[2]
    y = _mm(ao_ref[0], wo_ref[0:n_a, :]) + _mm(ho_ref[0], wo_ref[n_a:, :])
    x1 = x_ref[0] + gt_ref[0] * _rms(y, gpost_ref[...])
    x1_ref[0] = x1
    h2 = _rms(x1, gpre_ref[...]) * (1.0 + sc_ref[0]) + sh_ref[0]
    h2_ref[0] = h2
    q_ref[0] = _mm(h2.astype(jnp.bfloat16), wq_ref[...]).astype(q_ref.dtype)


def _out_projection(ao, ho, x, w_out_bf16, g_post, gt1, g_pre, sc2, sh2, w_pq_bf16, tm):
    B, S, D = x.shape
    n_a, n_h, n_q = ao.shape[2], ho.shape[2], w_pq_bf16.shape[1]
    vec = pl.BlockSpec((1, 1, D), lambda b, i: (b, 0, 0))
    par = pl.BlockSpec((1, D), lambda b, i: (0, 0))
    row = lambda n: pl.BlockSpec((1, tm, n), lambda b, i: (b, i, 0))
    return pl.pallas_call(
        _outproj_kernel,
        grid=(B, S // tm),
        in_specs=[row(n_a), row(n_h), row(D),
                  pl.BlockSpec((n_a + n_h, D), lambda b, i: (0, 0)),
                  par, vec, par, vec, vec,
                  pl.BlockSpec((D, n_q), lambda b, i: (0, 0))],
        out_specs=[row(D), row(D), row(n_q)],
        out_shape=[jax.ShapeDtypeStruct((B, S, D), jnp.float32),
                   jax.ShapeDtypeStruct((B, S, D), jnp.float32),
                   jax.ShapeDtypeStruct((B, S, n_q), jnp.bfloat16)],
        compiler_params=pltpu.CompilerParams(
            dimension_semantics=("parallel", "parallel"), vmem_limit_bytes=VMEM_LIMIT),
        name="outproj_norms_peerq",
    )(ao, ho, x, w_out_bf16, g_post.reshape(1, D), gt1, g_pre.reshape(1, D), sc2, sh2, w_pq_bf16)


def _pair_list():
    return [(a, b) for a in range(PEER_TOPK) for b in range(PEER_TOPK) if (a + 1) * (b + 1) <= PEER_TOPK]


def _topk_kernel(q_ref, keys_ref, idx_ref, gate_ref, v_scr, i_scr, cand_scr, cidx_scr, ts_scr, sel_scr):
    K = PEER_TOPK
    tb = q_ref.shape[0]
    pairs = _pair_list()
    n_cand = cand_scr.shape[0]
    kio = lax.broadcasted_iota(jnp.int32, (N_KEYS, tb), 0).astype(jnp.float32)
    pio = lax.broadcasted_iota(jnp.int32, (n_cand, tb), 0).astype(jnp.float32)

    for h in range(PEER_HEADS):
        for j in range(2):
            c0 = (h * 2 + j) * N_KEYS
            s = _mm_nt(keys_ref[h, j], q_ref[:, c0:c0 + N_KEYS])
            for r in range(K):
                m = jnp.max(s, axis=0, keepdims=True)
                am = jnp.min(jnp.where(s == m, kio, float(N_KEYS)), axis=0, keepdims=True)
                v_scr[j, r:r + 1, :] = m
                i_scr[j, r:r + 1, :] = am
                s = jnp.where(kio == am, _NEG_INF, s)
        v1, v2 = v_scr[0], v_scr[1]
        i1, i2 = i_scr[0], i_scr[1]
        cand_scr[...] = jnp.full(cand_scr.shape, _NEG_INF, jnp.float32)
        cidx_scr[...] = jnp.zeros(cidx_scr.shape, jnp.float32)
        off = 0
        for a in range(K):
            nb = sum(1 for (aa, _) in pairs if aa == a)
            cand_scr[off:off + nb, :] = v1[a:a + 1, :] + v2[0:nb, :]
            cidx_scr[off:off + nb, :] = i1[a:a + 1, :] * float(N_KEYS) + i2[0:nb, :]
            off += nb
        cand = cand_scr[...]
        cidx = cidx_scr[...]
        for r in range(K):
            m = jnp.max(cand, axis=0, keepdims=True)
            pos = jnp.min(jnp.where(cand == m, pio, float(n_cand)), axis=0, keepdims=True)
            hit = pio == pos
            ts_scr[r:r + 1, :] = m
            sel_scr[h * K + r:h * K + r + 1, :] = jnp.sum(jnp.where(hit, cidx, 0.0), axis=0, keepdims=True)
            cand = jnp.where(hit, _NEG_INF, cand)
        ts = ts_scr[...]
        e = jnp.exp(ts - ts[0:1, :])
        gate_ref[h * K:(h + 1) * K, :] = e / jnp.sum(e, axis=0, keepdims=True)
    idx_ref[...] = sel_scr[...].T.astype(jnp.int32)


def _peer_topk(q, sub_keys_bf16, tb):
    T = q.shape[0]
    n_sel = PEER_HEADS * PEER_TOPK
    n_cand = -(-len(_pair_list()) // 8) * 8
    return pl.pallas_call(
        _topk_kernel,
        grid=(T // tb,),
        in_specs=[pl.BlockSpec((tb, q.shape[1]), lambda i: (i, 0)),
                  pl.BlockSpec(sub_keys_bf16.shape, lambda i: (0, 0, 0, 0))],
        out_specs=[pl.BlockSpec((tb, n_sel), lambda i: (i, 0)),
                   pl.BlockSpec((n_sel, tb), lambda i: (0, i))],
        out_shape=[jax.ShapeDtypeStruct((T, n_sel), jnp.int32),
                   jax.ShapeDtypeStruct((n_sel, T), jnp.float32)],
        scratch_shapes=[pltpu.VMEM((2, PEER_TOPK, tb), jnp.float32),
                        pltpu.VMEM((2, PEER_TOPK, tb), jnp.float32),
                        pltpu.VMEM((n_cand, tb), jnp.float32),
                        pltpu.VMEM((n_cand, tb), jnp.float32),
                        pltpu.VMEM((PEER_TOPK, tb), jnp.float32),
                        pltpu.VMEM((n_sel, tb), jnp.float32)],
        compiler_params=pltpu.CompilerParams(
            dimension_semantics=("parallel",), vmem_limit_bytes=VMEM_LIMIT),
        name="peer_topk",
    )(q, sub_keys_bf16)


def _peer_kernel(idx_ref, h_ref, gate_ref, x1_ref, gt_ref, g_ref, uv_ref, o_ref, buf, sem, y_scr, *, n_slots):
    tb, D = h_ref.shape
    n_sel = idx_ref.shape[1]
    n_c = D // HEAD_W
    n_t = 2 * n_c
    pitch = buf.shape[1] // n_sel

    def row_copy(t, k, slot):
        e = idx_ref[t, k]
        return pltpu.make_async_copy(uv_ref.at[e], buf.at[slot, pl.ds(k * pitch, n_t), :], sem.at[slot])

    def tile(slot, c):
        return buf[slot, pl.ds(c, n_sel, stride=pitch), :]

    def start_rows(t, slot, k0, k1):
        for k in range(k0, k1):
            row_copy(t, k, slot).start(priority=k % 2)

    def wait_slot(slot):
        done = buf.at[slot, pl.ds(0, n_sel * n_t), :]
        pltpu.make_async_copy(done, done, sem.at[slot]).wait()

    n_ahead = n_slots - 1
    for t in range(n_ahead):
        start_rows(t, t, 0, n_sel)

    lane = lax.broadcasted_iota(jnp.int32, gate_ref.shape, 1)
    per_piece = n_sel // n_t

    def token(t, slot, prefetch):
        def issue(piece):
            if prefetch:
                start_rows(t + n_ahead, (slot + n_ahead) % n_slots, piece * per_piece, (piece + 1) * per_piece)

        wait_slot(slot)
        h = h_ref[pl.ds(t, 1), :]
        p = None
        for c in range(n_c):
            pc = tile(slot, c) * h[:, c * HEAD_W:(c + 1) * HEAD_W]
            p = pc if p is None else p + pc
            issue(c)
        a = jnp.sum(p, axis=1, keepdims=True)
        gcol = jnp.sum(jnp.where(lane == t, gate_ref[...], 0.0), axis=1, keepdims=True)
        w = gcol * (0.5 * a * (1.0 + lax.erf(a * (1.0 / math.sqrt(2.0)))))
        y = []
        for c in range(n_c):
            y.append(jnp.sum(w * tile(slot, n_c + c), axis=0, keepdims=True))
            issue(n_c + c)
        y_scr[pl.ds(t, 1), :] = jnp.concatenate(y, axis=1)

    n_main = tb - n_ahead
    n_groups = n_main // n_slots

    def group(g, carry):
        for j in range(n_slots):
            token(g * n_slots + j, j, True)
        return carry

    lax.fori_loop(0, n_groups, group, 0)
    for t in range(n_groups * n_slots, tb):
        token(t, t % n_slots, t < n_main)
    o_ref[...] = x1_ref[...] + gt_ref[0] * _rms(y_scr[...], g_ref[...])


def _pack_experts(u, v):
    E, D = u.shape
    return jnp.concatenate([u, v], axis=1).reshape(E, 2 * D // HEAD_W, HEAD_W)


def _peer_mix(idx, h2, gate_t, x1, gt2, g_post, uv, seq_len, tb, n_slots):
    T, D = h2.shape
    n_sel = idx.shape[1]
    blocks_per_seq = seq_len // tb
    rows = pl.BlockSpec((tb, D), lambda i: (i, 0))
    return pl.pallas_call(
        functools.partial(_peer_kernel, n_slots=n_slots),
        grid=(T // tb,),
        in_specs=[pl.BlockSpec((tb, n_sel), lambda i: (i, 0), memory_space=pltpu.SMEM),
                  rows,
                  pl.BlockSpec((n_sel, tb), lambda i: (0, i)),
                  rows,
                  pl.BlockSpec((1, 1, D), lambda i: (i // blocks_per_seq, 0, 0)),
                  pl.BlockSpec((1, D), lambda i: (0, 0)),
                  pl.BlockSpec(memory_space=pl.ANY)],
        out_specs=rows,
        out_shape=jax.ShapeDtypeStruct((T, D), jnp.float32),
        scratch_shapes=[pltpu.VMEM((n_slots, n_sel * (2 * D // HEAD_W + 1), HEAD_W), uv.dtype),
                        pltpu.SemaphoreType.DMA((n_slots,)),
                        pltpu.VMEM((tb, D), jnp.float32)],
        compiler_params=pltpu.CompilerParams(
            dimension_semantics=("arbitrary",), vmem_limit_bytes=VMEM_LIMIT),
        name="peer_gather_mix",
    )(idx, h2, gate_t, x1, gt2, g_post.reshape(1, D), uv)


def _pick(n, pref):
    t = min(n, pref)
    assert n % t == 0, (n, pref)
    return t


def kernel(x, c, w_ada, b_ada, g_pre_mix, g_post_mix, g_pre_ffn, g_post_ffn, w_in, lam_qk, g_diff_sub,
           lb_theta, g_hgrn_out, w_out, w_pq, sub_keys, expert_u, expert_v):
    B, S, D = x.shape
    depth = w_in.shape[0]
    n_attn = 3 * DIFF_HEADS * HEAD_W
    tm = _pick(S, 512)
    tq = _pick(S, 512)
    sb = _pick(S, 512)
    tb = _pick(S, 128)
    bf = jnp.bfloat16
    for l in range(depth):
        lam_init = 0.8 - 0.6 * math.exp(-0.3 * l)
        mod = _modulation(c, w_ada[l], b_ada[l]).reshape(N_MOD, B, 1, D)
        sh1, sc1, gt1, sh2, sc2, gt2 = (mod[i] for i in range(N_MOD))
        attn_in, hg_in = _in_projection(x, g_pre_mix[l], sc1, sh1, w_in[l].astype(bf), n_attn, tm)
        ao = _diff_attention(attn_in, lam_qk[l], g_diff_sub[l], lam_init, tq)
        ho = _hgrn(hg_in, lb_theta, g_hgrn_out[l], l, sb)
        x1, h2, q = _out_projection(ao, ho, x, w_out[l].astype(bf), g_post_mix[l], gt1, g_pre_ffn[l],
                                    sc2, sh2, w_pq[l].astype(bf), tm)
        idx, gate_t = _peer_topk(q.reshape(B * S, -1), sub_keys[l].astype(bf), tb)
        uv = _pack_experts(expert_u[l], expert_v[l])
        out = _peer_mix(idx, h2.reshape(B * S, D), gate_t, x1.reshape(B * S, D), gt2, g_post_ffn[l], uv,
                        S, _pick(S, 256), n_slots=10)
        x = out.reshape(B, S, D)
    return x
```

```python
import functools
import math

import jax
import jax.numpy as jnp
from jax import lax
from jax.experimental import pallas as pl
from jax.experimental.pallas import tpu as pltpu

EPS = 1e-6
N_MOD = 6
DIFF_HEADS = 4
DIFF_QK = 64
HEAD_W = 128
HGRN_HEADS = 4
HGRN_CHUNK = 64
HGRN_SUB = 16
PEER_HEADS = 8
N_KEYS = 128
PEER_TOPK = 16
VMEM_LIMIT = 56 * 1024 * 1024

_HI = lax.Precision.HIGHEST
_NEG_INF = float("-inf")
LOG2E = 1.4426950408889634


def _dot(a, b, dims, precision=None):
    return lax.dot_general(a, b, (dims, ((), ())), precision=precision,
                           preferred_element_type=jnp.float32)


def _mm(a, b, precision=None):
    return _dot(a, b, ((1,), (0,)), precision)


def _mm_nt(a, b, precision=None):
    return _dot(a, b, ((1,), (1,)), precision)


def _mm_tn(a, b, precision=None):
    return _dot(a, b, ((0,), (0,)), precision)


def _rms(x, g):
    return x * lax.rsqrt(jnp.mean(x * x, axis=-1, keepdims=True) + EPS) * g


def _sigmoid(x):
    return 1.0 / (1.0 + jnp.exp(-x))


def _mod_kernel(c_ref, w_ref, b_ref, o_ref):
    c = c_ref[...]
    ca = c * _sigmoid(c)
    o_ref[0] = _mm(ca, w_ref[...], _HI) + b_ref[...]


def _modulation(c, w_ada, b_ada):
    B, D = c.shape
    return pl.pallas_call(
        _mod_kernel,
        grid=(N_MOD,),
        in_specs=[pl.BlockSpec((B, D), lambda j: (0, 0)),
                  pl.BlockSpec((D, D), lambda j: (0, j)),
                  pl.BlockSpec((1, D), lambda j: (0, j))],
        out_specs=pl.BlockSpec((1, B, D), lambda j: (j, 0, 0)),
        out_shape=jax.ShapeDtypeStruct((N_MOD, B, D), jnp.float32),
        compiler_params=pltpu.CompilerParams(vmem_limit_bytes=VMEM_LIMIT),
        name="adaln_mod",
    )(c, w_ada, b_ada.reshape(1, N_MOD * D))


def _inproj_kernel(x_ref, g_ref, sc_ref, sh_ref, w_ref, attn_ref, hg_ref, *, n_attn, col_chunk):
    x = x_ref[0]
    h = _rms(x, g_ref[...]) * (1.0 + sc_ref[0]) + sh_ref[0]
    hb = h.astype(jnp.bfloat16)
    n_cols = w_ref.shape[1]
    for c0 in range(0, n_cols, col_chunk):
        r = _mm(hb, w_ref[:, c0:c0 + col_chunk])
        if c0 < n_attn:
            attn_ref[0, :, c0:c0 + col_chunk] = r.astype(attn_ref.dtype)
        else:
            hg_ref[0, :, c0 - n_attn:c0 - n_attn + col_chunk] = r


def _in_projection(x, g, sc, sh, w_in_bf16, n_attn, tm):
    B, S, D = x.shape
    n_cols = w_in_bf16.shape[1]
    vec = pl.BlockSpec((1, 1, D), lambda b, i: (b, 0, 0))
    return pl.pallas_call(
        functools.partial(_inproj_kernel, n_attn=n_attn, col_chunk=512),
        grid=(B, S // tm),
        in_specs=[pl.BlockSpec((1, tm, D), lambda b, i: (b, i, 0)),
                  pl.BlockSpec((1, D), lambda b, i: (0, 0)),
                  vec, vec,
                  pl.BlockSpec((D, n_cols), lambda b, i: (0, 0))],
        out_specs=[pl.BlockSpec((1, tm, n_attn), lambda b, i: (b, i, 0)),
                   pl.BlockSpec((1, tm, n_cols - n_attn), lambda b, i: (b, i, 0))],
        out_shape=[jax.ShapeDtypeStruct((B, S, n_attn), jnp.bfloat16),
                   jax.ShapeDtypeStruct((B, S, n_cols - n_attn), jnp.float32)],
        compiler_params=pltpu.CompilerParams(
            dimension_semantics=("parallel", "parallel"), vmem_limit_bytes=VMEM_LIMIT),
        name="prenorm_inproj",
    )(x, g.reshape(1, D), sc, sh, w_in_bf16)


def _fold_lanes(x, op):
    out = x[:, 0:HEAD_W]
    for i in range(1, x.shape[1] // HEAD_W):
        out = op(out, x[:, i * HEAD_W:(i + 1) * HEAD_W])
    return out


def _attn_kernel(q_ref, k_ref, v_ref, lam_ref, g_ref, o_ref, s_scr, mx_scr, l_scr, acc_scr, *, tq, lam_init):
    qi = pl.program_id(2)
    q = q_ref[0]
    lane = lax.broadcasted_iota(jnp.int32, q.shape, 1)
    qs = q * jnp.asarray(1.0 / math.sqrt(DIFF_QK), q.dtype)
    zero = jnp.zeros_like(qs)
    q2 = jnp.concatenate([jnp.where(lane < DIFF_QK, qs, zero), jnp.where(lane >= DIFF_QK, qs, zero)], axis=0)
    n_rep = tq // HEAD_W

    mx_scr[...] = jnp.full(mx_scr.shape, _NEG_INF, jnp.float32)

    def scores(j, carry):
        s = _mm_nt(q2, k_ref[0, pl.ds(pl.multiple_of(j * tq, tq), tq), :]) * LOG2E
        s_scr[j] = s
        mx_scr[...] = jnp.maximum(mx_scr[...], _fold_lanes(s, jnp.maximum))
        return carry

    lax.fori_loop(0, qi, scores, 0)
    s = _mm_nt(q2, k_ref[0, pl.ds(pl.multiple_of(qi * tq, tq), tq), :]) * LOG2E
    row = lax.broadcasted_iota(jnp.int32, s.shape, 0) & (tq - 1)
    col = lax.broadcasted_iota(jnp.int32, s.shape, 1)
    s = jnp.where(col <= row, s, _NEG_INF)
    s_scr[qi] = s
    m = jnp.max(jnp.maximum(mx_scr[...], _fold_lanes(s, jnp.maximum)), axis=1, keepdims=True)
    mx_scr[...] = jnp.broadcast_to(m, mx_scr.shape)
    l_scr[...] = jnp.zeros(l_scr.shape, jnp.float32)
    acc_scr[...] = jnp.zeros(acc_scr.shape, jnp.float32)

    def contract(j, carry):
        mb = mx_scr[...]
        p = jnp.exp2(s_scr[j] - jnp.concatenate([mb] * n_rep, axis=1))
        l_scr[...] += _fold_lanes(p, jnp.add)
        vb = v_ref[0, pl.ds(pl.multiple_of(j * tq, tq), tq), :]
        acc_scr[...] += _mm(p.astype(vb.dtype), vb)
        return carry

    lax.fori_loop(0, qi + 1, contract, 0)

    lq = lam_ref[...]
    lam = (jnp.exp(jnp.sum(lq[0:1] * lq[1:2], axis=1, keepdims=True))
           - jnp.exp(jnp.sum(lq[2:3] * lq[3:4], axis=1, keepdims=True)) + lam_init)
    o = acc_scr[...] / jnp.sum(l_scr[...], axis=1, keepdims=True)
    o = o[0:tq] - lam * o[tq:2 * tq]
    o = _rms(o, g_ref[...]) * (1.0 - lam_init)
    o_ref[0] = o.astype(o_ref.dtype)


def _diff_attention(attn_in, lam_qk, g_diff_sub, lam_init, tq):
    B, S, _ = attn_in.shape
    H = DIFF_HEADS
    assert tq % HEAD_W == 0 and tq & (tq - 1) == 0, tq
    kv_spec = lambda off: pl.BlockSpec((1, S, HEAD_W), lambda b, h, i: (b, 0, off + h))
    return pl.pallas_call(
        functools.partial(_attn_kernel, tq=tq, lam_init=lam_init),
        grid=(B, H, S // tq),
        in_specs=[pl.BlockSpec((1, tq, HEAD_W), lambda b, h, i: (b, i, h)),
                  kv_spec(H), kv_spec(2 * H),
                  pl.BlockSpec(lam_qk.shape, lambda b, h, i: (0, 0)),
                  pl.BlockSpec((1, HEAD_W), lambda b, h, i: (0, 0))],
        out_specs=pl.BlockSpec((1, tq, HEAD_W), lambda b, h, i: (b, i, h)),
        out_shape=jax.ShapeDtypeStruct((B, S, H * HEAD_W), jnp.bfloat16),
        scratch_shapes=[pltpu.VMEM((S // tq, 2 * tq, tq), jnp.float32),
                        pltpu.VMEM((2 * tq, HEAD_W), jnp.float32),
                        pltpu.VMEM((2 * tq, HEAD_W), jnp.float32),
                        pltpu.VMEM((2 * tq, HEAD_W), jnp.float32)],
        compiler_params=pltpu.CompilerParams(
            dimension_semantics=("parallel", "parallel", "arbitrary"), vmem_limit_bytes=VMEM_LIMIT),
        name="diff_attention",
    )(attn_in, attn_in, attn_in, lam_qk, g_diff_sub.reshape(1, HEAD_W))


def _hgrn_kernel(hq_ref, hf_ref, hi_ref, hgate_ref, lbt_ref, g_ref, o_ref, state_ref, *, layer, n_chunks):
    C, SUB = HGRN_CHUNK, HGRN_SUB
    n_sub = C // SUB
    n_heads = state_ref.shape[0]

    @pl.when(pl.program_id(1) == 0)
    def _():
        state_ref[...] = jnp.zeros(state_ref.shape, jnp.float32)

    th = lbt_ref[...]
    e = jnp.exp(th - jnp.max(th, axis=0, keepdims=True))
    lb_all = jnp.sum(e[0:layer + 1], axis=0, keepdims=True) / jnp.sum(e, axis=0, keepdims=True)

    r_io = lax.broadcasted_iota(jnp.int32, (C, C), 0)
    c_io = lax.broadcasted_iota(jnp.int32, (C, C), 1)
    tril = (c_io <= r_io).astype(jnp.float32)
    tloc = lax.broadcasted_iota(jnp.int32, (C, HEAD_W), 0) % SUB
    sub_col = lax.broadcasted_iota(jnp.int32, (SUB, C), 1)
    g_out = g_ref[...]

    def group_rows(x, s):
        return jnp.concatenate(
            [jnp.broadcast_to(x[i * SUB + s:i * SUB + s + 1, :], (SUB, HEAD_W)) for i in range(n_sub)], axis=0)

    def head_chunk(rows, hd):
        cols = slice(hd * HEAD_W, (hd + 1) * HEAD_W)
        lb = lb_all[:, cols]
        hq = hq_ref[0, rows, cols]
        f = lb + (1.0 - lb) * _sigmoid(hf_ref[0, rows, cols])
        glog = jnp.log(f)
        kk = 1.0 - f
        q = hq * _sigmoid(hq)
        v = hi_ref[0, rows, cols]
        b = _mm(tril, glog, _HI)
        st = state_ref[hd]

        bf = jnp.bfloat16
        vb = v.astype(bf)
        o = _mm_nt((q * jnp.exp(b)).astype(bf), st.astype(bf))

        o_sub = [jnp.zeros((SUB, HEAD_W), jnp.float32)]
        for i in range(1, n_sub):
            beta = b[i * SUB:i * SUB + 1, :]
            qt = q[i * SUB:(i + 1) * SUB, :] * jnp.exp(b[i * SUB:(i + 1) * SUB, :] - beta)
            kt = kk * jnp.exp(jnp.minimum(beta - b, 0.0))
            p = _mm_nt(qt.astype(bf), kt.astype(bf))
            p = jnp.where(sub_col < i * SUB, p, 0.0)
            o_sub.append(_mm(p.astype(bf), vb))
        o = o + jnp.concatenate(o_sub, axis=0)

        for s in range(SUB):
            b_s, k_s, v_s = group_rows(b, s), group_rows(kk, s), group_rows(v, s)
            w = q * k_s * jnp.exp(jnp.where(tloc >= s, b - b_s, _NEG_INF))
            o = o + jnp.sum(w, axis=1, keepdims=True) * v_s

        b_last = b[C - 1:C, :]
        kdec = kk * jnp.exp(b_last - b)
        state_ref[hd] = st * jnp.exp(b_last) + _mm_tn(vb, kdec.astype(bf))

        hgate = hgate_ref[0, rows, cols]
        y = _rms(o, g_out) * (hgate * _sigmoid(hgate))
        o_ref[0, rows, cols] = y.astype(o_ref.dtype)

    per_trip = 2 if n_chunks % 2 == 0 else 1

    def chunk(ci, carry):
        for sub in range(per_trip):
            rows = pl.ds(pl.multiple_of((ci * per_trip + sub) * C, C), C)
            for hd in range(n_heads):
                head_chunk(rows, hd)
        return carry

    lax.fori_loop(0, n_chunks // per_trip, chunk, 0)


def _hgrn(hg_in, lb_theta, g_hgrn_out, layer, sb):
    B, S, _ = hg_in.shape
    H = HGRN_HEADS
    W = H * HEAD_W
    spec = lambda off: pl.BlockSpec((1, sb, W), lambda b, i: (b, i, off))
    n_slots = lb_theta.shape[0]
    return pl.pallas_call(
        functools.partial(_hgrn_kernel, layer=layer, n_chunks=sb // HGRN_CHUNK),
        grid=(B, S // sb),
        in_specs=[spec(0), spec(1), spec(2), spec(3),
                  pl.BlockSpec((n_slots, W), lambda b, i: (0, 0)),
                  pl.BlockSpec((1, HEAD_W), lambda b, i: (0, 0))],
        out_specs=pl.BlockSpec((1, sb, W), lambda b, i: (b, i, 0)),
        out_shape=jax.ShapeDtypeStruct((B, S, W), jnp.bfloat16),
        scratch_shapes=[pltpu.VMEM((H, HEAD_W, HEAD_W), jnp.float32)],
        compiler_params=pltpu.CompilerParams(
            dimension_semantics=("parallel", "arbitrary"), vmem_limit_bytes=VMEM_LIMIT),
        name="hgrn2",
    )(hg_in, hg_in, hg_in, hg_in, lb_theta, g_hgrn_out.reshape(1, HEAD_W))


def _outproj_kernel(ao_ref, ho_ref, x_ref, wo_ref, gpost_ref, gt_ref, gpre_ref, sc_ref, sh_ref, wq_ref,
                    x1_ref, h2_ref, q_ref):
    n_a = ao_ref.shape[2]
    y = _mm(ao_ref[0], wo_ref[0:n_a, :]) + _mm(ho_ref[0], wo_ref[n_a:, :])
    x1 = x_ref[0] + gt_ref[0] * _rms(y, gpost_ref[...])
    x1_ref[0] = x1
    h2 = _rms(x1, gpre_ref[...]) * (1.0 + sc_ref[0]) + sh_ref[0]
    h2_ref[0] = h2
    q_ref[0] = _mm(h2.astype(jnp.bfloat16), wq_ref[...]).astype(q_ref.dtype)


def _out_projection(ao, ho, x, w_out_bf16, g_post, gt1, g_pre, sc2, sh2, w_pq_bf16, tm):
    B, S, D = x.shape
    n_a, n_h, n_q = ao.shape[2], ho.shape[2], w_pq_bf16.shape[1]
    vec = pl.BlockSpec((1, 1, D), lambda b, i: (b, 0, 0))
    par = pl.BlockSpec((1, D), lambda b, i: (0, 0))
    row = lambda n: pl.BlockSpec((1, tm, n), lambda b, i: (b, i, 0))
    return pl.pallas_call(
        _outproj_kernel,
        grid=(B, S // tm),
        in_specs=[row(n_a), row(n_h), row(D),
                  pl.BlockSpec((n_a + n_h, D), lambda b, i: (0, 0)),
                  par, vec, par, vec, vec,
                  pl.BlockSpec((D, n_q), lambda b, i: (0, 0))],
        out_specs=[row(D), row(D), row(n_q)],
        out_shape=[jax.ShapeDtypeStruct((B, S, D), jnp.float32),
                   jax.ShapeDtypeStruct((B, S, D), jnp.float32),
                   jax.ShapeDtypeStruct((B, S, n_q), jnp.bfloat16)],
        compiler_params=pltpu.CompilerParams(
            dimension_semantics=("parallel", "parallel"), vmem_limit_bytes=VMEM_LIMIT),
        name="outproj_norms_peerq",
    )(ao, ho, x, w_out_bf16, g_post.reshape(1, D), gt1, g_pre.reshape(1, D), sc2, sh2, w_pq_bf16)


def _pair_list():
    return [(a, b) for a in range(PEER_TOPK) for b in range(PEER_TOPK) if (a + 1) * (b + 1) <= PEER_TOPK]


def _topk_kernel(q_ref, keys_ref, idx_ref, gate_ref, v_scr, i_scr, cand_scr, cidx_scr, ts_scr, sel_scr):
    K = PEER_TOPK
    tb = q_ref.shape[0]
    pairs = _pair_list()
    n_cand = cand_scr.shape[0]
    kio = lax.broadcasted_iota(jnp.int32, (N_KEYS, tb), 0).astype(jnp.float32)
    pio = lax.broadcasted_iota(jnp.int32, (n_cand, tb), 0).astype(jnp.float32)

    for h in range(PEER_HEADS):
        for j in range(2):
            c0 = (h * 2 + j) * N_KEYS
            s = _mm_nt(keys_ref[h, j], q_ref[:, c0:c0 + N_KEYS])
            for r in range(K):
                m = jnp.max(s, axis=0, keepdims=True)
                am = jnp.min(jnp.where(s == m, kio, float(N_KEYS)), axis=0, keepdims=True)
                v_scr[j, r:r + 1, :] = m
                i_scr[j, r:r + 1, :] = am
                s = jnp.where(kio == am, _NEG_INF, s)
        v1, v2 = v_scr[0], v_scr[1]
        i1, i2 = i_scr[0], i_scr[1]
        cand_scr[...] = jnp.full(cand_scr.shape, _NEG_INF, jnp.float32)
        cidx_scr[...] = jnp.zeros(cidx_scr.shape, jnp.float32)
        off = 0
        for a in range(K):
            nb = sum(1 for (aa, _) in pairs if aa == a)
            cand_scr[off:off + nb, :] = v1[a:a + 1, :] + v2[0:nb, :]
            cidx_scr[off:off + nb, :] = i1[a:a + 1, :] * float(N_KEYS) + i2[0:nb, :]
            off += nb
        cand = cand_scr[...]
        cidx = cidx_scr[...]
        for r in range(K):
            m = jnp.max(cand, axis=0, keepdims=True)
            pos = jnp.min(jnp.where(cand == m, pio, float(n_cand)), axis=0, keepdims=True)
            hit = pio == pos
            ts_scr[r:r + 1, :] = m
            sel_scr[h * K + r:h * K + r + 1, :] = jnp.sum(jnp.where(hit, cidx, 0.0), axis=0, keepdims=True)
            cand = jnp.where(hit, _NEG_INF, cand)
        ts = ts_scr[...]
        e = jnp.exp(ts - ts[0:1, :])
        gate_ref[h * K:(h + 1) * K, :] = e / jnp.sum(e, axis=0, keepdims=True)
    idx_ref[...] = sel_scr[...].T.astype(jnp.int32)


def _peer_topk(q, sub_keys_bf16, tb):
    T = q.shape[0]
    n_sel = PEER_HEADS * PEER_TOPK
    n_cand = -(-len(_pair_list()) // 8) * 8
    return pl.pallas_call(
        _topk_kernel,
        grid=(T // tb,),
        in_specs=[pl.BlockSpec((tb, q.shape[1]), lambda i: (i, 0)),
                  pl.BlockSpec(sub_keys_bf16.shape, lambda i: (0, 0, 0, 0))],
        out_specs=[pl.BlockSpec((tb, n_sel), lambda i: (i, 0)),
                   pl.BlockSpec((n_sel, tb), lambda i: (0, i))],
        out_shape=[jax.ShapeDtypeStruct((T, n_sel), jnp.int32),
                   jax.ShapeDtypeStruct((n_sel, T), jnp.float32)],
        scratch_shapes=[pltpu.VMEM((2, PEER_TOPK, tb), jnp.float32),
                        pltpu.VMEM((2, PEER_TOPK, tb), jnp.float32),
                        pltpu.VMEM((n_cand, tb), jnp.float32),
                        pltpu.VMEM((n_cand, tb), jnp.float32),
                        pltpu.VMEM((PEER_TOPK, tb), jnp.float32),
                        pltpu.VMEM((n_sel, tb), jnp.float32)],
        compiler_params=pltpu.CompilerParams(
            dimension_semantics=("parallel",), vmem_limit_bytes=VMEM_LIMIT),
        name="peer_topk",
    )(q, sub_keys_bf16)


def _peer_kernel(idx_ref, h_ref, gate_ref, x1_ref, gt_ref, g_ref, uv_ref, o_ref, buf, sem, y_scr, *, n_slots):
    tb, D = h_ref.shape
    n_sel = idx_ref.shape[1]
    n_c = D // HEAD_W
    n_t = 2 * n_c
    pitch = buf.shape[1] // n_sel

    def row_copy(t, k, slot):
        e = idx_ref[t, k]
        return pltpu.make_async_copy(uv_ref.at[e], buf.at[slot, pl.ds(k * pitch, n_t), :], sem.at[slot])

    def tile(slot, c):
        return buf[slot, pl.ds(c, n_sel, stride=pitch), :]

    def start_rows(t, slot, k0, k1):
        for k in range(k0, k1):
            row_copy(t, k, slot).start(priority=k % 2)

    def wait_slot(slot):
        done = buf.at[slot, pl.ds(0, n_sel * n_t), :]
        pltpu.make_async_copy(done, done, sem.at[slot]).wait()

    n_ahead = n_slots - 1
    for t in range(n_ahead):
        start_rows(t, t, 0, n_sel)

    lane = lax.broadcasted_iota(jnp.int32, gate_ref.shape, 1)
    per_piece = n_sel // n_t

    def token(t, slot, prefetch):
        def issue(piece):
            if prefetch:
                start_rows(t + n_ahead, (slot + n_ahead) % n_slots, piece * per_piece, (piece + 1) * per_piece)

        wait_slot(slot)
        h = h_ref[pl.ds(t, 1), :]
        p = None
        for c in range(n_c):
            pc = tile(slot, c) * h[:, c * HEAD_W:(c + 1) * HEAD_W]
            p = pc if p is None else p + pc
            issue(c)
        a = jnp.sum(p, axis=1, keepdims=True)
        gcol = jnp.sum(jnp.where(lane == t, gate_ref[...], 0.0), axis=1, keepdims=True)
        w = gcol * (0.5 * a * (1.0 + lax.erf(a * (1.0 / math.sqrt(2.0)))))
        y = []
        for c in range(n_c):
            y.append(jnp.sum(w * tile(slot, n_c + c), axis=0, keepdims=True))
            issue(n_c + c)
        y_scr[pl.ds(t, 1), :] = jnp.concatenate(y, axis=1)

    n_main = tb - n_ahead
    n_groups = n_main // n_slots

    def group(g, carry):
        for j in range(n_slots):
            token(g * n_slots + j, j, True)
        return carry

    lax.fori_loop(0, n_groups, group, 0)
    for t in range(n_groups * n_slots, tb):
        token(t, t % n_slots, t < n_main)
    o_ref[...] = x1_ref[...] + gt_ref[0] * _rms(y_scr[...], g_ref[...])


def _pack_experts(u, v):
    E, D = u.shape
    return jnp.concatenate([u, v], axis=1).reshape(E, 2 * D // HEAD_W, HEAD_W)


def _peer_mix(idx, h2, gate_t, x1, gt2, g_post, uv, seq_len, tb, n_slots):
    T, D = h2.shape
    n_sel = idx.shape[1]
    blocks_per_seq = seq_len // tb
    rows = pl.BlockSpec((tb, D), lambda i: (i, 0))
    return pl.pallas_call(
        functools.partial(_peer_kernel, n_slots=n_slots),
        grid=(T // tb,),
        in_specs=[pl.BlockSpec((tb, n_sel), lambda i: (i, 0), memory_space=pltpu.SMEM),
                  rows,
                  pl.BlockSpec((n_sel, tb), lambda i: (0, i)),
                  rows,
                  pl.BlockSpec((1, 1, D), lambda i: (i // blocks_per_seq, 0, 0)),
                  pl.BlockSpec((1, D), lambda i: (0, 0)),
                  pl.BlockSpec(memory_space=pl.ANY)],
        out_specs=rows,
        out_shape=jax.ShapeDtypeStruct((T, D), jnp.float32),
        scratch_shapes=[pltpu.VMEM((n_slots, n_sel * (2 * D // HEAD_W + 1), HEAD_W), uv.dtype),
                        pltpu.SemaphoreType.DMA((n_slots,)),
                        pltpu.VMEM((tb, D), jnp.float32)],
        compiler_params=pltpu.CompilerParams(
            dimension_semantics=("arbitrary",), vmem_limit_bytes=VMEM_LIMIT),
        name="peer_gather_mix",
    )(idx, h2, gate_t, x1, gt2, g_post.reshape(1, D), uv)


def _pick(n, pref):
    t = min(n, pref)
    assert n % t == 0, (n, pref)
    return t


def kernel(x, c, w_ada, b_ada, g_pre_mix, g_post_mix, g_pre_ffn, g_post_ffn, w_in, lam_qk, g_diff_sub,
           lb_theta, g_hgrn_out, w_out, w_pq, sub_keys, expert_u, expert_v):
    B, S, D = x.shape
    depth = w_in.shape[0]
    n_attn = 3 * DIFF_HEADS * HEAD_W
    tm = _pick(S, 512)
    tq = _pick(S, 512)
    sb = _pick(S, 512)
    tb = _pick(S, 128)
    bf = jnp.bfloat16
    for l in range(depth):
        lam_init = 0.8 - 0.6 * math.exp(-0.3 * l)
        mod = _modulation(c, w_ada[l], b_ada[l]).reshape(N_MOD, B, 1, D)
        sh1, sc1, gt1, sh2, sc2, gt2 = (mod[i] for i in range(N_MOD))
        attn_in, hg_in = _in_projection(x, g_pre_mix[l], sc1, sh1, w_in[l].astype(bf), n_attn, tm)
        ao = _diff_attention(attn_in, lam_qk[l], g_diff_sub[l], lam_init, tq)
        ho = _hgrn(hg_in, lb_theta, g_hgrn_out[l], l, sb)
        x1, h2, q = _out_projection(ao, ho, x, w_out[l].astype(bf), g_post_mix[l], gt1, g_pre_ffn[l],
                                    sc2, sh2, w_pq[l].astype(bf), tm)
        idx, gate_t = _peer_topk(q.reshape(B * S, -1), sub_keys[l].astype(bf), tb)
        uv = _pack_experts(expert_u[l], expert_v[l])
        out = _peer_mix(idx, h2.reshape(B * S, D), gate_t, x1.reshape(B * S, D), gt2, g_post_ffn[l], uv,
                        S, _pick(S, 256), n_slots=16)
        x = out.reshape(B, S, D)
    return x
```

```python
import functools
import math

import jax
import jax.numpy as jnp
from jax import lax
from jax.experimental import pallas as pl
from jax.experimental.pallas import tpu as pltpu

EPS = 1e-6
N_MOD = 6
DIFF_HEADS = 4
DIFF_QK = 64
HEAD_W = 128
HGRN_HEADS = 4
HGRN_CHUNK = 64
HGRN_SUB = 16
PEER_HEADS = 8
N_KEYS = 128
PEER_TOPK = 16
VMEM_LIMIT = 56 * 1024 * 1024

_HI = lax.Precision.HIGHEST
_NEG_INF = float("-inf")
LOG2E = 1.4426950408889634


def _dot(a, b, dims, precision=None):
    return lax.dot_general(a, b, (dims, ((), ())), precision=precision,
                           preferred_element_type=jnp.float32)


def _mm(a, b, precision=None):
    return _dot(a, b, ((1,), (0,)), precision)


def _mm_nt(a, b, precision=None):
    return _dot(a, b, ((1,), (1,)), precision)


def _mm_tn(a, b, precision=None):
    return _dot(a, b, ((0,), (0,)), precision)


def _rms(x, g):
    return x * lax.rsqrt(jnp.mean(x * x, axis=-1, keepdims=True) + EPS) * g


def _sigmoid(x):
    return 1.0 / (1.0 + jnp.exp(-x))


def _mod_kernel(c_ref, w_ref, b_ref, o_ref):
    c = c_ref[...]
    ca = c * _sigmoid(c)
    o_ref[0] = _mm(ca, w_ref[...], _HI) + b_ref[...]


def _modulation(c, w_ada, b_ada):
    B, D = c.shape
    return pl.pallas_call(
        _mod_kernel,
        grid=(N_MOD,),
        in_specs=[pl.BlockSpec((B, D), lambda j: (0, 0)),
                  pl.BlockSpec((D, D), lambda j: (0, j)),
                  pl.BlockSpec((1, D), lambda j: (0, j))],
        out_specs=pl.BlockSpec((1, B, D), lambda j: (j, 0, 0)),
        out_shape=jax.ShapeDtypeStruct((N_MOD, B, D), jnp.float32),
        compiler_params=pltpu.CompilerParams(vmem_limit_bytes=VMEM_LIMIT),
        name="adaln_mod",
    )(c, w_ada, b_ada.reshape(1, N_MOD * D))


def _inproj_kernel(x_ref, g_ref, sc_ref, sh_ref, w_ref, attn_ref, hg_ref, *, n_attn, col_chunk):
    x = x_ref[0]
    h = _rms(x, g_ref[...]) * (1.0 + sc_ref[0]) + sh_ref[0]
    hb = h.astype(jnp.bfloat16)
    n_cols = w_ref.shape[1]
    for c0 in range(0, n_cols, col_chunk):
        r = _mm(hb, w_ref[:, c0:c0 + col_chunk])
        if c0 < n_attn:
            attn_ref[0, :, c0:c0 + col_chunk] = r.astype(attn_ref.dtype)
        else:
            hg_ref[0, :, c0 - n_attn:c0 - n_attn + col_chunk] = r


def _in_projection(x, g, sc, sh, w_in_bf16, n_attn, tm):
    B, S, D = x.shape
    n_cols = w_in_bf16.shape[1]
    vec = pl.BlockSpec((1, 1, D), lambda b, i: (b, 0, 0))
    return pl.pallas_call(
        functools.partial(_inproj_kernel, n_attn=n_attn, col_chunk=512),
        grid=(B, S // tm),
        in_specs=[pl.BlockSpec((1, tm, D), lambda b, i: (b, i, 0)),
                  pl.BlockSpec((1, D), lambda b, i: (0, 0)),
                  vec, vec,
                  pl.BlockSpec((D, n_cols), lambda b, i: (0, 0))],
        out_specs=[pl.BlockSpec((1, tm, n_attn), lambda b, i: (b, i, 0)),
                   pl.BlockSpec((1, tm, n_cols - n_attn), lambda b, i: (b, i, 0))],
        out_shape=[jax.ShapeDtypeStruct((B, S, n_attn), jnp.bfloat16),
                   jax.ShapeDtypeStruct((B, S, n_cols - n_attn), jnp.float32)],
        compiler_params=pltpu.CompilerParams(
            dimension_semantics=("parallel", "parallel"), vmem_limit_bytes=VMEM_LIMIT),
        name="prenorm_inproj",
    )(x, g.reshape(1, D), sc, sh, w_in_bf16)


def _fold_lanes(x, op):
    out = x[:, 0:HEAD_W]
    for i in range(1, x.shape[1] // HEAD_W):
        out = op(out, x[:, i * HEAD_W:(i + 1) * HEAD_W])
    return out


def _attn_kernel(q_ref, k_ref, v_ref, lam_ref, g_ref, o_ref, s_scr, mx_scr, l_scr, acc_scr, *, tq, lam_init):
    qi = pl.program_id(2)
    q = q_ref[0]
    lane = lax.broadcasted_iota(jnp.int32, q.shape, 1)
    qs = q * jnp.asarray(1.0 / math.sqrt(DIFF_QK), q.dtype)
    zero = jnp.zeros_like(qs)
    q2 = jnp.concatenate([jnp.where(lane < DIFF_QK, qs, zero), jnp.where(lane >= DIFF_QK, qs, zero)], axis=0)
    n_rep = tq // HEAD_W

    mx_scr[...] = jnp.full(mx_scr.shape, _NEG_INF, jnp.float32)

    def scores(j, carry):
        s = _mm_nt(q2, k_ref[0, pl.ds(pl.multiple_of(j * tq, tq), tq), :]) * LOG2E
        s_scr[j] = s
        mx_scr[...] = jnp.maximum(mx_scr[...], _fold_lanes(s, jnp.maximum))
        return carry

    lax.fori_loop(0, qi, scores, 0)
    s = _mm_nt(q2, k_ref[0, pl.ds(pl.multiple_of(qi * tq, tq), tq), :]) * LOG2E
    row = lax.broadcasted_iota(jnp.int32, s.shape, 0) & (tq - 1)
    col = lax.broadcasted_iota(jnp.int32, s.shape, 1)
    s = jnp.where(col <= row, s, _NEG_INF)
    s_scr[qi] = s
    m = jnp.max(jnp.maximum(mx_scr[...], _fold_lanes(s, jnp.maximum)), axis=1, keepdims=True)
    mx_scr[...] = jnp.broadcast_to(m, mx_scr.shape)
    l_scr[...] = jnp.zeros(l_scr.shape, jnp.float32)
    acc_scr[...] = jnp.zeros(acc_scr.shape, jnp.float32)

    def contract(j, carry):
        mb = mx_scr[...]
        p = jnp.exp2(s_scr[j] - jnp.concatenate([mb] * n_rep, axis=1))
        l_scr[...] += _fold_lanes(p, jnp.add)
        vb = v_ref[0, pl.ds(pl.multiple_of(j * tq, tq), tq), :]
        acc_scr[...] += _mm(p.astype(vb.dtype), vb)
        return carry

    lax.fori_loop(0, qi + 1, contract, 0)

    lq = lam_ref[...]
    lam = (jnp.exp(jnp.sum(lq[0:1] * lq[1:2], axis=1, keepdims=True))
           - jnp.exp(jnp.sum(lq[2:3] * lq[3:4], axis=1, keepdims=True)) + lam_init)
    o = acc_scr[...] / jnp.sum(l_scr[...], axis=1, keepdims=True)
    o = o[0:tq] - lam * o[tq:2 * tq]
    o = _rms(o, g_ref[...]) * (1.0 - lam_init)
    o_ref[0] = o.astype(o_ref.dtype)


def _diff_attention(attn_in, lam_qk, g_diff_sub, lam_init, tq):
    B, S, _ = attn_in.shape
    H = DIFF_HEADS
    assert tq % HEAD_W == 0 and tq & (tq - 1) == 0, tq
    kv_spec = lambda off: pl.BlockSpec((1, S, HEAD_W), lambda b, h, i: (b, 0, off + h))
    return pl.pallas_call(
        functools.partial(_attn_kernel, tq=tq, lam_init=lam_init),
        grid=(B, H, S // tq),
        in_specs=[pl.BlockSpec((1, tq, HEAD_W), lambda b, h, i: (b, i, h)),
                  kv_spec(H), kv_spec(2 * H),
                  pl.BlockSpec(lam_qk.shape, lambda b, h, i: (0, 0)),
                  pl.BlockSpec((1, HEAD_W), lambda b, h, i: (0, 0))],
        out_specs=pl.BlockSpec((1, tq, HEAD_W), lambda b, h, i: (b, i, h)),
        out_shape=jax.ShapeDtypeStruct((B, S, H * HEAD_W), jnp.bfloat16),
        scratch_shapes=[pltpu.VMEM((S // tq, 2 * tq, tq), jnp.float32),
                        pltpu.VMEM((2 * tq, HEAD_W), jnp.float32),
                        pltpu.VMEM((2 * tq, HEAD_W), jnp.float32),
                        pltpu.VMEM((2 * tq, HEAD_W), jnp.float32)],
        compiler_params=pltpu.CompilerParams(
            dimension_semantics=("parallel", "parallel", "arbitrary"), vmem_limit_bytes=VMEM_LIMIT),
        name="diff_attention",
    )(attn_in, attn_in, attn_in, lam_qk, g_diff_sub.reshape(1, HEAD_W))


def _hgrn_kernel(hq_ref, hf_ref, hi_ref, hgate_ref, lbt_ref, g_ref, o_ref, state_ref, *, layer, n_chunks):
    C, SUB = HGRN_CHUNK, HGRN_SUB
    n_sub = C // SUB
    n_heads = state_ref.shape[0]

    @pl.when(pl.program_id(1) == 0)
    def _():
        state_ref[...] = jnp.zeros(state_ref.shape, jnp.float32)

    th = lbt_ref[...]
    e = jnp.exp(th - jnp.max(th, axis=0, keepdims=True))
    lb_all = jnp.sum(e[0:layer + 1], axis=0, keepdims=True) / jnp.sum(e, axis=0, keepdims=True)

    r_io = lax.broadcasted_iota(jnp.int32, (C, C), 0)
    c_io = lax.broadcasted_iota(jnp.int32, (C, C), 1)
    tril = (c_io <= r_io).astype(jnp.float32)
    tloc = lax.broadcasted_iota(jnp.int32, (C, HEAD_W), 0) % SUB
    sub_col = lax.broadcasted_iota(jnp.int32, (SUB, C), 1)
    g_out = g_ref[...]

    def group_rows(x, s):
        return jnp.concatenate(
            [jnp.broadcast_to(x[i * SUB + s:i * SUB + s + 1, :], (SUB, HEAD_W)) for i in range(n_sub)], axis=0)

    def head_chunk(rows, hd):
        cols = slice(hd * HEAD_W, (hd + 1) * HEAD_W)
        lb = lb_all[:, cols]
        hq = hq_ref[0, rows, cols]
        f = lb + (1.0 - lb) * _sigmoid(hf_ref[0, rows, cols])
        glog = jnp.log(f)
        kk = 1.0 - f
        q = hq * _sigmoid(hq)
        v = hi_ref[0, rows, cols]
        b = _mm(tril, glog, _HI)
        st = state_ref[hd]

        bf = jnp.bfloat16
        vb = v.astype(bf)
        o = _mm_nt((q * jnp.exp(b)).astype(bf), st.astype(bf))

        o_sub = [jnp.zeros((SUB, HEAD_W), jnp.float32)]
        for i in range(1, n_sub):
            beta = b[i * SUB:i * SUB + 1, :]
            qt = q[i * SUB:(i + 1) * SUB, :] * jnp.exp(b[i * SUB:(i + 1) * SUB, :] - beta)
            kt = kk * jnp.exp(jnp.minimum(beta - b, 0.0))
            p = _mm_nt(qt.astype(bf), kt.astype(bf))
            p = jnp.where(sub_col < i * SUB, p, 0.0)
            o_sub.append(_mm(p.astype(bf), vb))
        o = o + jnp.concatenate(o_sub, axis=0)

        for s in range(SUB):
            b_s, k_s, v_s = group_rows(b, s), group_rows(kk, s), group_rows(v, s)
            w = q * k_s * jnp.exp(jnp.where(tloc >= s, b - b_s, _NEG_INF))
            o = o + jnp.sum(w, axis=1, keepdims=True) * v_s

        b_last = b[C - 1:C, :]
        kdec = kk * jnp.exp(b_last - b)
        state_ref[hd] = st * jnp.exp(b_last) + _mm_tn(vb, kdec.astype(bf))

        hgate = hgate_ref[0, rows, cols]
        y = _rms(o, g_out) * (hgate * _sigmoid(hgate))
        o_ref[0, rows, cols] = y.astype(o_ref.dtype)

    per_trip = 2 if n_chunks % 2 == 0 else 1

    def chunk(ci, carry):
        for sub in range(per_trip):
            rows = pl.ds(pl.multiple_of((ci * per_trip + sub) * C, C), C)
            for hd in range(n_heads):
                head_chunk(rows, hd)
        return carry

    lax.fori_loop(0, n_chunks // per_trip, chunk, 0)


def _hgrn(hg_in, lb_theta, g_hgrn_out, layer, sb):
    B, S, _ = hg_in.shape
    H = HGRN_HEADS
    W = H * HEAD_W
    spec = lambda off: pl.BlockSpec((1, sb, W), lambda b, i: (b, i, off))
    n_slots = lb_theta.shape[0]
    return pl.pallas_call(
        functools.partial(_hgrn_kernel, layer=layer, n_chunks=sb // HGRN_CHUNK),
        grid=(B, S // sb),
        in_specs=[spec(0), spec(1), spec(2), spec(3),
                  pl.BlockSpec((n_slots, W), lambda b, i: (0, 0)),
                  pl.BlockSpec((1, HEAD_W), lambda b, i: (0, 0))],
        out_specs=pl.BlockSpec((1, sb, W), lambda b, i: (b, i, 0)),
        out_shape=jax.ShapeDtypeStruct((B, S, W), jnp.bfloat16),
        scratch_shapes=[pltpu.VMEM((H, HEAD_W, HEAD_W), jnp.float32)],
        compiler_params=pltpu.CompilerParams(
            dimension_semantics=("parallel", "arbitrary"), vmem_limit_bytes=VMEM_LIMIT),
        name="hgrn2",
    )(hg_in, hg_in, hg_in, hg_in, lb_theta, g_hgrn_out.reshape(1, HEAD_W))


def _outproj_kernel(ao_ref, ho_ref, x_ref, wo_ref, gpost_ref, gt_ref, gpre_ref, sc_ref, sh_ref, wq_ref,
                    x1_ref, h2_ref, q_ref):
    n_a = ao_ref.shape[2]
    y = _mm(ao_ref[0], wo_ref[0:n_a, :]) + _mm(ho_ref[0], wo_ref[n_a:, :])
    x1 = x_ref[0] + gt_ref[0] * _rms(y, gpost_ref[...])
    x1_ref[0] = x1
    h2 = _rms(x1, gpre_ref[...]) * (1.0 + sc_ref[0]) + sh_ref[0]
    h2_ref[0] = h2
    q_ref[0] = _mm(h2.astype(jnp.bfloat16), wq_ref[...]).astype(q_ref.dtype)


def _out_projection(ao, ho, x, w_out_bf16, g_post, gt1, g_pre, sc2, sh2, w_pq_bf16, tm):
    B, S, D = x.shape
    n_a, n_h, n_q = ao.shape[2], ho.shape[2], w_pq_bf16.shape[1]
    vec = pl.BlockSpec((1, 1, D), lambda b, i: (b, 0, 0))
    par = pl.BlockSpec((1, D), lambda b, i: (0, 0))
    row = lambda n: pl.BlockSpec((1, tm, n), lambda b, i: (b, i, 0))
    return pl.pallas_call(
        _outproj_kernel,
        grid=(B, S // tm),
        in_specs=[row(n_a), row(n_h), row(D),
                  pl.BlockSpec((n_a + n_h, D), lambda b, i: (0, 0)),
                  par, vec, par, vec, vec,
                  pl.BlockSpec((D, n_q), lambda b, i: (0, 0))],
        out_specs=[row(D), row(D), row(n_q)],
        out_shape=[jax.ShapeDtypeStruct((B, S, D), jnp.float32),
                   jax.ShapeDtypeStruct((B, S, D), jnp.float32),
                   jax.ShapeDtypeStruct((B, S, n_q), jnp.bfloat16)],
        compiler_params=pltpu.CompilerParams(
            dimension_semantics=("parallel", "parallel"), vmem_limit_bytes=VMEM_LIMIT),
        name="outproj_norms_peerq",
    )(ao, ho, x, w_out_bf16, g_post.reshape(1, D), gt1, g_pre.reshape(1, D), sc2, sh2, w_pq_bf16)


def _pair_list():
    return [(a, b) for a in range(PEER_TOPK) for b in range(PEER_TOPK) if (a + 1) * (b + 1) <= PEER_TOPK]


def _topk_kernel(q_ref, keys_ref, idx_ref, gate_ref, v_scr, i_scr, cand_scr, cidx_scr, ts_scr, sel_scr):
    K = PEER_TOPK
    tb = q_ref.shape[0]
    pairs = _pair_list()
    n_cand = cand_scr.shape[0]
    kio = lax.broadcasted_iota(jnp.int32, (N_KEYS, tb), 0).astype(jnp.float32)
    pio = lax.broadcasted_iota(jnp.int32, (n_cand, tb), 0).astype(jnp.float32)

    for h in range(PEER_HEADS):
        for j in range(2):
            c0 = (h * 2 + j) * N_KEYS
            s = _mm_nt(keys_ref[h, j], q_ref[:, c0:c0 + N_KEYS])
            for r in range(K):
                m = jnp.max(s, axis=0, keepdims=True)
                am = jnp.min(jnp.where(s == m, kio, float(N_KEYS)), axis=0, keepdims=True)
                v_scr[j, r:r + 1, :] = m
                i_scr[j, r:r + 1, :] = am
                s = jnp.where(kio == am, _NEG_INF, s)
        v1, v2 = v_scr[0], v_scr[1]
        i1, i2 = i_scr[0], i_scr[1]
        cand_scr[...] = jnp.full(cand_scr.shape, _NEG_INF, jnp.float32)
        cidx_scr[...] = jnp.zeros(cidx_scr.shape, jnp.float32)
        off = 0
        for a in range(K):
            nb = sum(1 for (aa, _) in pairs if aa == a)
            cand_scr[off:off + nb, :] = v1[a:a + 1, :] + v2[0:nb, :]
            cidx_scr[off:off + nb, :] = i1[a:a + 1, :] * float(N_KEYS) + i2[0:nb, :]
            off += nb
        cand = cand_scr[...]
        cidx = cidx_scr[...]
        for r in range(K):
            m = jnp.max(cand, axis=0, keepdims=True)
            pos = jnp.min(jnp.where(cand == m, pio, float(n_cand)), axis=0, keepdims=True)
            hit = pio == pos
            ts_scr[r:r + 1, :] = m
            sel_scr[h * K + r:h * K + r + 1, :] = jnp.sum(jnp.where(hit, cidx, 0.0), axis=0, keepdims=True)
            cand = jnp.where(hit, _NEG_INF, cand)
        ts = ts_scr[...]
        e = jnp.exp(ts - ts[0:1, :])
        gate_ref[h * K:(h + 1) * K, :] = e / jnp.sum(e, axis=0, keepdims=True)
    idx_ref[...] = sel_scr[...].T.astype(jnp.int32)


def _peer_topk(q, sub_keys_bf16, tb):
    T = q.shape[0]
    n_sel = PEER_HEADS * PEER_TOPK
    n_cand = -(-len(_pair_list()) // 8) * 8
    return pl.pallas_call(
        _topk_kernel,
        grid=(T // tb,),
        in_specs=[pl.BlockSpec((tb, q.shape[1]), lambda i: (i, 0)),
                  pl.BlockSpec(sub_keys_bf16.shape, lambda i: (0, 0, 0, 0))],
        out_specs=[pl.BlockSpec((tb, n_sel), lambda i: (i, 0)),
                   pl.BlockSpec((n_sel, tb), lambda i: (0, i))],
        out_shape=[jax.ShapeDtypeStruct((T, n_sel), jnp.int32),
                   jax.ShapeDtypeStruct((n_sel, T), jnp.float32)],
        scratch_shapes=[pltpu.VMEM((2, PEER_TOPK, tb), jnp.float32),
                        pltpu.VMEM((2, PEER_TOPK, tb), jnp.float32),
                        pltpu.VMEM((n_cand, tb), jnp.float32),
                        pltpu.VMEM((n_cand, tb), jnp.float32),
                        pltpu.VMEM((PEER_TOPK, tb), jnp.float32),
                        pltpu.VMEM((n_sel, tb), jnp.float32)],
        compiler_params=pltpu.CompilerParams(
            dimension_semantics=("parallel",), vmem_limit_bytes=VMEM_LIMIT),
        name="peer_topk",
    )(q, sub_keys_bf16)


def _peer_kernel(idx_ref, h_ref, gate_ref, x1_ref, gt_ref, g_ref, uv_ref, o_ref, buf, sem, y_scr, *, n_slots):
    tb, D = h_ref.shape
    n_sel = idx_ref.shape[1]
    n_c = D // HEAD_W
    n_t = 2 * n_c
    pitch = buf.shape[1] // n_sel

    def row_copy(t, k, slot):
        e = idx_ref[t, k]
        return pltpu.make_async_copy(uv_ref.at[e], buf.at[slot, pl.ds(k * pitch, n_t), :], sem.at[slot])

    def tile(slot, c):
        return buf[slot, pl.ds(c, n_sel, stride=pitch), :]

    def start_rows(t, slot, k0, k1):
        for k in range(k0, k1):
            row_copy(t, k, slot).start(priority=k % 2)

    def wait_slot(slot):
        done = buf.at[slot, pl.ds(0, n_sel * n_t), :]
        pltpu.make_async_copy(done, done, sem.at[slot]).wait()

    n_ahead = n_slots - 1
    for t in range(n_ahead):
        start_rows(t, t, 0, n_sel)

    lane = lax.broadcasted_iota(jnp.int32, gate_ref.shape, 1)
    per_piece = n_sel // n_t

    def token(t, slot, prefetch):
        def issue(piece):
            if prefetch:
                start_rows(t + n_ahead, (slot + n_ahead) % n_slots, piece * per_piece, (piece + 1) * per_piece)

        wait_slot(slot)
        h = h_ref[pl.ds(t, 1), :]
        p = None
        for c in range(n_c):
            pc = tile(slot, c) * h[:, c * HEAD_W:(c + 1) * HEAD_W]
            p = pc if p is None else p + pc
            issue(c)
        a = jnp.sum(p, axis=1, keepdims=True)
        gcol = jnp.sum(jnp.where(lane == t, gate_ref[...], 0.0), axis=1, keepdims=True)
        w = gcol * (0.5 * a * (1.0 + lax.erf(a * (1.0 / math.sqrt(2.0)))))
        y = []
        for c in range(n_c):
            y.append(jnp.sum(w * tile(slot, n_c + c), axis=0, keepdims=True))
            issue(n_c + c)
        y_scr[pl.ds(t, 1), :] = jnp.concatenate(y, axis=1)

    n_main = tb - n_ahead
    n_groups = n_main // n_slots

    def group(g, carry):
        for j in range(n_slots):
            token(g * n_slots + j, j, True)
        return carry

    lax.fori_loop(0, n_groups, group, 0)
    for t in range(n_groups * n_slots, tb):
        token(t, t % n_slots, t < n_main)
    o_ref[...] = x1_ref[...] + gt_ref[0] * _rms(y_scr[...], g_ref[...])


def _pack_experts(u, v):
    E, D = u.shape
    return jnp.concatenate([u, v], axis=1).reshape(E, 2 * D // HEAD_W, HEAD_W)


def _peer_mix(idx, h2, gate_t, x1, gt2, g_post, uv, seq_len, tb, n_slots):
    T, D = h2.shape
    n_sel = idx.shape[1]
    blocks_per_seq = seq_len // tb
    rows = pl.BlockSpec((tb, D), lambda i: (i, 0))
    return pl.pallas_call(
        functools.partial(_peer_kernel, n_slots=n_slots),
        grid=(T // tb,),
        in_specs=[pl.BlockSpec((tb, n_sel), lambda i: (i, 0), memory_space=pltpu.SMEM),
                  rows,
                  pl.BlockSpec((n_sel, tb), lambda i: (0, i)),
                  rows,
                  pl.BlockSpec((1, 1, D), lambda i: (i // blocks_per_seq, 0, 0)),
                  pl.BlockSpec((1, D), lambda i: (0, 0)),
                  pl.BlockSpec(memory_space=pl.ANY)],
        out_specs=rows,
        out_shape=jax.ShapeDtypeStruct((T, D), jnp.float32),
        scratch_shapes=[pltpu.VMEM((n_slots, n_sel * (2 * D // HEAD_W + 1), HEAD_W), uv.dtype),
                        pltpu.SemaphoreType.DMA((n_slots,)),
                        pltpu.VMEM((tb, D), jnp.float32)],
        compiler_params=pltpu.CompilerParams(
            dimension_semantics=("arbitrary",), vmem_limit_bytes=VMEM_LIMIT),
        name="peer_gather_mix",
    )(idx, h2, gate_t, x1, gt2, g_post.reshape(1, D), uv)


def _pick(n, pref):
    t = min(n, pref)
    assert n % t == 0, (n, pref)
    return t


def kernel(x, c, w_ada, b_ada, g_pre_mix, g_post_mix, g_pre_ffn, g_post_ffn, w_in, lam_qk, g_diff_sub,
           lb_theta, g_hgrn_out, w_out, w_pq, sub_keys, expert_u, expert_v):
    B, S, D = x.shape
    depth = w_in.shape[0]
    n_attn = 3 * DIFF_HEADS * HEAD_W
    tm = _pick(S, 512)
    tq = _pick(S, 512)
    sb = _pick(S, 512)
    tb = _pick(S, 128)
    bf = jnp.bfloat16
    for l in range(depth):
        lam_init = 0.8 - 0.6 * math.exp(-0.3 * l)
        mod = _modulation(c, w_ada[l], b_ada[l]).reshape(N_MOD, B, 1, D)
        sh1, sc1, gt1, sh2, sc2, gt2 = (mod[i] for i in range(N_MOD))
        attn_in, hg_in = _in_projection(x, g_pre_mix[l], sc1, sh1, w_in[l].astype(bf), n_attn, tm)
        ao = _diff_attention(attn_in, lam_qk[l], g_diff_sub[l], lam_init, tq)
        ho = _hgrn(hg_in, lb_theta, g_hgrn_out[l], l, sb)
        x1, h2, q = _out_projection(ao, ho, x, w_out[l].astype(bf), g_post_mix[l], gt1, g_pre_ffn[l],
                                    sc2, sh2, w_pq[l].astype(bf), tm)
        idx, gate_t = _peer_topk(q.reshape(B * S, -1), sub_keys[l].astype(bf), tb)
        uv = _pack_experts(expert_u[l], expert_v[l])
        out = _peer_mix(idx, h2.reshape(B * S, D), gate_t, x1.reshape(B * S, D), gt2, g_post_ffn[l], uv,
                        S, _pick(S, 512), n_slots=10)
        x = out.reshape(B, S, D)
    return x
```

```python
import functools
import math

import jax
import jax.numpy as jnp
from jax import lax
from jax.experimental import pallas as pl
from jax.experimental.pallas import tpu as pltpu

EPS = 1e-6
N_MOD = 6
DIFF_HEADS = 4
DIFF_QK = 64
HEAD_W = 128
HGRN_HEADS = 4
HGRN_CHUNK = 64
HGRN_SUB = 16
PEER_HEADS = 8
N_KEYS = 128
PEER_TOPK = 16
VMEM_LIMIT = 56 * 1024 * 1024

_HI = lax.Precision.HIGHEST
_NEG_INF = float("-inf")
LOG2E = 1.4426950408889634


def _dot(a, b, dims, precision=None):
    return lax.dot_general(a, b, (dims, ((), ())), precision=precision,
                           preferred_element_type=jnp.float32)


def _mm(a, b, precision=None):
    return _dot(a, b, ((1,), (0,)), precision)


def _mm_nt(a, b, precision=None):
    return _dot(a, b, ((1,), (1,)), precision)


def _mm_tn(a, b, precision=None):
    return _dot(a, b, ((0,), (0,)), precision)


def _rms(x, g):
    return x * lax.rsqrt(jnp.mean(x * x, axis=-1, keepdims=True) + EPS) * g


def _sigmoid(x):
    return 1.0 / (1.0 + jnp.exp(-x))


def _mod_kernel(c_ref, w_ref, b_ref, o_ref):
    c = c_ref[...]
    ca = c * _sigmoid(c)
    o_ref[0] = _mm(ca, w_ref[...], _HI) + b_ref[...]


def _modulation(c, w_ada, b_ada):
    B, D = c.shape
    return pl.pallas_call(
        _mod_kernel,
        grid=(N_MOD,),
        in_specs=[pl.BlockSpec((B, D), lambda j: (0, 0)),
                  pl.BlockSpec((D, D), lambda j: (0, j)),
                  pl.BlockSpec((1, D), lambda j: (0, j))],
        out_specs=pl.BlockSpec((1, B, D), lambda j: (j, 0, 0)),
        out_shape=jax.ShapeDtypeStruct((N_MOD, B, D), jnp.float32),
        compiler_params=pltpu.CompilerParams(vmem_limit_bytes=VMEM_LIMIT),
        name="adaln_mod",
    )(c, w_ada, b_ada.reshape(1, N_MOD * D))


def _inproj_kernel(x_ref, g_ref, sc_ref, sh_ref, w_ref, attn_ref, hg_ref, *, n_attn, col_chunk):
    x = x_ref[0]
    h = _rms(x, g_ref[...]) * (1.0 + sc_ref[0]) + sh_ref[0]
    hb = h.astype(jnp.bfloat16)
    n_cols = w_ref.shape[1]
    for c0 in range(0, n_cols, col_chunk):
        r = _mm(hb, w_ref[:, c0:c0 + col_chunk])
        if c0 < n_attn:
            attn_ref[0, :, c0:c0 + col_chunk] = r.astype(attn_ref.dtype)
        else:
            hg_ref[0, :, c0 - n_attn:c0 - n_attn + col_chunk] = r


def _in_projection(x, g, sc, sh, w_in_bf16, n_attn, tm):
    B, S, D = x.shape
    n_cols = w_in_bf16.shape[1]
    vec = pl.BlockSpec((1, 1, D), lambda b, i: (b, 0, 0))
    return pl.pallas_call(
        functools.partial(_inproj_kernel, n_attn=n_attn, col_chunk=512),
        grid=(B, S // tm),
        in_specs=[pl.BlockSpec((1, tm, D), lambda b, i: (b, i, 0)),
                  pl.BlockSpec((1, D), lambda b, i: (0, 0)),
                  vec, vec,
                  pl.BlockSpec((D, n_cols), lambda b, i: (0, 0))],
        out_specs=[pl.BlockSpec((1, tm, n_attn), lambda b, i: (b, i, 0)),
                   pl.BlockSpec((1, tm, n_cols - n_attn), lambda b, i: (b, i, 0))],
        out_shape=[jax.ShapeDtypeStruct((B, S, n_attn), jnp.bfloat16),
                   jax.ShapeDtypeStruct((B, S, n_cols - n_attn), jnp.float32)],
        compiler_params=pltpu.CompilerParams(
            dimension_semantics=("parallel", "parallel"), vmem_limit_bytes=VMEM_LIMIT),
        name="prenorm_inproj",
    )(x, g.reshape(1, D), sc, sh, w_in_bf16)


def _fold_lanes(x, op):
    out = x[:, 0:HEAD_W]
    for i in range(1, x.shape[1] // HEAD_W):
        out = op(out, x[:, i * HEAD_W:(i + 1) * HEAD_W])
    return out


def _attn_kernel(q_ref, k_ref, v_ref, lam_ref, g_ref, o_ref, s_scr, mx_scr, l_scr, acc_scr, *, tq, lam_init):
    qi = pl.program_id(2)
    q = q_ref[0]
    lane = lax.broadcasted_iota(jnp.int32, q.shape, 1)
    qs = q * jnp.asarray(1.0 / math.sqrt(DIFF_QK), q.dtype)
    zero = jnp.zeros_like(qs)
    q2 = jnp.concatenate([jnp.where(lane < DIFF_QK, qs, zero), jnp.where(lane >= DIFF_QK, qs, zero)], axis=0)
    n_rep = tq // HEAD_W

    mx_scr[...] = jnp.full(mx_scr.shape, _NEG_INF, jnp.float32)

    def scores(j, carry):
        s = _mm_nt(q2, k_ref[0, pl.ds(pl.multiple_of(j * tq, tq), tq), :]) * LOG2E
        s_scr[j] = s
        mx_scr[...] = jnp.maximum(mx_scr[...], _fold_lanes(s, jnp.maximum))
        return carry

    def score_pair(jj, carry):
        scores(2 * jj, carry)
        return scores(2 * jj + 1, carry)

    n_pair = qi // 2
    lax.fori_loop(0, n_pair, score_pair, 0)
    lax.fori_loop(2 * n_pair, qi, scores, 0)
    s = _mm_nt(q2, k_ref[0, pl.ds(pl.multiple_of(qi * tq, tq), tq), :]) * LOG2E
    row = lax.broadcasted_iota(jnp.int32, s.shape, 0) & (tq - 1)
    col = lax.broadcasted_iota(jnp.int32, s.shape, 1)
    s = jnp.where(col <= row, s, _NEG_INF)
    s_scr[qi] = s
    m = jnp.max(jnp.maximum(mx_scr[...], _fold_lanes(s, jnp.maximum)), axis=1, keepdims=True)
    mx_scr[...] = jnp.broadcast_to(m, mx_scr.shape)
    l_scr[...] = jnp.zeros(l_scr.shape, jnp.float32)
    acc_scr[...] = jnp.zeros(acc_scr.shape, jnp.float32)

    def contract(j, carry):
        mb = mx_scr[...]
        p = jnp.exp2(s_scr[j] - jnp.concatenate([mb] * n_rep, axis=1))
        l_scr[...] += _fold_lanes(p, jnp.add)
        vb = v_ref[0, pl.ds(pl.multiple_of(j * tq, tq), tq), :]
        acc_scr[...] += _mm(p.astype(vb.dtype), vb)
        return carry

    def contract_pair(jj, carry):
        contract(2 * jj, carry)
        return contract(2 * jj + 1, carry)

    n_pair = (qi + 1) // 2
    lax.fori_loop(0, n_pair, contract_pair, 0)
    lax.fori_loop(2 * n_pair, qi + 1, contract, 0)

    lq = lam_ref[...]
    lam = (jnp.exp(jnp.sum(lq[0:1] * lq[1:2], axis=1, keepdims=True))
           - jnp.exp(jnp.sum(lq[2:3] * lq[3:4], axis=1, keepdims=True)) + lam_init)
    o = acc_scr[...] / jnp.sum(l_scr[...], axis=1, keepdims=True)
    o = o[0:tq] - lam * o[tq:2 * tq]
    o = _rms(o, g_ref[...]) * (1.0 - lam_init)
    o_ref[0] = o.astype(o_ref.dtype)


def _diff_attention(attn_in, lam_qk, g_diff_sub, lam_init, tq):
    B, S, _ = attn_in.shape
    H = DIFF_HEADS
    assert tq % HEAD_W == 0 and tq & (tq - 1) == 0, tq
    kv_spec = lambda off: pl.BlockSpec((1, S, HEAD_W), lambda b, h, i: (b, 0, off + h))
    return pl.pallas_call(
        functools.partial(_attn_kernel, tq=tq, lam_init=lam_init),
        grid=(B, H, S // tq),
        in_specs=[pl.BlockSpec((1, tq, HEAD_W), lambda b, h, i: (b, i, h)),
                  kv_spec(H), kv_spec(2 * H),
                  pl.BlockSpec(lam_qk.shape, lambda b, h, i: (0, 0)),
                  pl.BlockSpec((1, HEAD_W), lambda b, h, i: (0, 0))],
        out_specs=pl.BlockSpec((1, tq, HEAD_W), lambda b, h, i: (b, i, h)),
        out_shape=jax.ShapeDtypeStruct((B, S, H * HEAD_W), jnp.bfloat16),
        scratch_shapes=[pltpu.VMEM((S // tq, 2 * tq, tq), jnp.float32),
                        pltpu.VMEM((2 * tq, HEAD_W), jnp.float32),
                        pltpu.VMEM((2 * tq, HEAD_W), jnp.float32),
                        pltpu.VMEM((2 * tq, HEAD_W), jnp.float32)],
        compiler_params=pltpu.CompilerParams(
            dimension_semantics=("parallel", "parallel", "arbitrary"), vmem_limit_bytes=VMEM_LIMIT),
        name="diff_attention",
    )(attn_in, attn_in, attn_in, lam_qk, g_diff_sub.reshape(1, HEAD_W))


def _hgrn_kernel(hq_ref, hf_ref, hi_ref, hgate_ref, lbt_ref, g_ref, o_ref, state_ref, *, layer, n_chunks):
    C, SUB = HGRN_CHUNK, HGRN_SUB
    n_sub = C // SUB
    n_heads = state_ref.shape[0]

    @pl.when(pl.program_id(1) == 0)
    def _():
        state_ref[...] = jnp.zeros(state_ref.shape, jnp.float32)

    th = lbt_ref[...]
    e = jnp.exp(th - jnp.max(th, axis=0, keepdims=True))
    lb_all = jnp.sum(e[0:layer + 1], axis=0, keepdims=True) / jnp.sum(e, axis=0, keepdims=True)

    r_io = lax.broadcasted_iota(jnp.int32, (C, C), 0)
    c_io = lax.broadcasted_iota(jnp.int32, (C, C), 1)
    tril = (c_io <= r_io).astype(jnp.float32)
    tloc = lax.broadcasted_iota(jnp.int32, (C, HEAD_W), 0) % SUB
    sub_col = lax.broadcasted_iota(jnp.int32, (SUB, C), 1)
    g_out = g_ref[...]

    def group_rows(x, s):
        return jnp.concatenate(
            [jnp.broadcast_to(x[i * SUB + s:i * SUB + s + 1, :], (SUB, HEAD_W)) for i in range(n_sub)], axis=0)

    def head_chunk(rows, hd):
        cols = slice(hd * HEAD_W, (hd + 1) * HEAD_W)
        lb = lb_all[:, cols]
        hq = hq_ref[0, rows, cols]
        f = lb + (1.0 - lb) * _sigmoid(hf_ref[0, rows, cols])
        glog = jnp.log(f)
        kk = 1.0 - f
        q = hq * _sigmoid(hq)
        v = hi_ref[0, rows, cols]
        b = _mm(tril, glog, _HI)
        st = state_ref[hd]

        bf = jnp.bfloat16
        vb = v.astype(bf)
        o = _mm_nt((q * jnp.exp(b)).astype(bf), st.astype(bf))

        o_sub = [jnp.zeros((SUB, HEAD_W), jnp.float32)]
        for i in range(1, n_sub):
            beta = b[i * SUB:i * SUB + 1, :]
            qt = q[i * SUB:(i + 1) * SUB, :] * jnp.exp(b[i * SUB:(i + 1) * SUB, :] - beta)
            kt = kk * jnp.exp(jnp.minimum(beta - b, 0.0))
            p = _mm_nt(qt.astype(bf), kt.astype(bf))
            p = jnp.where(sub_col < i * SUB, p, 0.0)
            o_sub.append(_mm(p.astype(bf), vb))
        o = o + jnp.concatenate(o_sub, axis=0)

        for s in range(SUB):
            b_s, k_s, v_s = group_rows(b, s), group_rows(kk, s), group_rows(v, s)
            w = q * k_s * jnp.exp(jnp.where(tloc >= s, b - b_s, _NEG_INF))
            o = o + jnp.sum(w, axis=1, keepdims=True) * v_s

        b_last = b[C - 1:C, :]
        kdec = kk * jnp.exp(b_last - b)
        state_ref[hd] = st * jnp.exp(b_last) + _mm_tn(vb, kdec.astype(bf))

        hgate = hgate_ref[0, rows, cols]
        y = _rms(o, g_out) * (hgate * _sigmoid(hgate))
        o_ref[0, rows, cols] = y.astype(o_ref.dtype)

    per_trip = 2 if n_chunks % 2 == 0 else 1

    def chunk(ci, carry):
        for sub in range(per_trip):
            rows = pl.ds(pl.multiple_of((ci * per_trip + sub) * C, C), C)
            for hd in range(n_heads):
                head_chunk(rows, hd)
        return carry

    lax.fori_loop(0, n_chunks // per_trip, chunk, 0)


def _hgrn(hg_in, lb_theta, g_hgrn_out, layer, sb):
    B, S, _ = hg_in.shape
    H = HGRN_HEADS
    W = H * HEAD_W
    spec = lambda off: pl.BlockSpec((1, sb, W), lambda b, i: (b, i, off))
    n_slots = lb_theta.shape[0]
    return pl.pallas_call(
        functools.partial(_hgrn_kernel, layer=layer, n_chunks=sb // HGRN_CHUNK),
        grid=(B, S // sb),
        in_specs=[spec(0), spec(1), spec(2), spec(3),
                  pl.BlockSpec((n_slots, W), lambda b, i: (0, 0)),
                  pl.BlockSpec((1, HEAD_W), lambda b, i: (0, 0))],
        out_specs=pl.BlockSpec((1, sb, W), lambda b, i: (b, i, 0)),
        out_shape=jax.ShapeDtypeStruct((B, S, W), jnp.bfloat16),
        scratch_shapes=[pltpu.VMEM((H, HEAD_W, HEAD_W), jnp.float32)],
        compiler_params=pltpu.CompilerParams(
            dimension_semantics=("parallel", "arbitrary"), vmem_limit_bytes=VMEM_LIMIT),
        name="hgrn2",
    )(hg_in, hg_in, hg_in, hg_in, lb_theta, g_hgrn_out.reshape(1, HEAD_W))


def _outproj_kernel(ao_ref, ho_ref, x_ref, wo_ref, gpost_ref, gt_ref, gpre_ref, sc_ref, sh_ref, wq_ref,
                    x1_ref, h2_ref, q_ref):
    n_a = ao_ref.shape[2]
    y = _mm(ao_ref[0], wo_ref[0:n_a, :]) + _mm(ho_ref[0], wo_ref[n_a:, :])
    x1 = x_ref[0] + gt_ref[0] * _rms(y, gpost_ref[...])
    x1_ref[0] = x1
    h2 = _rms(x1, gpre_ref[...]) * (1.0 + sc_ref[0]) + sh_ref[0]
    h2_ref[0] = h2
    q_ref[0] = _mm(h2.astype(jnp.bfloat16), wq_ref[...]).astype(q_ref.dtype)


def _out_projection(ao, ho, x, w_out_bf16, g_post, gt1, g_pre, sc2, sh2, w_pq_bf16, tm):
    B, S, D = x.shape
    n_a, n_h, n_q = ao.shape[2], ho.shape[2], w_pq_bf16.shape[1]
    vec = pl.BlockSpec((1, 1, D), lambda b, i: (b, 0, 0))
    par = pl.BlockSpec((1, D), lambda b, i: (0, 0))
    row = lambda n: pl.BlockSpec((1, tm, n), lambda b, i: (b, i, 0))
    return pl.pallas_call(
        _outproj_kernel,
        grid=(B, S // tm),
        in_specs=[row(n_a), row(n_h), row(D),
                  pl.BlockSpec((n_a + n_h, D), lambda b, i: (0, 0)),
                  par, vec, par, vec, vec,
                  pl.BlockSpec((D, n_q), lambda b, i: (0, 0))],
        out_specs=[row(D), row(D), row(n_q)],
        out_shape=[jax.ShapeDtypeStruct((B, S, D), jnp.float32),
                   jax.ShapeDtypeStruct((B, S, D), jnp.float32),
                   jax.ShapeDtypeStruct((B, S, n_q), jnp.bfloat16)],
        compiler_params=pltpu.CompilerParams(
            dimension_semantics=("parallel", "parallel"), vmem_limit_bytes=VMEM_LIMIT),
        name="outproj_norms_peerq",
    )(ao, ho, x, w_out_bf16, g_post.reshape(1, D), gt1, g_pre.reshape(1, D), sc2, sh2, w_pq_bf16)


def _pair_list():
    return [(a, b) for a in range(PEER_TOPK) for b in range(PEER_TOPK) if (a + 1) * (b + 1) <= PEER_TOPK]


def _topk_kernel(q_ref, keys_ref, idx_ref, gate_ref, v_scr, i_scr, cand_scr, cidx_scr, ts_scr, sel_scr):
    K = PEER_TOPK
    tb = q_ref.shape[0]
    pairs = _pair_list()
    n_cand = cand_scr.shape[0]
    kio = lax.broadcasted_iota(jnp.int32, (N_KEYS, tb), 0).astype(jnp.float32)
    pio = lax.broadcasted_iota(jnp.int32, (n_cand, tb), 0).astype(jnp.float32)

    for h in range(PEER_HEADS):
        for j in range(2):
            c0 = (h * 2 + j) * N_KEYS
            s = _mm_nt(keys_ref[h, j], q_ref[:, c0:c0 + N_KEYS])
            for r in range(K):
                m = jnp.max(s, axis=0, keepdims=True)
                am = jnp.min(jnp.where(s == m, kio, float(N_KEYS)), axis=0, keepdims=True)
                v_scr[j, r:r + 1, :] = m
                i_scr[j, r:r + 1, :] = am
                s = jnp.where(kio == am, _NEG_INF, s)
        v1, v2 = v_scr[0], v_scr[1]
        i1, i2 = i_scr[0], i_scr[1]
        cand_scr[...] = jnp.full(cand_scr.shape, _NEG_INF, jnp.float32)
        cidx_scr[...] = jnp.zeros(cidx_scr.shape, jnp.float32)
        off = 0
        for a in range(K):
            nb = sum(1 for (aa, _) in pairs if aa == a)
            cand_scr[off:off + nb, :] = v1[a:a + 1, :] + v2[0:nb, :]
            cidx_scr[off:off + nb, :] = i1[a:a + 1, :] * float(N_KEYS) + i2[0:nb, :]
            off += nb
        cand = cand_scr[...]
        cidx = cidx_scr[...]
        for r in range(K):
            m = jnp.max(cand, axis=0, keepdims=True)
            pos = jnp.min(jnp.where(cand == m, pio, float(n_cand)), axis=0, keepdims=True)
            hit = pio == pos
            ts_scr[r:r + 1, :] = m
            sel_scr[h * K + r:h * K + r + 1, :] = jnp.sum(jnp.where(hit, cidx, 0.0), axis=0, keepdims=True)
            cand = jnp.where(hit, _NEG_INF, cand)
        ts = ts_scr[...]
        e = jnp.exp(ts - ts[0:1, :])
        gate_ref[h * K:(h + 1) * K, :] = e / jnp.sum(e, axis=0, keepdims=True)
    idx_ref[...] = sel_scr[...].T.astype(jnp.int32)


def _peer_topk(q, sub_keys_bf16, tb):
    T = q.shape[0]
    n_sel = PEER_HEADS * PEER_TOPK
    n_cand = -(-len(_pair_list()) // 8) * 8
    return pl.pallas_call(
        _topk_kernel,
        grid=(T // tb,),
        in_specs=[pl.BlockSpec((tb, q.shape[1]), lambda i: (i, 0)),
                  pl.BlockSpec(sub_keys_bf16.shape, lambda i: (0, 0, 0, 0))],
        out_specs=[pl.BlockSpec((tb, n_sel), lambda i: (i, 0)),
                   pl.BlockSpec((n_sel, tb), lambda i: (0, i))],
        out_shape=[jax.ShapeDtypeStruct((T, n_sel), jnp.int32),
                   jax.ShapeDtypeStruct((n_sel, T), jnp.float32)],
        scratch_shapes=[pltpu.VMEM((2, PEER_TOPK, tb), jnp.float32),
                        pltpu.VMEM((2, PEER_TOPK, tb), jnp.float32),
                        pltpu.VMEM((n_cand, tb), jnp.float32),
                        pltpu.VMEM((n_cand, tb), jnp.float32),
                        pltpu.VMEM((PEER_TOPK, tb), jnp.float32),
                        pltpu.VMEM((n_sel, tb), jnp.float32)],
        compiler_params=pltpu.CompilerParams(
            dimension_semantics=("parallel",), vmem_limit_bytes=VMEM_LIMIT),
        name="peer_topk",
    )(q, sub_keys_bf16)


def _peer_kernel(idx_ref, h_ref, gate_ref, x1_ref, gt_ref, g_ref, uv_ref, o_ref, buf, sem, y_scr, *, n_slots):
    tb, D = h_ref.shape
    n_sel = idx_ref.shape[1]
    n_c = D // HEAD_W
    n_t = 2 * n_c
    pitch = buf.shape[1] // n_sel

    def row_copy(t, k, slot):
        e = idx_ref[t, k]
        return pltpu.make_async_copy(uv_ref.at[e], buf.at[slot, pl.ds(k * pitch, n_t), :], sem.at[slot])

    def tile(slot, c):
        return buf[slot, pl.ds(c, n_sel, stride=pitch), :]

    def start_rows(t, slot, k0, k1):
        for k in range(k0, k1):
            row_copy(t, k, slot).start(priority=k % 2)

    def wait_slot(slot):
        done = buf.at[slot, pl.ds(0, n_sel * n_t), :]
        pltpu.make_async_copy(done, done, sem.at[slot]).wait()

    n_ahead = n_slots - 1
    for t in range(n_ahead):
        start_rows(t, t, 0, n_sel)

    lane = lax.broadcasted_iota(jnp.int32, gate_ref.shape, 1)
    per_piece = n_sel // n_t

    def token(t, slot, prefetch):
        def issue(piece):
            if prefetch:
                start_rows(t + n_ahead, (slot + n_ahead) % n_slots, piece * per_piece, (piece + 1) * per_piece)

        wait_slot(slot)
        h = h_ref[pl.ds(t, 1), :]
        p = None
        for c in range(n_c):
            pc = tile(slot, c) * h[:, c * HEAD_W:(c + 1) * HEAD_W]
            p = pc if p is None else p + pc
            issue(c)
        a = jnp.sum(p, axis=1, keepdims=True)
        gcol = jnp.sum(jnp.where(lane == t, gate_ref[...], 0.0), axis=1, keepdims=True)
        w = gcol * (0.5 * a * (1.0 + lax.erf(a * (1.0 / math.sqrt(2.0)))))
        y = []
        for c in range(n_c):
            y.append(jnp.sum(w * tile(slot, n_c + c), axis=0, keepdims=True))
            issue(n_c + c)
        y_scr[pl.ds(t, 1), :] = jnp.concatenate(y, axis=1)

    n_main = tb - n_ahead
    n_groups = n_main // n_slots

    def group(g, carry):
        for j in range(n_slots):
            token(g * n_slots + j, j, True)
        return carry

    lax.fori_loop(0, n_groups, group, 0)
    for t in range(n_groups * n_slots, tb):
        token(t, t % n_slots, t < n_main)
    o_ref[...] = x1_ref[...] + gt_ref[0] * _rms(y_scr[...], g_ref[...])


def _pack_experts(u, v):
    E, D = u.shape
    return jnp.concatenate([u, v], axis=1).reshape(E, 2 * D // HEAD_W, HEAD_W)


def _peer_mix(idx, h2, gate_t, x1, gt2, g_post, uv, seq_len, tb, n_slots):
    T, D = h2.shape
    n_sel = idx.shape[1]
    blocks_per_seq = seq_len // tb
    rows = pl.BlockSpec((tb, D), lambda i: (i, 0))
    return pl.pallas_call(
        functools.partial(_peer_kernel, n_slots=n_slots),
        grid=(T // tb,),
        in_specs=[pl.BlockSpec((tb, n_sel), lambda i: (i, 0), memory_space=pltpu.SMEM),
                  rows,
                  pl.BlockSpec((n_sel, tb), lambda i: (0, i)),
                  rows,
                  pl.BlockSpec((1, 1, D), lambda i: (i // blocks_per_seq, 0, 0)),
                  pl.BlockSpec((1, D), lambda i: (0, 0)),
                  pl.BlockSpec(memory_space=pl.ANY)],
        out_specs=rows,
        out_shape=jax.ShapeDtypeStruct((T, D), jnp.float32),
        scratch_shapes=[pltpu.VMEM((n_slots, n_sel * (2 * D // HEAD_W + 1), HEAD_W), uv.dtype),
                        pltpu.SemaphoreType.DMA((n_slots,)),
                        pltpu.VMEM((tb, D), jnp.float32)],
        compiler_params=pltpu.CompilerParams(
            dimension_semantics=("arbitrary",), vmem_limit_bytes=VMEM_LIMIT),
        name="peer_gather_mix",
    )(idx, h2, gate_t, x1, gt2, g_post.reshape(1, D), uv)


def _pick(n, pref):
    t = min(n, pref)
    assert n % t == 0, (n, pref)
    return t


def kernel(x, c, w_ada, b_ada, g_pre_mix, g_post_mix, g_pre_ffn, g_post_ffn, w_in, lam_qk, g_diff_sub,
           lb_theta, g_hgrn_out, w_out, w_pq, sub_keys, expert_u, expert_v):
    B, S, D = x.shape
    depth = w_in.shape[0]
    n_attn = 3 * DIFF_HEADS * HEAD_W
    tm = _pick(S, 512)
    tq = _pick(S, 512)
    sb = _pick(S, 512)
    tb = _pick(S, 128)
    bf = jnp.bfloat16
    for l in range(depth):
        lam_init = 0.8 - 0.6 * math.exp(-0.3 * l)
        mod = _modulation(c, w_ada[l], b_ada[l]).reshape(N_MOD, B, 1, D)
        sh1, sc1, gt1, sh2, sc2, gt2 = (mod[i] for i in range(N_MOD))
        attn_in, hg_in = _in_projection(x, g_pre_mix[l], sc1, sh1, w_in[l].astype(bf), n_attn, tm)
        ao = _diff_attention(attn_in, lam_qk[l], g_diff_sub[l], lam_init, tq)
        ho = _hgrn(hg_in, lb_theta, g_hgrn_out[l], l, sb)
        x1, h2, q = _out_projection(ao, ho, x, w_out[l].astype(bf), g_post_mix[l], gt1, g_pre_ffn[l],
                                    sc2, sh2, w_pq[l].astype(bf), tm)
        idx, gate_t = _peer_topk(q.reshape(B * S, -1), sub_keys[l].astype(bf), tb)
        uv = _pack_experts(expert_u[l], expert_v[l])
        out = _peer_mix(idx, h2.reshape(B * S, D), gate_t, x1.reshape(B * S, D), gt2, g_post_ffn[l], uv,
                        S, _pick(S, 256), n_slots=10)
        x = out.reshape(B, S, D)
    return x
```

```python
import functools
import math

import jax
import jax.numpy as jnp
from jax import lax
from jax.experimental import pallas as pl
from jax.experimental.pallas import tpu as pltpu

EPS = 1e-6
N_MOD = 6
DIFF_HEADS = 4
DIFF_QK = 64
HEAD_W = 128
HGRN_HEADS = 4
HGRN_CHUNK = 64
HGRN_SUB = 16
PEER_HEADS = 8
N_KEYS = 128
PEER_TOPK = 16
VMEM_LIMIT = 56 * 1024 * 1024

_HI = lax.Precision.HIGHEST
_NEG_INF = float("-inf")
LOG2E = 1.4426950408889634


def _dot(a, b, dims, precision=None):
    return lax.dot_general(a, b, (dims, ((), ())), precision=precision,
                           preferred_element_type=jnp.float32)


def _mm(a, b, precision=None):
    return _dot(a, b, ((1,), (0,)), precision)


def _mm_nt(a, b, precision=None):
    return _dot(a, b, ((1,), (1,)), precision)


def _mm_tn(a, b, precision=None):
    return _dot(a, b, ((0,), (0,)), precision)


def _rms(x, g):
    return x * lax.rsqrt(jnp.mean(x * x, axis=-1, keepdims=True) + EPS) * g


def _sigmoid(x):
    return 1.0 / (1.0 + jnp.exp(-x))


def _mod_kernel(c_ref, w_ref, b_ref, o_ref):
    c = c_ref[...]
    ca = c * _sigmoid(c)
    o_ref[0] = _mm(ca, w_ref[...], _HI) + b_ref[...]


def _modulation(c, w_ada, b_ada):
    B, D = c.shape
    return pl.pallas_call(
        _mod_kernel,
        grid=(N_MOD,),
        in_specs=[pl.BlockSpec((B, D), lambda j: (0, 0)),
                  pl.BlockSpec((D, D), lambda j: (0, j)),
                  pl.BlockSpec((1, D), lambda j: (0, j))],
        out_specs=pl.BlockSpec((1, B, D), lambda j: (j, 0, 0)),
        out_shape=jax.ShapeDtypeStruct((N_MOD, B, D), jnp.float32),
        compiler_params=pltpu.CompilerParams(vmem_limit_bytes=VMEM_LIMIT),
        name="adaln_mod",
    )(c, w_ada, b_ada.reshape(1, N_MOD * D))


def _inproj_kernel(x_ref, g_ref, sc_ref, sh_ref, w_ref, attn_ref, hg_ref, *, n_attn, col_chunk):
    x = x_ref[0]
    h = _rms(x, g_ref[...]) * (1.0 + sc_ref[0]) + sh_ref[0]
    hb = h.astype(jnp.bfloat16)
    n_cols = w_ref.shape[1]
    for c0 in range(0, n_cols, col_chunk):
        r = _mm(hb, w_ref[:, c0:c0 + col_chunk])
        if c0 < n_attn:
            attn_ref[0, :, c0:c0 + col_chunk] = r.astype(attn_ref.dtype)
        else:
            hg_ref[0, :, c0 - n_attn:c0 - n_attn + col_chunk] = r


def _in_projection(x, g, sc, sh, w_in_bf16, n_attn, tm):
    B, S, D = x.shape
    n_cols = w_in_bf16.shape[1]
    vec = pl.BlockSpec((1, 1, D), lambda b, i: (b, 0, 0))
    return pl.pallas_call(
        functools.partial(_inproj_kernel, n_attn=n_attn, col_chunk=512),
        grid=(B, S // tm),
        in_specs=[pl.BlockSpec((1, tm, D), lambda b, i: (b, i, 0)),
                  pl.BlockSpec((1, D), lambda b, i: (0, 0)),
                  vec, vec,
                  pl.BlockSpec((D, n_cols), lambda b, i: (0, 0))],
        out_specs=[pl.BlockSpec((1, tm, n_attn), lambda b, i: (b, i, 0)),
                   pl.BlockSpec((1, tm, n_cols - n_attn), lambda b, i: (b, i, 0))],
        out_shape=[jax.ShapeDtypeStruct((B, S, n_attn), jnp.bfloat16),
                   jax.ShapeDtypeStruct((B, S, n_cols - n_attn), jnp.float32)],
        compiler_params=pltpu.CompilerParams(
            dimension_semantics=("parallel", "parallel"), vmem_limit_bytes=VMEM_LIMIT),
        name="prenorm_inproj",
    )(x, g.reshape(1, D), sc, sh, w_in_bf16)


def _fold_lanes(x, op):
    out = x[:, 0:HEAD_W]
    for i in range(1, x.shape[1] // HEAD_W):
        out = op(out, x[:, i * HEAD_W:(i + 1) * HEAD_W])
    return out


def _attn_kernel(q_ref, k_ref, v_ref, lam_ref, g_ref, o_ref, s_scr, mx_scr, l_scr, acc_scr, *, tq, lam_init):
    qi = pl.program_id(2)
    q = q_ref[0]
    lane = lax.broadcasted_iota(jnp.int32, q.shape, 1)
    qs = q * jnp.asarray(1.0 / math.sqrt(DIFF_QK), q.dtype)
    zero = jnp.zeros_like(qs)
    q2 = jnp.concatenate([jnp.where(lane < DIFF_QK, qs, zero), jnp.where(lane >= DIFF_QK, qs, zero)], axis=0)
    n_rep = tq // HEAD_W

    mx_scr[...] = jnp.full(mx_scr.shape, _NEG_INF, jnp.float32)

    def scores(j, carry):
        s = _mm_nt(q2, k_ref[0, pl.ds(pl.multiple_of(j * tq, tq), tq), :]) * LOG2E
        s_scr[j] = s
        mx_scr[...] = jnp.maximum(mx_scr[...], _fold_lanes(s, jnp.maximum))
        return carry

    def score_pair(jj, carry):
        scores(2 * jj, carry)
        return scores(2 * jj + 1, carry)

    n_pair = qi // 2
    lax.fori_loop(0, n_pair, score_pair, 0)
    lax.fori_loop(2 * n_pair, qi, scores, 0)
    s = _mm_nt(q2, k_ref[0, pl.ds(pl.multiple_of(qi * tq, tq), tq), :]) * LOG2E
    row = lax.broadcasted_iota(jnp.int32, s.shape, 0) & (tq - 1)
    col = lax.broadcasted_iota(jnp.int32, s.shape, 1)
    s = jnp.where(col <= row, s, _NEG_INF)
    s_scr[qi] = s
    m = jnp.max(jnp.maximum(mx_scr[...], _fold_lanes(s, jnp.maximum)), axis=1, keepdims=True)
    mx_scr[...] = jnp.broadcast_to(m, mx_scr.shape)
    l_scr[...] = jnp.zeros(l_scr.shape, jnp.float32)
    acc_scr[...] = jnp.zeros(acc_scr.shape, jnp.float32)

    def contract(j, carry):
        mb = mx_scr[...]
        p = jnp.exp2(s_scr[j] - jnp.concatenate([mb] * n_rep, axis=1))
        l_scr[...] += _fold_lanes(p, jnp.add)
        vb = v_ref[0, pl.ds(pl.multiple_of(j * tq, tq), tq), :]
        acc_scr[...] += _mm(p.astype(vb.dtype), vb)
        return carry

    def contract_pair(jj, carry):
        contract(2 * jj, carry)
        return contract(2 * jj + 1, carry)

    n_pair = (qi + 1) // 2
    lax.fori_loop(0, n_pair, contract_pair, 0)
    lax.fori_loop(2 * n_pair, qi + 1, contract, 0)

    lq = lam_ref[...]
    lam = (jnp.exp(jnp.sum(lq[0:1] * lq[1:2], axis=1, keepdims=True))
           - jnp.exp(jnp.sum(lq[2:3] * lq[3:4], axis=1, keepdims=True)) + lam_init)
    o = acc_scr[...] / jnp.sum(l_scr[...], axis=1, keepdims=True)
    o = o[0:tq] - lam * o[tq:2 * tq]
    o = _rms(o, g_ref[...]) * (1.0 - lam_init)
    o_ref[0] = o.astype(o_ref.dtype)


def _diff_attention(attn_in, lam_qk, g_diff_sub, lam_init, tq):
    B, S, _ = attn_in.shape
    H = DIFF_HEADS
    assert tq % HEAD_W == 0 and tq & (tq - 1) == 0, tq
    kv_spec = lambda off: pl.BlockSpec((1, S, HEAD_W), lambda b, h, i: (b, 0, off + h))
    return pl.pallas_call(
        functools.partial(_attn_kernel, tq=tq, lam_init=lam_init),
        grid=(B, H, S // tq),
        in_specs=[pl.BlockSpec((1, tq, HEAD_W), lambda b, h, i: (b, i, h)),
                  kv_spec(H), kv_spec(2 * H),
                  pl.BlockSpec(lam_qk.shape, lambda b, h, i: (0, 0)),
                  pl.BlockSpec((1, HEAD_W), lambda b, h, i: (0, 0))],
        out_specs=pl.BlockSpec((1, tq, HEAD_W), lambda b, h, i: (b, i, h)),
        out_shape=jax.ShapeDtypeStruct((B, S, H * HEAD_W), jnp.bfloat16),
        scratch_shapes=[pltpu.VMEM((S // tq, 2 * tq, tq), jnp.float32),
                        pltpu.VMEM((2 * tq, HEAD_W), jnp.float32),
                        pltpu.VMEM((2 * tq, HEAD_W), jnp.float32),
                        pltpu.VMEM((2 * tq, HEAD_W), jnp.float32)],
        compiler_params=pltpu.CompilerParams(
            dimension_semantics=("parallel", "parallel", "arbitrary"), vmem_limit_bytes=VMEM_LIMIT),
        name="diff_attention",
    )(attn_in, attn_in, attn_in, lam_qk, g_diff_sub.reshape(1, HEAD_W))


def _hgrn_kernel(hq_ref, hf_ref, hi_ref, hgate_ref, lbt_ref, g_ref, o_ref, state_ref, *, layer, n_chunks):
    C, SUB = HGRN_CHUNK, HGRN_SUB
    n_sub = C // SUB
    n_heads = state_ref.shape[0]

    @pl.when(pl.program_id(1) == 0)
    def _():
        state_ref[...] = jnp.zeros(state_ref.shape, jnp.float32)

    th = lbt_ref[...]
    e = jnp.exp(th - jnp.max(th, axis=0, keepdims=True))
    lb_all = jnp.sum(e[0:layer + 1], axis=0, keepdims=True) / jnp.sum(e, axis=0, keepdims=True)

    r_io = lax.broadcasted_iota(jnp.int32, (C, C), 0)
    c_io = lax.broadcasted_iota(jnp.int32, (C, C), 1)
    tril = (c_io <= r_io).astype(jnp.float32).astype(jnp.bfloat16)
    tloc = lax.broadcasted_iota(jnp.int32, (C, HEAD_W), 0) % SUB
    sub_col = lax.broadcasted_iota(jnp.int32, (SUB, C), 1)
    g_out = g_ref[...]

    def group_rows(x, s):
        return jnp.concatenate(
            [jnp.broadcast_to(x[i * SUB + s:i * SUB + s + 1, :], (SUB, HEAD_W)) for i in range(n_sub)], axis=0)

    def head_chunk(rows, hd):
        cols = slice(hd * HEAD_W, (hd + 1) * HEAD_W)
        lb = lb_all[:, cols]
        hq = hq_ref[0, rows, cols]
        f = lb + (1.0 - lb) * _sigmoid(hf_ref[0, rows, cols])
        glog = jnp.log(f)
        kk = 1.0 - f
        q = hq * _sigmoid(hq)
        v = hi_ref[0, rows, cols]
        g1 = glog.astype(jnp.bfloat16)
        r1 = glog - g1.astype(jnp.float32)
        g2 = r1.astype(jnp.bfloat16)
        g3 = (r1 - g2.astype(jnp.float32)).astype(jnp.bfloat16)
        b = _mm(tril, g1) + _mm(tril, g2) + _mm(tril, g3)
        st = state_ref[hd]

        bf = jnp.bfloat16
        vb = v.astype(bf)
        o = _mm_nt((q * jnp.exp(b)).astype(bf), st.astype(bf))

        o_sub = [jnp.zeros((SUB, HEAD_W), jnp.float32)]
        for i in range(1, n_sub):
            beta = b[i * SUB:i * SUB + 1, :]
            qt = q[i * SUB:(i + 1) * SUB, :] * jnp.exp(b[i * SUB:(i + 1) * SUB, :] - beta)
            kt = kk * jnp.exp(jnp.minimum(beta - b, 0.0))
            p = _mm_nt(qt.astype(bf), kt.astype(bf))
            p = jnp.where(sub_col < i * SUB, p, 0.0)
            o_sub.append(_mm(p.astype(bf), vb))
        o = o + jnp.concatenate(o_sub, axis=0)

        for s in range(SUB):
            b_s, k_s, v_s = group_rows(b, s), group_rows(kk, s), group_rows(v, s)
            w = q * k_s * jnp.exp(jnp.where(tloc >= s, b - b_s, _NEG_INF))
            o = o + jnp.sum(w, axis=1, keepdims=True) * v_s

        b_last = b[C - 1:C, :]
        kdec = kk * jnp.exp(b_last - b)
        state_ref[hd] = st * jnp.exp(b_last) + _mm_tn(vb, kdec.astype(bf))

        hgate = hgate_ref[0, rows, cols]
        y = _rms(o, g_out) * (hgate * _sigmoid(hgate))
        o_ref[0, rows, cols] = y.astype(o_ref.dtype)

    per_trip = 2 if n_chunks % 2 == 0 else 1

    def chunk(ci, carry):
        for sub in range(per_trip):
            rows = pl.ds(pl.multiple_of((ci * per_trip + sub) * C, C), C)
            for hd in range(n_heads):
                head_chunk(rows, hd)
        return carry

    lax.fori_loop(0, n_chunks // per_trip, chunk, 0)


def _hgrn(hg_in, lb_theta, g_hgrn_out, layer, sb):
    B, S, _ = hg_in.shape
    H = HGRN_HEADS
    W = H * HEAD_W
    spec = lambda off: pl.BlockSpec((1, sb, W), lambda b, i: (b, i, off))
    n_slots = lb_theta.shape[0]
    return pl.pallas_call(
        functools.partial(_hgrn_kernel, layer=layer, n_chunks=sb // HGRN_CHUNK),
        grid=(B, S // sb),
        in_specs=[spec(0), spec(1), spec(2), spec(3),
                  pl.BlockSpec((n_slots, W), lambda b, i: (0, 0)),
                  pl.BlockSpec((1, HEAD_W), lambda b, i: (0, 0))],
        out_specs=pl.BlockSpec((1, sb, W), lambda b, i: (b, i, 0)),
        out_shape=jax.ShapeDtypeStruct((B, S, W), jnp.bfloat16),
        scratch_shapes=[pltpu.VMEM((H, HEAD_W, HEAD_W), jnp.float32)],
        compiler_params=pltpu.CompilerParams(
            dimension_semantics=("parallel", "arbitrary"), vmem_limit_bytes=VMEM_LIMIT),
        name="hgrn2",
    )(hg_in, hg_in, hg_in, hg_in, lb_theta, g_hgrn_out.reshape(1, HEAD_W))


def _outproj_kernel(ao_ref, ho_ref, x_ref, wo_ref, gpost_ref, gt_ref, gpre_ref, sc_ref, sh_ref, wq_ref,
                    x1_ref, h2_ref, q_ref):
    n_a = ao_ref.shape[2]
    y = _mm(ao_ref[0], wo_ref[0:n_a, :]) + _mm(ho_ref[0], wo_ref[n_a:, :])
    x1 = x_ref[0] + gt_ref[0] * _rms(y, gpost_ref[...])
    x1_ref[0] = x1
    h2 = _rms(x1, gpre_ref[...]) * (1.0 + sc_ref[0]) + sh_ref[0]
    h2_ref[0] = h2
    q_ref[0] = _mm(h2.astype(jnp.bfloat16), wq_ref[...]).astype(q_ref.dtype)


def _out_projection(ao, ho, x, w_out_bf16, g_post, gt1, g_pre, sc2, sh2, w_pq_bf16, tm):
    B, S, D = x.shape
    n_a, n_h, n_q = ao.shape[2], ho.shape[2], w_pq_bf16.shape[1]
    vec = pl.BlockSpec((1, 1, D), lambda b, i: (b, 0, 0))
    par = pl.BlockSpec((1, D), lambda b, i: (0, 0))
    row = lambda n: pl.BlockSpec((1, tm, n), lambda b, i: (b, i, 0))
    return pl.pallas_call(
        _outproj_kernel,
        grid=(B, S // tm),
        in_specs=[row(n_a), row(n_h), row(D),
                  pl.BlockSpec((n_a + n_h, D), lambda b, i: (0, 0)),
                  par, vec, par, vec, vec,
                  pl.BlockSpec((D, n_q), lambda b, i: (0, 0))],
        out_specs=[row(D), row(D), row(n_q)],
        out_shape=[jax.ShapeDtypeStruct((B, S, D), jnp.float32),
                   jax.ShapeDtypeStruct((B, S, D), jnp.float32),
                   jax.ShapeDtypeStruct((B, S, n_q), jnp.bfloat16)],
        compiler_params=pltpu.CompilerParams(
            dimension_semantics=("parallel", "parallel"), vmem_limit_bytes=VMEM_LIMIT),
        name="outproj_norms_peerq",
    )(ao, ho, x, w_out_bf16, g_post.reshape(1, D), gt1, g_pre.reshape(1, D), sc2, sh2, w_pq_bf16)


def _pair_list():
    return [(a, b) for a in range(PEER_TOPK) for b in range(PEER_TOPK) if (a + 1) * (b + 1) <= PEER_TOPK]


def _topk_kernel(q_ref, keys_ref, idx_ref, gate_ref, v_scr, i_scr, cand_scr, cidx_scr, ts_scr, sel_scr):
    K = PEER_TOPK
    tb = q_ref.shape[0]
    pairs = _pair_list()
    n_cand = cand_scr.shape[0]
    kio = lax.broadcasted_iota(jnp.int32, (N_KEYS, tb), 0).astype(jnp.float32)
    pio = lax.broadcasted_iota(jnp.int32, (n_cand, tb), 0).astype(jnp.float32)

    for h in range(PEER_HEADS):
        for j in range(2):
            c0 = (h * 2 + j) * N_KEYS
            s = _mm_nt(keys_ref[h, j], q_ref[:, c0:c0 + N_KEYS])
            for r in range(K):
                m = jnp.max(s, axis=0, keepdims=True)
                am = jnp.min(jnp.where(s == m, kio, float(N_KEYS)), axis=0, keepdims=True)
                v_scr[j, r:r + 1, :] = m
                i_scr[j, r:r + 1, :] = am
                s = jnp.where(kio == am, _NEG_INF, s)
        v1, v2 = v_scr[0], v_scr[1]
        i1, i2 = i_scr[0], i_scr[1]
        cand_scr[...] = jnp.full(cand_scr.shape, _NEG_INF, jnp.float32)
        cidx_scr[...] = jnp.zeros(cidx_scr.shape, jnp.float32)
        off = 0
        for a in range(K):
            nb = sum(1 for (aa, _) in pairs if aa == a)
            cand_scr[off:off + nb, :] = v1[a:a + 1, :] + v2[0:nb, :]
            cidx_scr[off:off + nb, :] = i1[a:a + 1, :] * float(N_KEYS) + i2[0:nb, :]
            off += nb
        cand = cand_scr[...]
        cidx = cidx_scr[...]
        for r in range(K):
            m = jnp.max(cand, axis=0, keepdims=True)
            pos = jnp.min(jnp.where(cand == m, pio, float(n_cand)), axis=0, keepdims=True)
            hit = pio == pos
            ts_scr[r:r + 1, :] = m
            sel_scr[h * K + r:h * K + r + 1, :] = jnp.sum(jnp.where(hit, cidx, 0.0), axis=0, keepdims=True)
            cand = jnp.where(hit, _NEG_INF, cand)
        ts = ts_scr[...]
        e = jnp.exp(ts - ts[0:1, :])
        gate_ref[h * K:(h + 1) * K, :] = e / jnp.sum(e, axis=0, keepdims=True)
    idx_ref[...] = sel_scr[...].T.astype(jnp.int32)


def _peer_topk(q, sub_keys_bf16, tb):
    T = q.shape[0]
    n_sel = PEER_HEADS * PEER_TOPK
    n_cand = -(-len(_pair_list()) // 8) * 8
    return pl.pallas_call(
        _topk_kernel,
        grid=(T // tb,),
        in_specs=[pl.BlockSpec((tb, q.shape[1]), lambda i: (i, 0)),
                  pl.BlockSpec(sub_keys_bf16.shape, lambda i: (0, 0, 0, 0))],
        out_specs=[pl.BlockSpec((tb, n_sel), lambda i: (i, 0)),
                   pl.BlockSpec((n_sel, tb), lambda i: (0, i))],
        out_shape=[jax.ShapeDtypeStruct((T, n_sel), jnp.int32),
                   jax.ShapeDtypeStruct((n_sel, T), jnp.float32)],
        scratch_shapes=[pltpu.VMEM((2, PEER_TOPK, tb), jnp.float32),
                        pltpu.VMEM((2, PEER_TOPK, tb), jnp.float32),
                        pltpu.VMEM((n_cand, tb), jnp.float32),
                        pltpu.VMEM((n_cand, tb), jnp.float32),
                        pltpu.VMEM((PEER_TOPK, tb), jnp.float32),
                        pltpu.VMEM((n_sel, tb), jnp.float32)],
        compiler_params=pltpu.CompilerParams(
            dimension_semantics=("parallel",), vmem_limit_bytes=VMEM_LIMIT),
        name="peer_topk",
    )(q, sub_keys_bf16)


def _peer_kernel(idx_ref, h_ref, gate_ref, x1_ref, gt_ref, g_ref, uv_ref, o_ref, buf, sem, y_scr, *, n_slots):
    tb, D = h_ref.shape
    n_sel = idx_ref.shape[1]
    n_c = D // HEAD_W
    n_t = 2 * n_c
    pitch = buf.shape[1] // n_sel

    def row_copy(t, k, slot):
        e = idx_ref[t, k]
        return pltpu.make_async_copy(uv_ref.at[e], buf.at[slot, pl.ds(k * pitch, n_t), :], sem.at[slot])

    def tile(slot, c):
        return buf[slot, pl.ds(c, n_sel, stride=pitch), :]

    def start_rows(t, slot, k0, k1):
        for k in range(k0, k1):
            row_copy(t, k, slot).start(priority=k % 2)

    def wait_slot(slot):
        done = buf.at[slot, pl.ds(0, n_sel * n_t), :]
        pltpu.make_async_copy(done, done, sem.at[slot]).wait()

    n_ahead = n_slots - 1
    for t in range(n_ahead):
        start_rows(t, t, 0, n_sel)

    lane = lax.broadcasted_iota(jnp.int32, gate_ref.shape, 1)
    per_piece = n_sel // n_t

    def token(t, slot, prefetch):
        def issue(piece):
            if prefetch:
                start_rows(t + n_ahead, (slot + n_ahead) % n_slots, piece * per_piece, (piece + 1) * per_piece)

        wait_slot(slot)
        h = h_ref[pl.ds(t, 1), :]
        p = None
        for c in range(n_c):
            pc = tile(slot, c) * h[:, c * HEAD_W:(c + 1) * HEAD_W]
            p = pc if p is None else p + pc
            issue(c)
        a = jnp.sum(p, axis=1, keepdims=True)
        gcol = jnp.sum(jnp.where(lane == t, gate_ref[...], 0.0), axis=1, keepdims=True)
        w = gcol * (0.5 * a * (1.0 + lax.erf(a * (1.0 / math.sqrt(2.0)))))
        y = []
        for c in range(n_c):
            y.append(jnp.sum(w * tile(slot, n_c + c), axis=0, keepdims=True))
            issue(n_c + c)
        y_scr[pl.ds(t, 1), :] = jnp.concatenate(y, axis=1)

    n_main = tb - n_ahead
    n_groups = n_main // n_slots

    def group(g, carry):
        for j in range(n_slots):
            token(g * n_slots + j, j, True)
        return carry

    lax.fori_loop(0, n_groups, group, 0)
    for t in range(n_groups * n_slots, tb):
        token(t, t % n_slots, t < n_main)
    o_ref[...] = x1_ref[...] + gt_ref[0] * _rms(y_scr[...], g_ref[...])


def _pack_experts(u, v):
    E, D = u.shape
    return jnp.concatenate([u, v], axis=1).reshape(E, 2 * D // HEAD_W, HEAD_W)


def _peer_mix(idx, h2, gate_t, x1, gt2, g_post, uv, seq_len, tb, n_slots):
    T, D = h2.shape
    n_sel = idx.shape[1]
    blocks_per_seq = seq_len // tb
    rows = pl.BlockSpec((tb, D), lambda i: (i, 0))
    return pl.pallas_call(
        functools.partial(_peer_kernel, n_slots=n_slots),
        grid=(T // tb,),
        in_specs=[pl.BlockSpec((tb, n_sel), lambda i: (i, 0), memory_space=pltpu.SMEM),
                  rows,
                  pl.BlockSpec((n_sel, tb), lambda i: (0, i)),
                  rows,
                  pl.BlockSpec((1, 1, D), lambda i: (i // blocks_per_seq, 0, 0)),
                  pl.BlockSpec((1, D), lambda i: (0, 0)),
                  pl.BlockSpec(memory_space=pl.ANY)],
        out_specs=rows,
        out_shape=jax.ShapeDtypeStruct((T, D), jnp.float32),
        scratch_shapes=[pltpu.VMEM((n_slots, n_sel * (2 * D // HEAD_W + 1), HEAD_W), uv.dtype),
                        pltpu.SemaphoreType.DMA((n_slots,)),
                        pltpu.VMEM((tb, D), jnp.float32)],
        compiler_params=pltpu.CompilerParams(
            dimension_semantics=("arbitrary",), vmem_limit_bytes=VMEM_LIMIT),
        name="peer_gather_mix",
    )(idx, h2, gate_t, x1, gt2, g_post.reshape(1, D), uv)


def _pick(n, pref):
    t = min(n, pref)
    assert n % t == 0, (n, pref)
    return t


def kernel(x, c, w_ada, b_ada, g_pre_mix, g_post_mix, g_pre_ffn, g_post_ffn, w_in, lam_qk, g_diff_sub,
           lb_theta, g_hgrn_out, w_out, w_pq, sub_keys, expert_u, expert_v):
    B, S, D = x.shape
    depth = w_in.shape[0]
    n_attn = 3 * DIFF_HEADS * HEAD_W
    tm = _pick(S, 512)
    tq = _pick(S, 512)
    sb = _pick(S, 512)
    tb = _pick(S, 128)
    bf = jnp.bfloat16
    for l in range(depth):
        lam_init = 0.8 - 0.6 * math.exp(-0.3 * l)
        mod = _modulation(c, w_ada[l], b_ada[l]).reshape(N_MOD, B, 1, D)
        sh1, sc1, gt1, sh2, sc2, gt2 = (mod[i] for i in range(N_MOD))
        attn_in, hg_in = _in_projection(x, g_pre_mix[l], sc1, sh1, w_in[l].astype(bf), n_attn, tm)
        ao = _diff_attention(attn_in, lam_qk[l], g_diff_sub[l], lam_init, tq)
        ho = _hgrn(hg_in, lb_theta, g_hgrn_out[l], l, sb)
        x1, h2, q = _out_projection(ao, ho, x, w_out[l].astype(bf), g_post_mix[l], gt1, g_pre_ffn[l],
                                    sc2, sh2, w_pq[l].astype(bf), tm)
        idx, gate_t = _peer_topk(q.reshape(B * S, -1), sub_keys[l].astype(bf), tb)
        uv = _pack_experts(expert_u[l], expert_v[l])
        out = _peer_mix(idx, h2.reshape(B * S, D), gate_t, x1.reshape(B * S, D), gt2, g_post_ffn[l], uv,
                        S, _pick(S, 256), n_slots=10)
        x = out.reshape(B, S, D)
    return x
```
